```python
import jax, jax.numpy as jnp
from jax import lax
import numpy as np


D_MODEL = 2048
BATCH = 2
SEQ = 4096
DEPTH = 1

D_FF = 5632
A_HEADS = 8
A_HEAD_DIM = 128
Q_LORA = 512
KV_LORA = 256
IDX_HEADS = 16
IDX_DIM = 128
TOPK_MAX = 256
Q_BLOCK = 128
B_HEADS = 8
B_HEAD_DIM = 128
CONV_WIDTH = 4
CHUNK = 64
B_WIDTH = B_HEADS * B_HEAD_DIM
D_MIX = A_HEADS * A_HEAD_DIM + B_WIDTH
IN_SPLITS = (Q_LORA, KV_LORA, IDX_DIM, IDX_HEADS, B_WIDTH, B_WIDTH, B_WIDTH, B_WIDTH, B_HEADS, B_HEADS)
D_IN = sum(IN_SPLITS)
DEEPNORM_ALPHA = (2.0 * DEPTH) ** 0.25
DEEPNORM_BETA = (8.0 * DEPTH) ** -0.25
EPS = 1e-6

kernel_name = 'hybrid_dsa_gdn_macaron_block'


def layer_norm(x, g, b):
    xf = x.astype(jnp.float32)
    mu = jnp.mean(xf, -1, keepdims=True)
    xc = xf - mu
    var = jnp.mean(xc * xc, -1, keepdims=True)
    return (xc * lax.rsqrt(var + EPS) * g.astype(jnp.float32) + b.astype(jnp.float32)).astype(x.dtype)


def rms_norm(x, g):
    xf = x.astype(jnp.float32)
    y = xf * lax.rsqrt(jnp.mean(xf * xf, -1, keepdims=True) + EPS)
    return (y * g.astype(jnp.float32)).astype(x.dtype)


def l2_norm(x):
    return x * lax.rsqrt(jnp.sum(x * x, -1, keepdims=True) + EPS)


def swiglu(x, w_gate, w_up, w_down):
    return (jax.nn.silu(x @ w_gate) * (x @ w_up)) @ w_down


def causal_depthwise_conv(x, w):
    c = x.shape[-1]
    return lax.conv_general_dilated(x, w.astype(x.dtype)[:, None, :], window_strides=(1,),
                                    padding=[(CONV_WIDTH - 1, 0)],
                                    dimension_numbers=('NWC', 'WIO', 'NWC'),
                                    feature_group_count=c)


def dsa_attention(c_q, c_kv, k_idx_raw, w_idx_raw, q_norm_g, kv_norm_g, w_uq, w_uk, w_uv,
                  w_q_idx, k_idx_ln_g, k_idx_ln_b):
    bsz, seq, _ = c_q.shape
    dtype = c_q.dtype
    cq = rms_norm(c_q, q_norm_g)
    ckv = rms_norm(c_kv, kv_norm_g)
    q = jnp.einsum('bsr,rhd->bshd', cq, w_uq)
    q_lat = jnp.einsum('bshd,hdr->bshr', q, w_uk)
    q_idx = (cq @ w_q_idx).reshape(bsz, seq, IDX_HEADS, IDX_DIM)
    k_idx = layer_norm(k_idx_raw, k_idx_ln_g, k_idx_ln_b)
    w_idx = w_idx_raw.astype(jnp.float32) * (IDX_HEADS ** -0.5 * IDX_DIM ** -0.5)
    n_sel = min(TOPK_MAX, seq // 4)
    slopes = 2.0 ** (-8.0 * (jnp.arange(A_HEADS, dtype=jnp.float32) + 1.0) / A_HEADS)
    n_blk = seq // Q_BLOCK
    s_pos = jnp.arange(seq, dtype=jnp.int32)

    def to_blocks(t):
        return jnp.swapaxes(t.reshape(bsz, n_blk, Q_BLOCK, *t.shape[2:]), 0, 1)

    def block(args):
        q_lat_b, q_idx_b, w_b, t_pos = args
        idx_logits = jnp.einsum('bqhd,bsd->bqhs', q_idx_b, k_idx).astype(jnp.float32)
        score = jnp.einsum('bqhs,bqh->bqs', jax.nn.relu(idx_logits), w_b)
        score = jnp.where(s_pos[None, None, :] <= t_pos[None, :, None], score, -jnp.inf)
        _, sel = lax.top_k(score, n_sel)
        kv_sel = jax.vmap(lambda c, i: c[i])(ckv, sel)
        att = jnp.einsum('bqhr,bqkr->bqhk', q_lat_b, kv_sel).astype(jnp.float32) * (A_HEAD_DIM ** -0.5)
        dist = (t_pos[None, :, None] - sel).astype(jnp.float32)
        att = att - slopes[None, None, :, None] * dist[:, :, None, :]
        valid = (sel <= t_pos[None, :, None])[:, :, None, :]
        p = jax.nn.softmax(jnp.where(valid, att, -jnp.inf), axis=-1).astype(dtype)
        o_lat = jnp.einsum('bqhk,bqkr->bqhr', p, kv_sel)
        return jnp.einsum('bqhr,hrd->bqhd', o_lat, w_uv)

    t_blocks = jnp.arange(seq, dtype=jnp.int32).reshape(n_blk, Q_BLOCK)
    out = lax.map(block, (to_blocks(q_lat), to_blocks(q_idx), to_blocks(w_idx), t_blocks))
    return jnp.swapaxes(out, 0, 1).reshape(bsz, seq, A_HEADS * A_HEAD_DIM)


def chunk_gated_delta_rule(q, k, v, g, beta):
    bsz, seq, nh, dk = q.shape
    dv = v.shape[-1]
    n_chk = seq // CHUNK

    def chunks4(t):
        return t.reshape(bsz, n_chk, CHUNK, nh, t.shape[-1]).transpose(0, 3, 1, 2, 4)

    def chunks3(t):
        return t.reshape(bsz, n_chk, CHUNK, nh).transpose(0, 3, 1, 2)

    q, k, v = chunks4(q), chunks4(k), chunks4(v)
    g, beta = chunks3(g), chunks3(beta)
    g_cum = jnp.cumsum(g, axis=-1)
    ar = jnp.arange(CHUNK)
    lower_incl = ar[:, None] >= ar[None, :]
    strict = ar[:, None] > ar[None, :]
    decay = jnp.exp(jnp.where(lower_incl, g_cum[..., :, None] - g_cum[..., None, :], -jnp.inf))
    k_beta = k * beta[..., None]
    v_beta = v * beta[..., None]
    a_mat = jnp.where(strict, jnp.einsum('bhncd,bhned->bhnce', k_beta, k) * decay, 0.0)
    m_mat = a_mat + jnp.eye(CHUNK, dtype=a_mat.dtype)

    def solve(rhs):
        return lax.linalg.triangular_solve(m_mat, rhs, left_side=True, lower=True, unit_diagonal=True)

    value = solve(v_beta)
    k_cumdecay = solve(k_beta * jnp.exp(g_cum)[..., None])
    attn_intra = jnp.einsum('bhncd,bhned->bhnce', q, k) * decay
    q_dec = q * jnp.exp(g_cum)[..., None]
    g_last = g_cum[..., -1]
    k_dec = k * jnp.exp(g_last[..., None] - g_cum)[..., None]
    chunk_decay = jnp.exp(g_last)

    def n_first(t):
        return jnp.moveaxis(t, 2, 0)

    xs = (n_first(q_dec), n_first(k_dec), n_first(value), n_first(k_cumdecay), n_first(attn_intra), n_first(chunk_decay))

    def step(state, inp):
        qd, kd, val, kcd, ai, cd = inp
        v_new = val - jnp.einsum('bhcd,bhde->bhce', kcd, state)
        o = jnp.einsum('bhcd,bhde->bhce', qd, state) + jnp.einsum('bhcj,bhje->bhce', ai, v_new)
        state = state * cd[..., None, None] + jnp.einsum('bhcd,bhce->bhde', kd, v_new)
        return state, o

    state0 = jnp.zeros((bsz, nh, dk, dv), jnp.float32)
    _, o = lax.scan(step, state0, xs)
    return o.transpose(1, 0, 3, 2, 4).reshape(bsz, seq, nh, dv)


def gated_deltanet(q_raw, k_raw, v_raw, z, a, b, conv_w, a_log, dt_bias, o_norm_g):
    bsz, seq, _ = q_raw.shape
    dtype = q_raw.dtype
    qkv = jax.nn.silu(causal_depthwise_conv(jnp.concatenate([q_raw, k_raw, v_raw], -1), conv_w))
    q, k, v = jnp.split(qkv.astype(jnp.float32), 3, axis=-1)
    q = l2_norm(q.reshape(bsz, seq, B_HEADS, B_HEAD_DIM)) * (B_HEAD_DIM ** -0.5)
    k = l2_norm(k.reshape(bsz, seq, B_HEADS, B_HEAD_DIM))
    v = v.reshape(bsz, seq, B_HEADS, B_HEAD_DIM)
    beta = jax.nn.sigmoid(b.astype(jnp.float32))
    g = -jnp.exp(a_log.astype(jnp.float32)) * jax.nn.softplus(a.astype(jnp.float32) + dt_bias.astype(jnp.float32))
    o = chunk_gated_delta_rule(q, k, v, g, beta)
    zf = z.astype(jnp.float32).reshape(bsz, seq, B_HEADS, B_HEAD_DIM)
    y = rms_norm(o, o_norm_g) * jax.nn.silu(zf)
    return y.reshape(bsz, seq, B_WIDTH).astype(dtype)


def hybrid_mixer(h, w_in, q_norm_g, kv_norm_g, w_uq, w_uk, w_uv, w_q_idx, k_idx_ln_g, k_idx_ln_b,
                 conv_w, a_log, dt_bias, o_norm_g, w_out):
    proj = h @ w_in
    cuts = [int(c) for c in np.cumsum(IN_SPLITS)[:-1]]
    c_q, c_kv, k_idx_raw, w_idx_raw, q_b, k_b, v_b, z_b, a_b, b_b = jnp.split(proj, cuts, axis=-1)
    o_a = dsa_attention(c_q, c_kv, k_idx_raw, w_idx_raw, q_norm_g, kv_norm_g, w_uq, w_uk, w_uv,
                        w_q_idx, k_idx_ln_g, k_idx_ln_b)
    o_b = gated_deltanet(q_b, k_b, v_b, z_b, a_b, b_b, conv_w, a_log, dt_bias, o_norm_g)
    return jnp.concatenate([o_a, o_b], axis=-1) @ w_out


def setup_inputs(seed: int = 0) -> dict:
    key = jax.random.key(seed)
    ks = iter(jax.random.split(key, 40))
    f32 = jnp.float32
    L = DEPTH

    def nrm(shape, fan_in, scale=1.0):
        return jax.random.normal(next(ks), shape, f32) * (scale * fan_in ** -0.5)

    def gain(shape):
        return 1.0 + 0.02 * jax.random.normal(next(ks), shape, f32)

    def bias(shape):
        return 0.02 * jax.random.normal(next(ks), shape, f32)

    x = jax.random.normal(next(ks), (BATCH, SEQ, D_MODEL), f32)
    ffn1_w_gate = nrm((L, D_MODEL, D_FF), D_MODEL)
    ffn1_w_up = nrm((L, D_MODEL, D_FF), D_MODEL)
    ffn1_w_down = nrm((L, D_FF, D_MODEL), D_FF, DEEPNORM_BETA)
    ln1_g = gain((L, D_MODEL))
    ln1_b = bias((L, D_MODEL))
    w_in = nrm((L, D_MODEL, D_IN), D_MODEL)
    q_norm_g = gain((L, Q_LORA))
    kv_norm_g = gain((L, KV_LORA))
    w_uq = nrm((L, Q_LORA, A_HEADS, A_HEAD_DIM), Q_LORA)
    w_uk = nrm((L, A_HEADS, A_HEAD_DIM, KV_LORA), KV_LORA)
    w_uv = nrm((L, A_HEADS, KV_LORA, A_HEAD_DIM), KV_LORA)
    w_q_idx = nrm((L, Q_LORA, IDX_HEADS * IDX_DIM), Q_LORA)
    k_idx_ln_g = gain((L, IDX_DIM))
    k_idx_ln_b = bias((L, IDX_DIM))
    conv_w = nrm((L, CONV_WIDTH, 3 * B_WIDTH), CONV_WIDTH)
    a_log = jnp.log(jax.random.uniform(next(ks), (L, B_HEADS), f32, 1.0, 16.0))
    dt = jnp.exp(jax.random.uniform(next(ks), (L, B_HEADS), f32, np.log(1e-3), np.log(1e-1)))
    dt_bias = dt + jnp.log(-jnp.expm1(-dt))
    o_norm_g = gain((L, B_HEAD_DIM))
    w_out = nrm((L, D_MIX, D_MODEL), D_MIX, DEEPNORM_BETA)
    ln2_g = gain((L, D_MODEL))
    ln2_b = bias((L, D_MODEL))
    ffn2_w_gate = nrm((L, D_MODEL, D_FF), D_MODEL)
    ffn2_w_up = nrm((L, D_MODEL, D_FF), D_MODEL)
    ffn2_w_down = nrm((L, D_FF, D_MODEL), D_FF, DEEPNORM_BETA)
    ln3_g = gain((L, D_MODEL))
    ln3_b = bias((L, D_MODEL))
    return {'x': x, 'ffn1_w_gate': ffn1_w_gate, 'ffn1_w_up': ffn1_w_up, 'ffn1_w_down': ffn1_w_down,
            'ln1_g': ln1_g, 'ln1_b': ln1_b, 'w_in': w_in, 'q_norm_g': q_norm_g, 'kv_norm_g': kv_norm_g,
            'w_uq': w_uq, 'w_uk': w_uk, 'w_uv': w_uv, 'w_q_idx': w_q_idx, 'k_idx_ln_g': k_idx_ln_g,
            'k_idx_ln_b': k_idx_ln_b, 'conv_w': conv_w, 'a_log': a_log, 'dt_bias': dt_bias,
            'o_norm_g': o_norm_g, 'w_out': w_out, 'ln2_g': ln2_g, 'ln2_b': ln2_b,
            'ffn2_w_gate': ffn2_w_gate, 'ffn2_w_up': ffn2_w_up, 'ffn2_w_down': ffn2_w_down,
            'ln3_g': ln3_g, 'ln3_b': ln3_b}


def reference(x, ffn1_w_gate, ffn1_w_up, ffn1_w_down, ln1_g, ln1_b, w_in, q_norm_g, kv_norm_g,
              w_uq, w_uk, w_uv, w_q_idx, k_idx_ln_g, k_idx_ln_b, conv_w, a_log, dt_bias, o_norm_g,
              w_out, ln2_g, ln2_b, ffn2_w_gate, ffn2_w_up, ffn2_w_down, ln3_g, ln3_b):
    h = x
    for l in range(DEPTH):
        h = layer_norm(DEEPNORM_ALPHA * h + 0.5 * swiglu(h, ffn1_w_gate[l], ffn1_w_up[l], ffn1_w_down[l]),
                       ln1_g[l], ln1_b[l])
        mix = hybrid_mixer(h, w_in[l], q_norm_g[l], kv_norm_g[l], w_uq[l], w_uk[l], w_uv[l], w_q_idx[l],
                           k_idx_ln_g[l], k_idx_ln_b[l], conv_w[l], a_log[l], dt_bias[l], o_norm_g[l], w_out[l])
        h = layer_norm(DEEPNORM_ALPHA * h + mix, ln2_g[l], ln2_b[l])
        h = layer_norm(DEEPNORM_ALPHA * h + 0.5 * swiglu(h, ffn2_w_gate[l], ffn2_w_up[l], ffn2_w_down[l]),
                       ln3_g[l], ln3_b[l])
    return h
```

```python
import functools
import math

import jax
import jax.numpy as jnp
from jax import lax
from jax.experimental import pallas as pl
from jax.experimental.pallas import tpu as pltpu

F32 = jnp.float32
BF16 = jnp.bfloat16

A_HEADS = 8
A_HEAD_DIM = 128
Q_LORA = 512
KV_LORA = 256
IDX_HEADS = 16
IDX_DIM = 128
TOPK_MAX = 256
B_HEADS = 8
B_HEAD_DIM = 128
B_WIDTH = B_HEADS * B_HEAD_DIM
CONV_WIDTH = 4
DEPTH = 1
DEEPNORM_ALPHA = (2.0 * DEPTH) ** 0.25
EPS = 1e-6

LANES = 128
SUBLANES = 8
VMEM_LIMIT_BYTES = 56 * 1024 * 1024

COL_QKV = 0
COL_Z = 3 * B_WIDTH
COL_CQ = COL_Z + B_WIDTH
COL_CKV = COL_CQ + Q_LORA
COL_KIDX = COL_CKV + KV_LORA
COL_SMALL = COL_KIDX + IDX_DIM
D_IN_PAD = COL_SMALL + LANES
SMALL_W = 0
SMALL_A = IDX_HEADS
SMALL_B = IDX_HEADS + B_HEADS

Q_BLOCK = 128
S_TILE = 256
IDX_GROUP = 2
ATT_GROUP = 2
GDN_CHUNK = 128
INT_MIN = -2 ** 31
F32_LOWEST = -3.4028234663852886e38
LOG2E = math.log2(math.e)


def _dot(a, b):
    return jnp.dot(a, b, preferred_element_type=F32)


def _dot_nt(a, b):
    return lax.dot_general(a, b, (((1,), (1,)), ((), ())), preferred_element_type=F32)


def _sigmoid(x):
    return 1.0 / (1.0 + jnp.exp(-x))


def _layer_norm(y, g, b):
    mu = jnp.mean(y, axis=-1, keepdims=True)
    yc = y - mu
    var = jnp.mean(yc * yc, axis=-1, keepdims=True)
    return yc * lax.rsqrt(var + EPS) * g + b


def _params(*sem):
    return pltpu.CompilerParams(dimension_semantics=sem, vmem_limit_bytes=VMEM_LIMIT_BYTES)


def _ffn_ln_body(x_ref, wg_ref, wu_ref, wd_ref, g_ref, b_ref, o_ref, xb_ref, acc_ref):
    j = pl.program_id(1)

    @pl.when(j == 0)
    def _():
        xb_ref[...] = x_ref[...].astype(BF16)
        acc_ref[...] = jnp.zeros_like(acc_ref)

    xb = xb_ref[...]
    gate = _dot(xb, wg_ref[...])
    up = _dot(xb, wu_ref[...])
    act = gate * _sigmoid(gate) * up
    acc_ref[...] += _dot(act.astype(BF16), wd_ref[...])

    @pl.when(j == pl.num_programs(1) - 1)
    def _():
        y = DEEPNORM_ALPHA * x_ref[...] + 0.5 * acc_ref[...]
        o_ref[...] = _layer_norm(y, g_ref[...], b_ref[...])


def _ffn_ln(x, wg, wu, wd, g, b, *, tm=512, tf=512):
    t, d = x.shape
    f = wg.shape[1]
    return pl.pallas_call(
        _ffn_ln_body,
        out_shape=jax.ShapeDtypeStruct((t, d), F32),
        grid=(t // tm, f // tf),
        in_specs=[
            pl.BlockSpec((tm, d), lambda i, j: (i, 0)),
            pl.BlockSpec((d, tf), lambda i, j: (0, j)),
            pl.BlockSpec((d, tf), lambda i, j: (0, j)),
            pl.BlockSpec((tf, d), lambda i, j: (j, 0)),
            pl.BlockSpec((1, d), lambda i, j: (0, 0)),
            pl.BlockSpec((1, d), lambda i, j: (0, 0)),
        ],
        out_specs=pl.BlockSpec((tm, d), lambda i, j: (i, 0)),
        scratch_shapes=[pltpu.VMEM((tm, d), BF16), pltpu.VMEM((tm, d), F32)],
        compiler_params=_params("parallel", "arbitrary"),
        name="ffn_ln",
    )(x, wg, wu, wd, g, b)


def _in_proj_body(x_ref, w_ref, o_ref, xb_ref):
    @pl.when(pl.program_id(1) == 0)
    def _():
        xb_ref[...] = x_ref[...].astype(BF16)

    o_ref[...] = _dot(xb_ref[...], w_ref[...])


def _in_proj(x, w, *, tm=512, tn=1024):
    t, d = x.shape
    n = w.shape[1]
    return pl.pallas_call(
        _in_proj_body,
        out_shape=jax.ShapeDtypeStruct((t, n), F32),
        grid=(t // tm, n // tn),
        in_specs=[
            pl.BlockSpec((tm, d), lambda i, j: (i, 0)),
            pl.BlockSpec((d, tn), lambda i, j: (0, j)),
        ],
        out_specs=pl.BlockSpec((tm, tn), lambda i, j: (i, j)),
        scratch_shapes=[pltpu.VMEM((tm, d), BF16)],
        compiler_params=_params("parallel", "arbitrary"),
        name="in_proj",
    )(x, w)


def _dsa_prep_body(cq_ref, ckv_ref, kidx_ref, small_ref, qg_ref, kvg_ref, wuq_ref, wuk_ref,
                   wqi_ref, lng_ref, lnb_ref,
                   qlat_ref, qidx_ref, ckv_o_ref, ckvt_o_ref, kidx_o_ref, wt_o_ref):
    tm = cq_ref.shape[0]
    nb = tm // Q_BLOCK

    cq = cq_ref[...]
    cq = cq * lax.rsqrt(jnp.mean(cq * cq, axis=-1, keepdims=True) + EPS) * qg_ref[...]
    cqb = cq.astype(BF16)

    q = _dot(cqb, wuq_ref[...]).astype(BF16)
    for h in range(A_HEADS):
        ql = _dot(q[:, h * A_HEAD_DIM:(h + 1) * A_HEAD_DIM], wuk_ref[h])
        ql = ql * (A_HEAD_DIM ** -0.5 * LOG2E)
        qlat_ref[:, h] = ql.astype(BF16).reshape(nb, Q_BLOCK, KV_LORA)

    qi = _dot(cqb, wqi_ref[...]).astype(BF16)
    for h in range(IDX_HEADS):
        qidx_ref[:, h] = qi[:, h * IDX_DIM:(h + 1) * IDX_DIM].reshape(nb, Q_BLOCK, IDX_DIM)

    ckv = ckv_ref[...]
    ckv = ckv * lax.rsqrt(jnp.mean(ckv * ckv, axis=-1, keepdims=True) + EPS) * kvg_ref[...]
    ckv_o_ref[...] = ckv.astype(BF16)
    ckvt_o_ref[0] = ckv.T.astype(BF16)

    kidx_o_ref[...] = _layer_norm(kidx_ref[...], lng_ref[...], lnb_ref[...]).astype(BF16)

    wt = small_ref[...].T
    wt_o_ref[...] = wt[SMALL_W:SMALL_W + IDX_HEADS, :] * (IDX_HEADS ** -0.5 * IDX_DIM ** -0.5)


def _dsa_prep(proj, qg, kvg, wuq, wuk, wqi, lng, lnb, *, tm=S_TILE):
    t = proj.shape[0]
    nq = t // Q_BLOCK
    full = lambda *shape: pl.BlockSpec(shape, lambda i: (0,) * len(shape))
    return pl.pallas_call(
        _dsa_prep_body,
        out_shape=(
            jax.ShapeDtypeStruct((nq, A_HEADS, Q_BLOCK, KV_LORA), BF16),
            jax.ShapeDtypeStruct((nq, IDX_HEADS, Q_BLOCK, IDX_DIM), BF16),
            jax.ShapeDtypeStruct((t, KV_LORA), BF16),
            jax.ShapeDtypeStruct((t // tm, KV_LORA, tm), BF16),
            jax.ShapeDtypeStruct((t, IDX_DIM), BF16),
            jax.ShapeDtypeStruct((IDX_HEADS, t), F32),
        ),
        grid=(t // tm,),
        in_specs=[
            pl.BlockSpec((tm, Q_LORA), lambda i: (i, COL_CQ // Q_LORA)),
            pl.BlockSpec((tm, KV_LORA), lambda i: (i, COL_CKV // KV_LORA)),
            pl.BlockSpec((tm, IDX_DIM), lambda i: (i, COL_KIDX // IDX_DIM)),
            pl.BlockSpec((tm, LANES), lambda i: (i, COL_SMALL // LANES)),
            full(1, Q_LORA), full(1, KV_LORA),
            full(Q_LORA, A_HEADS * A_HEAD_DIM),
            full(A_HEADS, A_HEAD_DIM, KV_LORA),
            full(Q_LORA, IDX_HEADS * IDX_DIM),
            full(1, IDX_DIM), full(1, IDX_DIM),
        ],
        out_specs=(
            pl.BlockSpec((tm // Q_BLOCK, A_HEADS, Q_BLOCK, KV_LORA), lambda i: (i, 0, 0, 0)),
            pl.BlockSpec((tm // Q_BLOCK, IDX_HEADS, Q_BLOCK, IDX_DIM), lambda i: (i, 0, 0, 0)),
            pl.BlockSpec((tm, KV_LORA), lambda i: (i, 0)),
            pl.BlockSpec((1, KV_LORA, tm), lambda i: (i, 0, 0)),
            pl.BlockSpec((tm, IDX_DIM), lambda i: (i, 0)),
            pl.BlockSpec((IDX_HEADS, tm), lambda i: (0, i)),
        ),
        compiler_params=_params("parallel"),
        name="dsa_prep",
    )(proj, proj, proj, proj, qg, kvg, wuq, wuk, wqi, lng, lnb)


def _key_to_f32(key):
    return pltpu.bitcast(key ^ ((key >> 31) & 0x7FFFFFFF), F32)


def _dsa_attn_body(qidx_ref, qlat_ref, wt_ref, kidx_ref, ckv_ref, ckvt_ref, wuv_ref, o_ref,
                   sc_ref, bias_ref, m_ref, l_ref, acc_ref, *, n_sel):
    i = pl.program_id(1)
    t0 = i * Q_BLOCK
    n_tiles = (t0 + Q_BLOCK + S_TILE - 1) // S_TILE

    t_lane = t0 + lax.broadcasted_iota(jnp.int32, (S_TILE, Q_BLOCK), 1)
    s_row = lax.broadcasted_iota(jnp.int32, (S_TILE, Q_BLOCK), 0)

    def tile_rows(j):
        return pl.ds(pl.multiple_of(j * S_TILE, S_TILE), S_TILE)

    wt = wt_ref[...]

    def score_tile(j, carry):
        k_tile = kidx_ref[tile_rows(j), :]
        sc = jnp.zeros((S_TILE, Q_BLOCK), F32)
        for g in range(IDX_HEADS // IDX_GROUP):
            qg = qidx_ref[0, g * IDX_GROUP:(g + 1) * IDX_GROUP].reshape(IDX_GROUP * Q_BLOCK, IDX_DIM)
            logits = _dot_nt(k_tile, qg)
            for u in range(IDX_GROUP):
                h = g * IDX_GROUP + u
                sc = sc + wt[h:h + 1, :] * jnp.maximum(logits[:, u * Q_BLOCK:(u + 1) * Q_BLOCK], 0.0)
        sc_ref[tile_rows(j), :] = jnp.where(s_row + j * S_TILE <= t_lane, sc, -jnp.inf)
        return carry

    lax.fori_loop(0, n_tiles, score_tile, 0)

    def count(pred):
        def body(j, acc):
            hit = jnp.where(pred(sc_ref[tile_rows(j), :]), 1, 0)
            return acc + jnp.sum(hit.reshape(S_TILE // SUBLANES, SUBLANES, Q_BLOCK), axis=0)

        acc = lax.fori_loop(0, n_tiles, body, jnp.zeros((SUBLANES, Q_BLOCK), jnp.int32))
        return jnp.sum(acc, axis=0, keepdims=True)

    def count_ge(key):
        cand = _key_to_f32(key)
        return count(lambda sc: sc >= cand)

    zero = jnp.zeros((1, Q_BLOCK), jnp.int32)
    key = jnp.where(count_ge(zero) >= n_sel, zero, zero + INT_MIN)

    def bit_step(b, key):
        cand = key | (1 << (30 - b))
        return jnp.where(count_ge(cand) >= n_sel, cand, key)

    key = lax.fori_loop(0, 31, bit_step, key)
    thr = _key_to_f32(key)
    thr = jnp.maximum(jnp.where(thr != thr, F32_LOWEST, thr), F32_LOWEST)

    n_ge = count(lambda sc: sc >= thr)

    @pl.when(jnp.max(n_ge) > n_sel)
    def _():
        need = (n_sel - count(lambda sc: sc > thr)).astype(F32)
        r = lax.broadcasted_iota(jnp.int32, (S_TILE, S_TILE), 0)
        c = lax.broadcasted_iota(jnp.int32, (S_TILE, S_TILE), 1)
        lower_incl = jnp.where(c <= r, 1.0, 0.0).astype(BF16)

        def body(j, run):
            sc = sc_ref[tile_rows(j), :]
            tied = sc == thr
            rank = run + _dot(lower_incl, jnp.where(tied, 1.0, 0.0).astype(BF16))
            sc_ref[tile_rows(j), :] = jnp.where(tied & (rank > need), -jnp.inf, sc)
            return rank[S_TILE - 1:S_TILE, :]

        lax.fori_loop(0, n_tiles, body, jnp.zeros((1, Q_BLOCK), F32))

    gw = ATT_GROUP * Q_BLOCK
    groups = range(A_HEADS // ATT_GROUP)
    lane = lax.broadcasted_iota(jnp.int32, (1, gw), 1)
    slope_rows = []
    for g in groups:
        row_g = jnp.zeros((1, gw), F32)
        for u in range(ATT_GROUP):
            slope = 2.0 ** (-8.0 * (g * ATT_GROUP + u + 1) / A_HEADS) * LOG2E
            row_g = jnp.where(lane >= u * Q_BLOCK, slope, row_g)
        slope_rows.append(row_g)
    s_rel = lax.broadcasted_iota(jnp.int32, (S_TILE, gw), 0).astype(F32)
    for g in groups:
        bias_ref[g] = slope_rows[g] * s_rel
    m_ref[...] = jnp.full(m_ref.shape, -1e30, F32)
    l_ref[...] = jnp.zeros_like(l_ref)
    acc_ref[...] = jnp.zeros_like(acc_ref)

    def att_tile(j, carry):
        drop = jnp.where(sc_ref[tile_rows(j), :] >= thr, 0.0, -jnp.inf)
        drop = jnp.concatenate([drop] * ATT_GROUP, axis=1)
        ckv_tile = ckv_ref[tile_rows(j), :]
        ckvt_tile = ckvt_ref[j]
        off = (j * S_TILE - t0).astype(F32)
        logits = [_dot_nt(ckv_tile, qlat_ref[0, g * ATT_GROUP:(g + 1) * ATT_GROUP].reshape(gw, KV_LORA))
                  for g in groups]
        probs, alphas = [], []
        for g in groups:
            a = logits[g] + bias_ref[g] + drop
            shift = slope_rows[g] * off
            m_old = m_ref[g:g + 1, :]
            m_new = jnp.maximum(m_old, jnp.max(a, axis=0, keepdims=True) + shift)
            alpha = jnp.exp2(m_old - m_new)
            p = jnp.exp2(a - (m_new - shift))
            l_ref[g:g + 1, :] = alpha * l_ref[g:g + 1, :] + jnp.sum(p, axis=0, keepdims=True)
            m_ref[g:g + 1, :] = m_new
            probs.append(p.astype(BF16))
            alphas.append(alpha)
        for g in groups:
            acc_ref[g] = alphas[g] * acc_ref[g] + _dot(ckvt_tile, probs[g])
        return carry

    lax.fori_loop(0, n_tiles, att_tile, 0)

    for g in groups:
        o_lat_t = acc_ref[g] * (1.0 / l_ref[g:g + 1, :])
        for u in range(ATT_GROUP):
            h = g * ATT_GROUP + u
            o_lat = o_lat_t[:, u * Q_BLOCK:(u + 1) * Q_BLOCK].T.astype(BF16)
            o_ref[:, h * A_HEAD_DIM:(h + 1) * A_HEAD_DIM] = _dot(o_lat, wuv_ref[h]).astype(BF16)


def _dsa_attn(qlat, qidx, wt, kidx, ckv, ckvt, wuv, *, bsz, seq):
    nq = seq // Q_BLOCK
    ns = seq // S_TILE
    n_sel = min(TOPK_MAX, seq // 4)
    ng, gw = A_HEADS // ATT_GROUP, ATT_GROUP * Q_BLOCK
    return pl.pallas_call(
        functools.partial(_dsa_attn_body, n_sel=n_sel),
        out_shape=jax.ShapeDtypeStruct((bsz * seq, A_HEADS * A_HEAD_DIM), BF16),
        grid=(bsz, nq),
        in_specs=[
            pl.BlockSpec((1, IDX_HEADS, Q_BLOCK, IDX_DIM), lambda b, i: (b * nq + i, 0, 0, 0)),
            pl.BlockSpec((1, A_HEADS, Q_BLOCK, KV_LORA), lambda b, i: (b * nq + i, 0, 0, 0)),
            pl.BlockSpec((IDX_HEADS, Q_BLOCK), lambda b, i: (0, b * nq + i)),
            pl.BlockSpec((seq, IDX_DIM), lambda b, i: (b, 0)),
            pl.BlockSpec((seq, KV_LORA), lambda b, i: (b, 0)),
            pl.BlockSpec((ns, KV_LORA, S_TILE), lambda b, i: (b, 0, 0)),
            pl.BlockSpec((A_HEADS, KV_LORA, A_HEAD_DIM), lambda b, i: (0, 0, 0)),
        ],
        out_specs=pl.BlockSpec((Q_BLOCK, A_HEADS * A_HEAD_DIM), lambda b, i: (b * nq + i, 0)),
        scratch_shapes=[
            pltpu.VMEM((seq, Q_BLOCK), F32),
            pltpu.VMEM((ng, S_TILE, gw), F32),
            pltpu.VMEM((ng, gw), F32),
            pltpu.VMEM((ng, gw), F32),
            pltpu.VMEM((ng, KV_LORA, gw), F32),
        ],
        compiler_params=_params("parallel", "arbitrary"),
        name="dsa_attn",
    )(qidx, qlat, wt, kidx, ckv, ckvt, wuv)


def _gdn_prep_body(x_ref, halo_ref, small_ref, cw_ref, alog_ref, dtb_ref,
                   q_ref, k_ref, v_ref, gcol_ref, grow_ref, xx_ref):
    ts = x_ref.shape[0]
    first = pl.program_id(1) == 0
    halo = halo_ref[...]
    xx_ref[0:SUBLANES, :] = jnp.where(first, jnp.zeros_like(halo), halo)
    xx_ref[SUBLANES:, :] = x_ref[...]
    cw = cw_ref[...]
    y = jnp.zeros(x_ref.shape, F32)
    for tap in range(CONV_WIDTH):
        off = SUBLANES - (CONV_WIDTH - 1) + tap
        y = y + cw[tap:tap + 1, :] * xx_ref[off:off + ts, :]
    y = y * _sigmoid(y)

    for h in range(B_HEADS):
        lo, hi = h * B_HEAD_DIM, (h + 1) * B_HEAD_DIM
        qh = y[:, lo:hi]
        q_ref[:, lo:hi] = qh * lax.rsqrt(jnp.sum(qh * qh, axis=-1, keepdims=True) + EPS) * (
            B_HEAD_DIM ** -0.5)
        kh = y[:, B_WIDTH + lo:B_WIDTH + hi]
        k_ref[:, lo:hi] = kh * lax.rsqrt(jnp.sum(kh * kh, axis=-1, keepdims=True) + EPS)
    v_ref[...] = y[:, 2 * B_WIDTH:]

    small = small_ref[...]
    pre = small + dtb_ref[...]
    softplus = jnp.maximum(pre, 0.0) + jnp.log1p(jnp.exp(-jnp.abs(pre)))
    g = -jnp.exp(alog_ref[...]) * softplus
    row = lax.broadcasted_iota(jnp.int32, g.shape, 0) & (GDN_CHUNK - 1)
    shift = 1
    while shift < GDN_CHUNK:
        g = g + jnp.where(row >= shift, pltpu.roll(g, shift, 0), 0.0)
        shift *= 2
    lane = lax.broadcasted_iota(jnp.int32, g.shape, 1)
    is_a = (lane >= SMALL_A) & (lane < SMALL_A + B_HEADS)
    gb = jnp.where(is_a, g, _sigmoid(small))
    gcol_ref[...] = gb
    grow_ref[...] = gb.T


def _gdn_prep(proj, conv_w, alog_p, dtb_p, *, bsz, seq, ts=256):
    t = bsz * seq
    ns = seq // ts
    c = 3 * B_WIDTH
    return pl.pallas_call(
        _gdn_prep_body,
        out_shape=(
            jax.ShapeDtypeStruct((t, B_WIDTH), F32),
            jax.ShapeDtypeStruct((t, B_WIDTH), F32),
            jax.ShapeDtypeStruct((t, B_WIDTH), F32),
            jax.ShapeDtypeStruct((t, LANES), F32),
            jax.ShapeDtypeStruct((LANES, t), F32),
        ),
        grid=(bsz, ns),
        in_specs=[
            pl.BlockSpec((ts, c), lambda b, i: (b * ns + i, COL_QKV // c)),
            pl.BlockSpec((SUBLANES, c),
                         lambda b, i: (jnp.maximum((b * ns + i) * (ts // SUBLANES) - 1, 0), 0)),
            pl.BlockSpec((ts, LANES), lambda b, i: (b * ns + i, COL_SMALL // LANES)),
            pl.BlockSpec((CONV_WIDTH, c), lambda b, i: (0, 0)),
            pl.BlockSpec((1, LANES), lambda b, i: (0, 0)),
            pl.BlockSpec((1, LANES), lambda b, i: (0, 0)),
        ],
        out_specs=(
            pl.BlockSpec((ts, B_WIDTH), lambda b, i: (b * ns + i, 0)),
            pl.BlockSpec((ts, B_WIDTH), lambda b, i: (b * ns + i, 0)),
            pl.BlockSpec((ts, B_WIDTH), lambda b, i: (b * ns + i, 0)),
            pl.BlockSpec((ts, LANES), lambda b, i: (b * ns + i, 0)),
            pl.BlockSpec((LANES, ts), lambda b, i: (0, b * ns + i)),
        ),
        scratch_shapes=[pltpu.VMEM((ts + SUBLANES, c), F32)],
        compiler_params=_params("parallel", "arbitrary"),
        name="gdn_prep",
    )(proj, proj, proj, conv_w, alog_p, dtb_p)


def _split_bf16(a):
    hi = a.astype(BF16)
    lo = (a - hi.astype(F32)).astype(BF16)
    return hi, lo


def _dot3(a, b):
    ah, al = _split_bf16(a)
    bh, bl = _split_bf16(b)
    return _dot(ah, bh) + (_dot(ah, bl) + _dot(al, bh))


def _each(fn, *lists):
    return [fn(*args) for args in zip(*lists)]


def _unit_lower_inverses(mats, row, col):
    base_log2 = 4
    eye = jnp.where(row == col, 1.0, 0.0)
    in_block = (row >> base_log2) == (col >> base_log2)
    d = [jnp.where(in_block, a, 0.0) for a in mats]
    x = [eye - dh for dh in d]
    p = _each(_dot3, d, d)
    for step in range(base_log2 - 1):
        x = _each(lambda xh, th: xh + th, x, _each(_dot3, x, p))
        if step < base_log2 - 2:
            p = _each(_dot3, p, p)
    sh = base_log2
    while (1 << sh) < GDN_CHUNK:
        same_parent = (row >> (sh + 1)) == (col >> (sh + 1))
        same_block = (row >> sh) == (col >> sh)
        quad = [jnp.where(same_block, 0.0, jnp.where(same_parent, a, 0.0)) for a in mats]
        x = _each(lambda xh, th: xh - th, x, _each(_dot3, x, _each(_dot3, quad, x)))
        sh += 1
    return x


def _gdn_chunk_body(q_ref, k_ref, v_ref, z_ref, gcol_ref, grow_ref, ng_ref, o_ref, state_ref):
    @pl.when(pl.program_id(1) == 0)
    def _():
        state_ref[...] = jnp.zeros_like(state_ref)

    c = GDN_CHUNK
    heads = range(B_HEADS)
    span = lambda h: slice(h * B_HEAD_DIM, (h + 1) * B_HEAD_DIM)
    row = lax.broadcasted_iota(jnp.int32, (c, c), 0)
    col = lax.broadcasted_iota(jnp.int32, (c, c), 1)
    gcol = gcol_ref[...]
    grow = grow_ref[...]
    e_cum = jnp.exp(gcol)
    e_rest = jnp.exp(gcol[c - 1:c, :] - gcol)
    e_last = jnp.exp(grow[:, c - 1:c])

    q = [q_ref[:, span(h)] for h in heads]
    k = [k_ref[:, span(h)] for h in heads]
    v = [v_ref[:, span(h)] for h in heads]
    beta = [gcol[:, SMALL_B + h:SMALL_B + h + 1] for h in heads]
    decay = [jnp.exp(jnp.where(row >= col,
                               gcol[:, SMALL_A + h:SMALL_A + h + 1] - grow[SMALL_A + h:SMALL_A + h + 1, :],
                               -jnp.inf)) for h in heads]
    k_beta = _each(lambda kh, bh: kh * bh, k, beta)
    k16 = [kh.astype(BF16) for kh in k]
    kk = _each(_dot_nt, [kb.astype(BF16) for kb in k_beta], k16)
    qk = _each(_dot_nt, [qh.astype(BF16) for qh in q], k16)
    a_mat = _each(lambda m, dh: jnp.where(row > col, m * dh, 0.0), kk, decay)
    attn = _each(lambda m, dh: (m * dh).astype(BF16), qk, decay)
    t_inv = _unit_lower_inverses(a_mat, row, col)
    rhs = [jnp.concatenate([v[h] * beta[h], k_beta[h] * e_cum[:, SMALL_A + h:SMALL_A + h + 1]], axis=1)
           for h in heads]
    sol = _each(_dot3, t_inv, rhs)
    q_dec = [(q[h] * e_cum[:, SMALL_A + h:SMALL_A + h + 1]).astype(BF16) for h in heads]
    k_dec_t = [(k[h] * e_rest[:, SMALL_A + h:SMALL_A + h + 1]).T.astype(BF16) for h in heads]

    state = [state_ref[h] for h in heads]
    s16 = [s.astype(BF16) for s in state]
    v_new = [sol[h][:, :B_HEAD_DIM] - _dot(sol[h][:, B_HEAD_DIM:].astype(BF16), s16[h]) for h in heads]
    v16 = [x.astype(BF16) for x in v_new]
    for h in heads:
        state_ref[h] = state[h] * e_last[SMALL_A + h:SMALL_A + h + 1, :] + _dot(k_dec_t[h], v16[h])
    out = [_dot(q_dec[h], s16[h]) + _dot(attn[h], v16[h]) for h in heads]
    for h in heads:
        o = out[h]
        y = o * lax.rsqrt(jnp.mean(o * o, axis=-1, keepdims=True) + EPS) * ng_ref[...]
        zh = z_ref[:, span(h)]
        o_ref[:, span(h)] = (y * (zh * _sigmoid(zh))).astype(BF16)


def _gdn_chunk(q, k, v, proj, gcol, grow, ng, *, bsz, seq):
    c = GDN_CHUNK
    nc = seq // c
    tok = lambda b, i: (b * nc + i, 0)
    return pl.pallas_call(
        _gdn_chunk_body,
        out_shape=jax.ShapeDtypeStruct((bsz * seq, B_WIDTH), BF16),
        grid=(bsz, nc),
        in_specs=[
            pl.BlockSpec((c, B_WIDTH), tok),
            pl.BlockSpec((c, B_WIDTH), tok),
            pl.BlockSpec((c, B_WIDTH), tok),
            pl.BlockSpec((c, B_WIDTH), lambda b, i: (b * nc + i, COL_Z // B_WIDTH)),
            pl.BlockSpec((c, LANES), tok),
            pl.BlockSpec((LANES, c), lambda b, i: (0, b * nc + i)),
            pl.BlockSpec((1, B_HEAD_DIM), lambda b, i: (0, 0)),
        ],
        out_specs=pl.BlockSpec((c, B_WIDTH), tok),
        scratch_shapes=[pltpu.VMEM((B_HEADS, B_HEAD_DIM, B_HEAD_DIM), F32)],
        compiler_params=_params("parallel", "arbitrary"),
        name="gdn_chunk",
    )(q, k, v, proj, gcol, grow, ng)


def _out_ln_body(h_ref, oa_ref, ob_ref, wa_ref, wb_ref, g_ref, b_ref, o_ref):
    mix = _dot(oa_ref[...], wa_ref[...]) + _dot(ob_ref[...], wb_ref[...])
    o_ref[...] = _layer_norm(DEEPNORM_ALPHA * h_ref[...] + mix, g_ref[...], b_ref[...])


def _out_ln(h, oa, ob, wa, wb, g, b, *, tm=256):
    t, d = h.shape
    da, db = oa.shape[1], ob.shape[1]
    return pl.pallas_call(
        _out_ln_body,
        out_shape=jax.ShapeDtypeStruct((t, d), F32),
        grid=(t // tm,),
        in_specs=[
            pl.BlockSpec((tm, d), lambda i: (i, 0)),
            pl.BlockSpec((tm, da), lambda i: (i, 0)),
            pl.BlockSpec((tm, db), lambda i: (i, 0)),
            pl.BlockSpec((da, d), lambda i: (0, 0)),
            pl.BlockSpec((db, d), lambda i: (0, 0)),
            pl.BlockSpec((1, d), lambda i: (0, 0)),
            pl.BlockSpec((1, d), lambda i: (0, 0)),
        ],
        out_specs=pl.BlockSpec((tm, d), lambda i: (i, 0)),
        compiler_params=_params("parallel"),
        name="out_ln",
    )(h, oa, ob, wa, wb, g, b)


def _regroup_w_in(w_in):
    cuts = [0, Q_LORA, KV_LORA, IDX_DIM, IDX_HEADS, B_WIDTH, B_WIDTH, B_WIDTH, B_WIDTH, B_HEADS, B_HEADS]
    offs = [sum(cuts[:n + 1]) for n in range(len(cuts))]
    part = lambda n: w_in[:, offs[n]:offs[n + 1]]
    c_q, c_kv, k_idx, w_idx, q_b, k_b, v_b, z_b, a_b, b_b = [part(n) for n in range(10)]
    pad = jnp.zeros((w_in.shape[0], LANES - IDX_HEADS - 2 * B_HEADS), w_in.dtype)
    return jnp.concatenate([q_b, k_b, v_b, z_b, c_q, c_kv, k_idx, w_idx, a_b, b_b, pad],
                           axis=1).astype(BF16)


def _lane_pad(vec, offset):
    out = jnp.zeros((1, LANES), F32)
    return out.at[0, offset:offset + vec.shape[0]].set(vec.astype(F32))


def kernel(x, ffn1_w_gate, ffn1_w_up, ffn1_w_down, ln1_g, ln1_b, w_in, q_norm_g, kv_norm_g, w_uq, w_uk, w_uv, w_q_idx, k_idx_ln_g, k_idx_ln_b, conv_w, a_log, dt_bias, o_norm_g, w_out, ln2_g, ln2_b, ffn2_w_gate, ffn2_w_up, ffn2_w_down, ln3_g, ln3_b):
    bsz, seq, d = x.shape
    assert seq % S_TILE == 0 and seq % GDN_CHUNK == 0 and x.dtype == F32
    t = bsz * seq
    row = lambda p: p.reshape(1, -1).astype(F32)
    h = x.reshape(t, d)
    for l in range(DEPTH):
        h = _ffn_ln(h, ffn1_w_gate[l].astype(BF16), ffn1_w_up[l].astype(BF16),
                    ffn1_w_down[l].astype(BF16), row(ln1_g[l]), row(ln1_b[l]))
        proj = _in_proj(h, _regroup_w_in(w_in[l]))
        qlat, qidx, ckv, ckvt, kidx, wt = _dsa_prep(
            proj, row(q_norm_g[l]), row(kv_norm_g[l]),
            w_uq[l].reshape(Q_LORA, A_HEADS * A_HEAD_DIM).astype(BF16), w_uk[l].astype(BF16),
            w_q_idx[l].astype(BF16), row(k_idx_ln_g[l]), row(k_idx_ln_b[l]))
        o_a = _dsa_attn(qlat, qidx, wt, kidx, ckv, ckvt, w_uv[l].astype(BF16), bsz=bsz, seq=seq)
        gq, gk, gv, gcol, grow = _gdn_prep(proj, conv_w[l].astype(F32), _lane_pad(a_log[l], SMALL_A),
                                           _lane_pad(dt_bias[l], SMALL_A), bsz=bsz, seq=seq)
        o_b = _gdn_chunk(gq, gk, gv, proj, gcol, grow, row(o_norm_g[l]), bsz=bsz, seq=seq)
        w_o = w_out[l].astype(BF16)
        h = _out_ln(h, o_a, o_b, w_o[:A_HEADS * A_HEAD_DIM], w_o[A_HEADS * A_HEAD_DIM:],
                    row(ln2_g[l]), row(ln2_b[l]))
        h = _ffn_ln(h, ffn2_w_gate[l].astype(BF16), ffn2_w_up[l].astype(BF16),
                    ffn2_w_down[l].astype(BF16), row(ln3_g[l]), row(ln3_b[l]))
    return h.reshape(bsz, seq, d)
```

```python
import functools
import math

import jax
import jax.numpy as jnp
from jax import lax
from jax.experimental import pallas as pl
from jax.experimental.pallas import tpu as pltpu

F32 = jnp.float32
BF16 = jnp.bfloat16

A_HEADS = 8
A_HEAD_DIM = 128
Q_LORA = 512
KV_LORA = 256
IDX_HEADS = 16
IDX_DIM = 128
TOPK_MAX = 256
B_HEADS = 8
B_HEAD_DIM = 128
B_WIDTH = B_HEADS * B_HEAD_DIM
CONV_WIDTH = 4
DEPTH = 1
DEEPNORM_ALPHA = (2.0 * DEPTH) ** 0.25
EPS = 1e-6

LANES = 128
SUBLANES = 8
VMEM_LIMIT_BYTES = 56 * 1024 * 1024

COL_QKV = 0
COL_Z = 3 * B_WIDTH
COL_CQ = COL_Z + B_WIDTH
COL_CKV = COL_CQ + Q_LORA
COL_KIDX = COL_CKV + KV_LORA
COL_SMALL = COL_KIDX + IDX_DIM
D_IN_PAD = COL_SMALL + LANES
SMALL_W = 0
SMALL_A = IDX_HEADS
SMALL_B = IDX_HEADS + B_HEADS

Q_BLOCK = 128
S_TILE = 256
COUNT_TILE = 2 * S_TILE
IDX_GROUP = 2
ATT_GROUP = 2
GDN_CHUNK = 128
INT_MIN = -2 ** 31
F32_LOWEST = -3.4028234663852886e38
LOG2E = math.log2(math.e)


def _dot(a, b):
    return jnp.dot(a, b, preferred_element_type=F32)


def _dot_nt(a, b):
    return lax.dot_general(a, b, (((1,), (1,)), ((), ())), preferred_element_type=F32)


def _sigmoid(x):
    return 1.0 / (1.0 + jnp.exp(-x))


def _layer_norm(y, g, b):
    mu = jnp.mean(y, axis=-1, keepdims=True)
    yc = y - mu
    var = jnp.mean(yc * yc, axis=-1, keepdims=True)
    return yc * lax.rsqrt(var + EPS) * g + b


def _params(*sem):
    return pltpu.CompilerParams(dimension_semantics=sem, vmem_limit_bytes=VMEM_LIMIT_BYTES)


def _ffn_ln_body(x_ref, wg_ref, wu_ref, wd_ref, g_ref, b_ref, o_ref, xb_ref, acc_ref):
    j = pl.program_id(1)

    @pl.when(j == 0)
    def _():
        xb_ref[...] = x_ref[...].astype(BF16)
        acc_ref[...] = jnp.zeros_like(acc_ref)

    xb = xb_ref[...]
    gate = _dot(xb, wg_ref[...])
    up = _dot(xb, wu_ref[...])
    act = gate * _sigmoid(gate) * up
    acc_ref[...] += _dot(act.astype(BF16), wd_ref[...])

    @pl.when(j == pl.num_programs(1) - 1)
    def _():
        y = DEEPNORM_ALPHA * x_ref[...] + 0.5 * acc_ref[...]
        o_ref[...] = _layer_norm(y, g_ref[...], b_ref[...])


def _ffn_ln(x, wg, wu, wd, g, b, *, tm=512, tf=512):
    t, d = x.shape
    f = wg.shape[1]
    return pl.pallas_call(
        _ffn_ln_body,
        out_shape=jax.ShapeDtypeStruct((t, d), F32),
        grid=(t // tm, f // tf),
        in_specs=[
            pl.BlockSpec((tm, d), lambda i, j: (i, 0)),
            pl.BlockSpec((d, tf), lambda i, j: (0, j)),
            pl.BlockSpec((d, tf), lambda i, j: (0, j)),
            pl.BlockSpec((tf, d), lambda i, j: (j, 0)),
            pl.BlockSpec((1, d), lambda i, j: (0, 0)),
            pl.BlockSpec((1, d), lambda i, j: (0, 0)),
        ],
        out_specs=pl.BlockSpec((tm, d), lambda i, j: (i, 0)),
        scratch_shapes=[pltpu.VMEM((tm, d), BF16), pltpu.VMEM((tm, d), F32)],
        compiler_params=_params("parallel", "arbitrary"),
        name="ffn_ln",
    )(x, wg, wu, wd, g, b)


def _in_proj_body(x_ref, w_ref, o_ref, *, tn):
    xb = x_ref[...].astype(BF16)
    for c in range(0, w_ref.shape[1], tn):
        o_ref[:, c:c + tn] = _dot(xb, w_ref[:, c:c + tn])


def _in_proj(x, w, *, tm=512, tn=1024):
    t, d = x.shape
    n = w.shape[1]
    return pl.pallas_call(
        functools.partial(_in_proj_body, tn=tn),
        out_shape=jax.ShapeDtypeStruct((t, n), F32),
        grid=(t // tm,),
        in_specs=[
            pl.BlockSpec((tm, d), lambda i: (i, 0)),
            pl.BlockSpec((d, n), lambda i: (0, 0), pipeline_mode=pl.Buffered(1)),
        ],
        out_specs=pl.BlockSpec((tm, n), lambda i: (i, 0)),
        compiler_params=_params("parallel"),
        name="in_proj",
    )(x, w)


def _dsa_prep_body(cq_ref, ckv_ref, kidx_ref, small_ref, qg_ref, kvg_ref, wuq_ref, wuk_ref,
                   wqi_ref, lng_ref, lnb_ref,
                   qlat_ref, qidx_ref, ckv_o_ref, ckvt_o_ref, kidx_o_ref, wt_o_ref):
    tm = cq_ref.shape[0]
    nb = tm // Q_BLOCK

    cq = cq_ref[...]
    cq = cq * lax.rsqrt(jnp.mean(cq * cq, axis=-1, keepdims=True) + EPS) * qg_ref[...]
    cqb = cq.astype(BF16)

    q = _dot(cqb, wuq_ref[...]).astype(BF16)
    for h in range(A_HEADS):
        ql = _dot(q[:, h * A_HEAD_DIM:(h + 1) * A_HEAD_DIM], wuk_ref[h])
        ql = ql * (A_HEAD_DIM ** -0.5 * LOG2E)
        qlat_ref[:, h] = ql.astype(BF16).reshape(nb, Q_BLOCK, KV_LORA)

    qi = _dot(cqb, wqi_ref[...]).astype(BF16)
    for h in range(IDX_HEADS):
        qidx_ref[:, h] = qi[:, h * IDX_DIM:(h + 1) * IDX_DIM].reshape(nb, Q_BLOCK, IDX_DIM)

    ckv = ckv_ref[...]
    ckv = ckv * lax.rsqrt(jnp.mean(ckv * ckv, axis=-1, keepdims=True) + EPS) * kvg_ref[...]
    ckv_o_ref[...] = ckv.astype(BF16)
    ckvt_o_ref[0] = ckv.T.astype(BF16)

    kidx_o_ref[...] = _layer_norm(kidx_ref[...], lng_ref[...], lnb_ref[...]).astype(BF16)

    wt = small_ref[...].T
    wt_o_ref[...] = wt[SMALL_W:SMALL_W + IDX_HEADS, :] * (IDX_HEADS ** -0.5 * IDX_DIM ** -0.5)


def _dsa_prep(proj, qg, kvg, wuq, wuk, wqi, lng, lnb, *, tm=S_TILE):
    t = proj.shape[0]
    nq = t // Q_BLOCK
    full = lambda *shape: pl.BlockSpec(shape, lambda i: (0,) * len(shape))
    return pl.pallas_call(
        _dsa_prep_body,
        out_shape=(
            jax.ShapeDtypeStruct((nq, A_HEADS, Q_BLOCK, KV_LORA), BF16),
            jax.ShapeDtypeStruct((nq, IDX_HEADS, Q_BLOCK, IDX_DIM), BF16),
            jax.ShapeDtypeStruct((t, KV_LORA), BF16),
            jax.ShapeDtypeStruct((t // tm, KV_LORA, tm), BF16),
            jax.ShapeDtypeStruct((t, IDX_DIM), BF16),
            jax.ShapeDtypeStruct((IDX_HEADS, t), F32),
        ),
        grid=(t // tm,),
        in_specs=[
            pl.BlockSpec((tm, Q_LORA), lambda i: (i, COL_CQ // Q_LORA)),
            pl.BlockSpec((tm, KV_LORA), lambda i: (i, COL_CKV // KV_LORA)),
            pl.BlockSpec((tm, IDX_DIM), lambda i: (i, COL_KIDX // IDX_DIM)),
            pl.BlockSpec((tm, LANES), lambda i: (i, COL_SMALL // LANES)),
            full(1, Q_LORA), full(1, KV_LORA),
            full(Q_LORA, A_HEADS * A_HEAD_DIM),
            full(A_HEADS, A_HEAD_DIM, KV_LORA),
            full(Q_LORA, IDX_HEADS * IDX_DIM),
            full(1, IDX_DIM), full(1, IDX_DIM),
        ],
        out_specs=(
            pl.BlockSpec((tm // Q_BLOCK, A_HEADS, Q_BLOCK, KV_LORA), lambda i: (i, 0, 0, 0)),
            pl.BlockSpec((tm // Q_BLOCK, IDX_HEADS, Q_BLOCK, IDX_DIM), lambda i: (i, 0, 0, 0)),
            pl.BlockSpec((tm, KV_LORA), lambda i: (i, 0)),
            pl.BlockSpec((1, KV_LORA, tm), lambda i: (i, 0, 0)),
            pl.BlockSpec((tm, IDX_DIM), lambda i: (i, 0)),
            pl.BlockSpec((IDX_HEADS, tm), lambda i: (0, i)),
        ),
        compiler_params=_params("parallel"),
        name="dsa_prep",
    )(proj, proj, proj, proj, qg, kvg, wuq, wuk, wqi, lng, lnb)


def _key_to_f32(key):
    return pltpu.bitcast(key ^ ((key >> 31) & 0x7FFFFFFF), F32)


def _dsa_attn_body(qidx_ref, qlat_ref, wt_ref, kidx_ref, ckv_ref, ckvt_ref, wuv_ref, o_ref,
                   sc_ref, bias_ref, m_ref, l_ref, acc_ref, *, n_sel):
    i = pl.program_id(1)
    t0 = i * Q_BLOCK
    n_tiles = (t0 + Q_BLOCK + S_TILE - 1) // S_TILE

    t_lane = t0 + lax.broadcasted_iota(jnp.int32, (S_TILE, Q_BLOCK), 1)
    s_row = lax.broadcasted_iota(jnp.int32, (S_TILE, Q_BLOCK), 0)

    def tile_rows(j):
        return pl.ds(pl.multiple_of(j * S_TILE, S_TILE), S_TILE)

    wt = wt_ref[...]

    def score_tile(j, carry):
        k_tile = kidx_ref[tile_rows(j), :]
        sc = jnp.zeros((S_TILE, Q_BLOCK), F32)
        for g in range(IDX_HEADS // IDX_GROUP):
            qg = qidx_ref[0, g * IDX_GROUP:(g + 1) * IDX_GROUP].reshape(IDX_GROUP * Q_BLOCK, IDX_DIM)
            logits = _dot_nt(k_tile, qg)
            for u in range(IDX_GROUP):
                h = g * IDX_GROUP + u
                sc = sc + wt[h:h + 1, :] * jnp.maximum(logits[:, u * Q_BLOCK:(u + 1) * Q_BLOCK], 0.0)
        sc_ref[tile_rows(j), :] = jnp.where(s_row + j * S_TILE <= t_lane, sc, -jnp.inf)
        return carry

    lax.fori_loop(0, n_tiles, score_tile, 0)

    @pl.when((n_tiles & 1) == 1)
    def _():
        sc_ref[tile_rows(n_tiles), :] = jnp.full((S_TILE, Q_BLOCK), -jnp.inf, F32)

    def count(pred):
        def body(j, acc):
            rows = pl.ds(pl.multiple_of(j * COUNT_TILE, COUNT_TILE), COUNT_TILE)
            hit = jnp.where(pred(sc_ref[rows, :]), 1, 0)
            return acc + jnp.sum(hit.reshape(COUNT_TILE // SUBLANES, SUBLANES, Q_BLOCK), axis=0)

        acc = lax.fori_loop(0, (n_tiles + 1) // 2, body, jnp.zeros((SUBLANES, Q_BLOCK), jnp.int32))
        return jnp.sum(acc, axis=0, keepdims=True)

    def try_key(cand_key, key, n_ge):
        cand = _key_to_f32(cand_key)
        n = count(lambda sc: sc >= cand)
        ok = n >= n_sel
        return jnp.where(ok, cand_key, key), jnp.where(ok, n, n_ge)

    zero = jnp.zeros((1, Q_BLOCK), jnp.int32)
    key, n_ge = try_key(zero, zero + INT_MIN, zero)

    def bit_step(b, carry):
        return try_key(carry[0] | (1 << (30 - b)), *carry)

    key, n_ge = lax.fori_loop(0, 31, bit_step, (key, n_ge))
    thr = _key_to_f32(key)
    thr = jnp.maximum(jnp.where(thr != thr, F32_LOWEST, thr), F32_LOWEST)

    @pl.when(jnp.max(n_ge) > n_sel)
    def _():
        need = (n_sel - count(lambda sc: sc > thr)).astype(F32)
        r = lax.broadcasted_iota(jnp.int32, (S_TILE, S_TILE), 0)
        c = lax.broadcasted_iota(jnp.int32, (S_TILE, S_TILE), 1)
        lower_incl = jnp.where(c <= r, 1.0, 0.0).astype(BF16)

        def body(j, run):
            sc = sc_ref[tile_rows(j), :]
            tied = sc == thr
            rank = run + _dot(lower_incl, jnp.where(tied, 1.0, 0.0).astype(BF16))
            sc_ref[tile_rows(j), :] = jnp.where(tied & (rank > need), -jnp.inf, sc)
            return rank[S_TILE - 1:S_TILE, :]

        lax.fori_loop(0, n_tiles, body, jnp.zeros((1, Q_BLOCK), F32))

    gw = ATT_GROUP * Q_BLOCK
    groups = range(A_HEADS // ATT_GROUP)
    lane = lax.broadcasted_iota(jnp.int32, (1, gw), 1)
    slope_rows = []
    for g in groups:
        row_g = jnp.zeros((1, gw), F32)
        for u in range(ATT_GROUP):
            slope = 2.0 ** (-8.0 * (g * ATT_GROUP + u + 1) / A_HEADS) * LOG2E
            row_g = jnp.where(lane >= u * Q_BLOCK, slope, row_g)
        slope_rows.append(row_g)
    s_rel = lax.broadcasted_iota(jnp.int32, (S_TILE, gw), 0).astype(F32)
    for g in groups:
        bias_ref[g] = slope_rows[g] * s_rel
    m_ref[...] = jnp.full(m_ref.shape, -1e30, F32)
    l_ref[...] = jnp.zeros_like(l_ref)
    acc_ref[...] = jnp.zeros_like(acc_ref)

    def att_tile(j, carry):
        drop = jnp.where(sc_ref[tile_rows(j), :] >= thr, 0.0, -jnp.inf)
        drop = jnp.concatenate([drop] * ATT_GROUP, axis=1)
        ckv_tile = ckv_ref[tile_rows(j), :]
        ckvt_tile = ckvt_ref[j]
        off = (j * S_TILE - t0).astype(F32)
        logits = [_dot_nt(ckv_tile, qlat_ref[0, g * ATT_GROUP:(g + 1) * ATT_GROUP].reshape(gw, KV_LORA))
                  for g in groups]
        probs, alphas = [], []
        for g in groups:
            a = logits[g] + bias_ref[g] + drop
            shift = slope_rows[g] * off
            m_old = m_ref[g:g + 1, :]
            m_new = jnp.maximum(m_old, jnp.max(a, axis=0, keepdims=True) + shift)
            alpha = jnp.exp2(m_old - m_new)
            p = jnp.exp2(a - (m_new - shift))
            l_ref[g:g + 1, :] = alpha * l_ref[g:g + 1, :] + jnp.sum(p, axis=0, keepdims=True)
            m_ref[g:g + 1, :] = m_new
            probs.append(p.astype(BF16))
            alphas.append(alpha)
        for g in groups:
            acc_ref[g] = alphas[g] * acc_ref[g] + _dot(ckvt_tile, probs[g])
        return carry

    lax.fori_loop(0, n_tiles, att_tile, 0)

    for g in groups:
        o_lat_t = acc_ref[g] * (1.0 / l_ref[g:g + 1, :])
        for u in range(ATT_GROUP):
            h = g * ATT_GROUP + u
            o_lat = o_lat_t[:, u * Q_BLOCK:(u + 1) * Q_BLOCK].T.astype(BF16)
            o_ref[:, h * A_HEAD_DIM:(h + 1) * A_HEAD_DIM] = _dot(o_lat, wuv_ref[h]).astype(BF16)


def _dsa_attn(qlat, qidx, wt, kidx, ckv, ckvt, wuv, *, bsz, seq):
    nq = seq // Q_BLOCK
    ns = seq // S_TILE
    n_sel = min(TOPK_MAX, seq // 4)
    ng, gw = A_HEADS // ATT_GROUP, ATT_GROUP * Q_BLOCK
    return pl.pallas_call(
        functools.partial(_dsa_attn_body, n_sel=n_sel),
        out_shape=jax.ShapeDtypeStruct((bsz * seq, A_HEADS * A_HEAD_DIM), BF16),
        grid=(bsz, nq),
        in_specs=[
            pl.BlockSpec((1, IDX_HEADS, Q_BLOCK, IDX_DIM), lambda b, i: (b * nq + i, 0, 0, 0)),
            pl.BlockSpec((1, A_HEADS, Q_BLOCK, KV_LORA), lambda b, i: (b * nq + i, 0, 0, 0)),
            pl.BlockSpec((IDX_HEADS, Q_BLOCK), lambda b, i: (0, b * nq + i)),
            pl.BlockSpec((seq, IDX_DIM), lambda b, i: (b, 0)),
            pl.BlockSpec((seq, KV_LORA), lambda b, i: (b, 0)),
            pl.BlockSpec((ns, KV_LORA, S_TILE), lambda b, i: (b, 0, 0)),
            pl.BlockSpec((A_HEADS, KV_LORA, A_HEAD_DIM), lambda b, i: (0, 0, 0)),
        ],
        out_specs=pl.BlockSpec((Q_BLOCK, A_HEADS * A_HEAD_DIM), lambda b, i: (b * nq + i, 0)),
        scratch_shapes=[
            pltpu.VMEM((seq, Q_BLOCK), F32),
            pltpu.VMEM((ng, S_TILE, gw), F32),
            pltpu.VMEM((ng, gw), F32),
            pltpu.VMEM((ng, gw), F32),
            pltpu.VMEM((ng, KV_LORA, gw), F32),
        ],
        compiler_params=_params("parallel", "arbitrary"),
        name="dsa_attn",
    )(qidx, qlat, wt, kidx, ckv, ckvt, wuv)


def _gdn_prep_body(x_ref, halo_ref, small_ref, cw_ref, alog_ref, dtb_ref,
                   q_ref, k_ref, v_ref, gcol_ref, grow_ref, xx_ref):
    ts = x_ref.shape[0]
    first = pl.program_id(1) == 0
    halo = halo_ref[...]
    xx_ref[0:SUBLANES, :] = jnp.where(first, jnp.zeros_like(halo), halo)
    xx_ref[SUBLANES:, :] = x_ref[...]
    cw = cw_ref[...]
    y = jnp.zeros(x_ref.shape, F32)
    for tap in range(CONV_WIDTH):
        off = SUBLANES - (CONV_WIDTH - 1) + tap
        y = y + cw[tap:tap + 1, :] * xx_ref[off:off + ts, :]
    y = y * _sigmoid(y)

    for h in range(B_HEADS):
        lo, hi = h * B_HEAD_DIM, (h + 1) * B_HEAD_DIM
        qh = y[:, lo:hi]
        q_ref[:, lo:hi] = qh * lax.rsqrt(jnp.sum(qh * qh, axis=-1, keepdims=True) + EPS) * (
            B_HEAD_DIM ** -0.5)
        kh = y[:, B_WIDTH + lo:B_WIDTH + hi]
        k_ref[:, lo:hi] = kh * lax.rsqrt(jnp.sum(kh * kh, axis=-1, keepdims=True) + EPS)
    v_ref[...] = y[:, 2 * B_WIDTH:]

    small = small_ref[...]
    pre = small + dtb_ref[...]
    softplus = jnp.maximum(pre, 0.0) + jnp.log1p(jnp.exp(-jnp.abs(pre)))
    g = -jnp.exp(alog_ref[...]) * softplus
    row = lax.broadcasted_iota(jnp.int32, g.shape, 0) & (GDN_CHUNK - 1)
    shift = 1
    while shift < GDN_CHUNK:
        g = g + jnp.where(row >= shift, pltpu.roll(g, shift, 0), 0.0)
        shift *= 2
    lane = lax.broadcasted_iota(jnp.int32, g.shape, 1)
    is_a = (lane >= SMALL_A) & (lane < SMALL_A + B_HEADS)
    gb = jnp.where(is_a, g, _sigmoid(small))
    gcol_ref[...] = gb
    grow_ref[...] = gb.T


def _gdn_prep(proj, conv_w, alog_p, dtb_p, *, bsz, seq, ts=256):
    t = bsz * seq
    ns = seq // ts
    c = 3 * B_WIDTH
    return pl.pallas_call(
        _gdn_prep_body,
        out_shape=(
            jax.ShapeDtypeStruct((t, B_WIDTH), F32),
            jax.ShapeDtypeStruct((t, B_WIDTH), F32),
            jax.ShapeDtypeStruct((t, B_WIDTH), F32),
            jax.ShapeDtypeStruct((t, LANES), F32),
            jax.ShapeDtypeStruct((LANES, t), F32),
        ),
        grid=(bsz, ns),
        in_specs=[
            pl.BlockSpec((ts, c), lambda b, i: (b * ns + i, COL_QKV // c)),
            pl.BlockSpec((SUBLANES, c),
                         lambda b, i: (jnp.maximum((b * ns + i) * (ts // SUBLANES) - 1, 0), 0)),
            pl.BlockSpec((ts, LANES), lambda b, i: (b * ns + i, COL_SMALL // LANES)),
            pl.BlockSpec((CONV_WIDTH, c), lambda b, i: (0, 0)),
            pl.BlockSpec((1, LANES), lambda b, i: (0, 0)),
            pl.BlockSpec((1, LANES), lambda b, i: (0, 0)),
        ],
        out_specs=(
            pl.BlockSpec((ts, B_WIDTH), lambda b, i: (b * ns + i, 0)),
            pl.BlockSpec((ts, B_WIDTH), lambda b, i: (b * ns + i, 0)),
            pl.BlockSpec((ts, B_WIDTH), lambda b, i: (b * ns + i, 0)),
            pl.BlockSpec((ts, LANES), lambda b, i: (b * ns + i, 0)),
            pl.BlockSpec((LANES, ts), lambda b, i: (0, b * ns + i)),
        ),
        scratch_shapes=[pltpu.VMEM((ts + SUBLANES, c), F32)],
        compiler_params=_params("parallel", "arbitrary"),
        name="gdn_prep",
    )(proj, proj, proj, conv_w, alog_p, dtb_p)


def _split_bf16(a):
    hi = a.astype(BF16)
    lo = (a - hi.astype(F32)).astype(BF16)
    return hi, lo


def _dot3(a, b):
    ah, al = _split_bf16(a)
    bh, bl = _split_bf16(b)
    return _dot(ah, bh) + (_dot(ah, bl) + _dot(al, bh))


def _dot1(a, b):
    return _dot(a.astype(BF16), b.astype(BF16))


def _each(fn, *lists):
    return [fn(*args) for args in zip(*lists)]


def _unit_lower_inverses(mats, row, col):
    base_log2 = 4
    eye = jnp.where(row == col, 1.0, 0.0)
    in_block = (row >> base_log2) == (col >> base_log2)
    d = [jnp.where(in_block, a, 0.0) for a in mats]
    x = [eye - dh for dh in d]
    p = _each(_dot1, d, d)
    for step in range(base_log2 - 1):
        x = _each(lambda xh, th: xh + th, x, _each(_dot1, x, p))
        if step < base_log2 - 2:
            p = _each(_dot1, p, p)
    sh = base_log2
    while (1 << sh) < GDN_CHUNK:
        same_parent = (row >> (sh + 1)) == (col >> (sh + 1))
        same_block = (row >> sh) == (col >> sh)
        quad = [jnp.where(same_block, 0.0, jnp.where(same_parent, a, 0.0)) for a in mats]
        x = _each(lambda xh, th: xh - th, x, _each(_dot1, x, _each(_dot1, quad, x)))
        sh += 1
    return x


def _gdn_chunk_body(q_ref, k_ref, v_ref, z_ref, gcol_ref, grow_ref, ng_ref, o_ref, state_ref):
    @pl.when(pl.program_id(1) == 0)
    def _():
        state_ref[...] = jnp.zeros_like(state_ref)

    c = GDN_CHUNK
    heads = range(B_HEADS)
    span = lambda h: slice(h * B_HEAD_DIM, (h + 1) * B_HEAD_DIM)
    row = lax.broadcasted_iota(jnp.int32, (c, c), 0)
    col = lax.broadcasted_iota(jnp.int32, (c, c), 1)
    gcol = gcol_ref[...]
    grow = grow_ref[...]
    e_cum = jnp.exp(gcol)
    e_rest = jnp.exp(gcol[c - 1:c, :] - gcol)
    e_last = jnp.exp(grow[:, c - 1:c])

    q = [q_ref[:, span(h)] for h in heads]
    k = [k_ref[:, span(h)] for h in heads]
    v = [v_ref[:, span(h)] for h in heads]
    beta = [gcol[:, SMALL_B + h:SMALL_B + h + 1] for h in heads]
    decay = [jnp.exp(jnp.where(row >= col,
                               gcol[:, SMALL_A + h:SMALL_A + h + 1] - grow[SMALL_A + h:SMALL_A + h + 1, :],
                               -jnp.inf)) for h in heads]
    k_beta = _each(lambda kh, bh: kh * bh, k, beta)
    k16 = [kh.astype(BF16) for kh in k]
    kk = _each(_dot_nt, [kb.astype(BF16) for kb in k_beta], k16)
    qk = _each(_dot_nt, [qh.astype(BF16) for qh in q], k16)
    a_mat = _each(lambda m, dh: jnp.where(row > col, m * dh, 0.0), kk, decay)
    attn = _each(lambda m, dh: (m * dh).astype(BF16), qk, decay)
    t_inv = _unit_lower_inverses(a_mat, row, col)
    rhs = [jnp.concatenate([v[h] * beta[h], k_beta[h] * e_cum[:, SMALL_A + h:SMALL_A + h + 1]], axis=1)
           for h in heads]
    sol = _each(_dot3, t_inv, rhs)
    q_dec = [(q[h] * e_cum[:, SMALL_A + h:SMALL_A + h + 1]).astype(BF16) for h in heads]
    k_dec_t = [(k[h] * e_rest[:, SMALL_A + h:SMALL_A + h + 1]).T.astype(BF16) for h in heads]

    state = [state_ref[h] for h in heads]
    s16 = [s.astype(BF16) for s in state]
    v_new = [sol[h][:, :B_HEAD_DIM] - _dot(sol[h][:, B_HEAD_DIM:].astype(BF16), s16[h]) for h in heads]
    v16 = [x.astype(BF16) for x in v_new]
    for h in heads:
        state_ref[h] = state[h] * e_last[SMALL_A + h:SMALL_A + h + 1, :] + _dot(k_dec_t[h], v16[h])
    out = [_dot(q_dec[h], s16[h]) + _dot(attn[h], v16[h]) for h in heads]
    for h in heads:
        o = out[h]
        y = o * lax.rsqrt(jnp.mean(o * o, axis=-1, keepdims=True) + EPS) * ng_ref[...]
        zh = z_ref[:, span(h)]
        o_ref[:, span(h)] = (y * (zh * _sigmoid(zh))).astype(BF16)


def _gdn_chunk(q, k, v, proj, gcol, grow, ng, *, bsz, seq):
    c = GDN_CHUNK
    nc = seq // c
    tok = lambda b, i: (b * nc + i, 0)
    return pl.pallas_call(
        _gdn_chunk_body,
        out_shape=jax.ShapeDtypeStruct((bsz * seq, B_WIDTH), BF16),
        grid=(bsz, nc),
        in_specs=[
            pl.BlockSpec((c, B_WIDTH), tok),
            pl.BlockSpec((c, B_WIDTH), tok),
            pl.BlockSpec((c, B_WIDTH), tok),
            pl.BlockSpec((c, B_WIDTH), lambda b, i: (b * nc + i, COL_Z // B_WIDTH)),
            pl.BlockSpec((c, LANES), tok),
            pl.BlockSpec((LANES, c), lambda b, i: (0, b * nc + i)),
            pl.BlockSpec((1, B_HEAD_DIM), lambda b, i: (0, 0)),
        ],
        out_specs=pl.BlockSpec((c, B_WIDTH), tok),
        scratch_shapes=[pltpu.VMEM((B_HEADS, B_HEAD_DIM, B_HEAD_DIM), F32)],
        compiler_params=_params("parallel", "arbitrary"),
        name="gdn_chunk",
    )(q, k, v, proj, gcol, grow, ng)


def _out_ln_body(h_ref, oa_ref, ob_ref, wa_ref, wb_ref, g_ref, b_ref, o_ref, *, sub):
    spans = [slice(r, r + sub) for r in range(0, h_ref.shape[0], sub)]
    mixes = [_dot(oa_ref[s, :], wa_ref[...]) + _dot(ob_ref[s, :], wb_ref[...]) for s in spans]
    for s, mix in zip(spans, mixes):
        o_ref[s, :] = _layer_norm(DEEPNORM_ALPHA * h_ref[s, :] + mix, g_ref[...], b_ref[...])


def _out_ln(h, oa, ob, wa, wb, g, b, *, tm=512, sub=256):
    t, d = h.shape
    da, db = oa.shape[1], ob.shape[1]
    return pl.pallas_call(
        functools.partial(_out_ln_body, sub=sub),
        out_shape=jax.ShapeDtypeStruct((t, d), F32),
        grid=(t // tm,),
        in_specs=[
            pl.BlockSpec((tm, d), lambda i: (i, 0)),
            pl.BlockSpec((tm, da), lambda i: (i, 0)),
            pl.BlockSpec((tm, db), lambda i: (i, 0)),
            pl.BlockSpec((da, d), lambda i: (0, 0)),
            pl.BlockSpec((db, d), lambda i: (0, 0)),
            pl.BlockSpec((1, d), lambda i: (0, 0)),
            pl.BlockSpec((1, d), lambda i: (0, 0)),
        ],
        out_specs=pl.BlockSpec((tm, d), lambda i: (i, 0)),
        compiler_params=_params("parallel"),
        name="out_ln",
    )(h, oa, ob, wa, wb, g, b)


def _regroup_w_in(w_in):
    cuts = [0, Q_LORA, KV_LORA, IDX_DIM, IDX_HEADS, B_WIDTH, B_WIDTH, B_WIDTH, B_WIDTH, B_HEADS, B_HEADS]
    offs = [sum(cuts[:n + 1]) for n in range(len(cuts))]
    part = lambda n: w_in[:, offs[n]:offs[n + 1]]
    c_q, c_kv, k_idx, w_idx, q_b, k_b, v_b, z_b, a_b, b_b = [part(n) for n in range(10)]
    pad = jnp.zeros((w_in.shape[0], LANES - IDX_HEADS - 2 * B_HEADS), w_in.dtype)
    return jnp.concatenate([q_b, k_b, v_b, z_b, c_q, c_kv, k_idx, w_idx, a_b, b_b, pad],
                           axis=1).astype(BF16)


def _lane_pad(vec, offset):
    out = jnp.zeros((1, LANES), F32)
    return out.at[0, offset:offset + vec.shape[0]].set(vec.astype(F32))


def kernel(x, ffn1_w_gate, ffn1_w_up, ffn1_w_down, ln1_g, ln1_b, w_in, q_norm_g, kv_norm_g, w_uq, w_uk, w_uv, w_q_idx, k_idx_ln_g, k_idx_ln_b, conv_w, a_log, dt_bias, o_norm_g, w_out, ln2_g, ln2_b, ffn2_w_gate, ffn2_w_up, ffn2_w_down, ln3_g, ln3_b):
    bsz, seq, d = x.shape
    assert seq % COUNT_TILE == 0 and seq % GDN_CHUNK == 0 and x.dtype == F32
    t = bsz * seq
    row = lambda p: p.reshape(1, -1).astype(F32)
    h = x.reshape(t, d)
    for l in range(DEPTH):
        h = _ffn_ln(h, ffn1_w_gate[l].astype(BF16), ffn1_w_up[l].astype(BF16),
                    ffn1_w_down[l].astype(BF16), row(ln1_g[l]), row(ln1_b[l]))
        proj = _in_proj(h, _regroup_w_in(w_in[l]))
        qlat, qidx, ckv, ckvt, kidx, wt = _dsa_prep(
            proj, row(q_norm_g[l]), row(kv_norm_g[l]),
            w_uq[l].reshape(Q_LORA, A_HEADS * A_HEAD_DIM).astype(BF16), w_uk[l].astype(BF16),
            w_q_idx[l].astype(BF16), row(k_idx_ln_g[l]), row(k_idx_ln_b[l]))
        o_a = _dsa_attn(qlat, qidx, wt, kidx, ckv, ckvt, w_uv[l].astype(BF16), bsz=bsz, seq=seq)
        gq, gk, gv, gcol, grow = _gdn_prep(proj, conv_w[l].astype(F32), _lane_pad(a_log[l], SMALL_A),
                                           _lane_pad(dt_bias[l], SMALL_A), bsz=bsz, seq=seq)
        o_b = _gdn_chunk(gq, gk, gv, proj, gcol, grow, row(o_norm_g[l]), bsz=bsz, seq=seq)
        w_o = w_out[l].astype(BF16)
        h = _out_ln(h, o_a, o_b, w_o[:A_HEADS * A_HEAD_DIM], w_o[A_HEADS * A_HEAD_DIM:],
                    row(ln2_g[l]), row(ln2_b[l]))
        h = _ffn_ln(h, ffn2_w_gate[l].astype(BF16), ffn2_w_up[l].astype(BF16),
                    ffn2_w_down[l].astype(BF16), row(ln3_g[l]), row(ln3_b[l]))
    return h.reshape(bsz, seq, d)
```

```python
import functools
import math

import jax
import jax.numpy as jnp
from jax import lax
from jax.experimental import pallas as pl
from jax.experimental.pallas import tpu as pltpu

F32 = jnp.float32
BF16 = jnp.bfloat16

A_HEADS = 8
A_HEAD_DIM = 128
Q_LORA = 512
KV_LORA = 256
IDX_HEADS = 16
IDX_DIM = 128
TOPK_MAX = 256
B_HEADS = 8
B_HEAD_DIM = 128
B_WIDTH = B_HEADS * B_HEAD_DIM
CONV_WIDTH = 4
DEPTH = 1
DEEPNORM_ALPHA = (2.0 * DEPTH) ** 0.25
EPS = 1e-6

LANES = 128
SUBLANES = 8
VMEM_LIMIT_BYTES = 56 * 1024 * 1024

COL_QKV = 0
COL_Z = 3 * B_WIDTH
COL_CQ = COL_Z + B_WIDTH
COL_CKV = COL_CQ + Q_LORA
COL_KIDX = COL_CKV + KV_LORA
COL_SMALL = COL_KIDX + IDX_DIM
D_IN_PAD = COL_SMALL + LANES
SMALL_W = 0
SMALL_A = IDX_HEADS
SMALL_B = IDX_HEADS + B_HEADS

Q_BLOCK = 128
S_TILE = 512
IDX_GROUP = 2
ATT_GROUP = 2
GDN_CHUNK = 128
INT_MIN = -2 ** 31
F32_LOWEST = -3.4028234663852886e38
LOG2E = math.log2(math.e)


def _dot(a, b):
    return jnp.dot(a, b, preferred_element_type=F32)


def _dot_nt(a, b):
    return lax.dot_general(a, b, (((1,), (1,)), ((), ())), preferred_element_type=F32)


def _sigmoid(x):
    return 1.0 / (1.0 + jnp.exp(-x))


def _layer_norm(y, g, b):
    mu = jnp.mean(y, axis=-1, keepdims=True)
    yc = y - mu
    var = jnp.mean(yc * yc, axis=-1, keepdims=True)
    return yc * lax.rsqrt(var + EPS) * g + b


def _params(*sem):
    return pltpu.CompilerParams(dimension_semantics=sem, vmem_limit_bytes=VMEM_LIMIT_BYTES)


def _ffn_ln_body(x_ref, wg_ref, wu_ref, wd_ref, g_ref, b_ref, o_ref, xb_ref, acc_ref):
    j = pl.program_id(1)

    @pl.when(j == 0)
    def _():
        xb_ref[...] = x_ref[...].astype(BF16)
        acc_ref[...] = jnp.zeros_like(acc_ref)

    xb = xb_ref[...]
    gate = _dot(xb, wg_ref[...])
    up = _dot(xb, wu_ref[...])
    act = gate * _sigmoid(gate) * up
    acc_ref[...] += _dot(act.astype(BF16), wd_ref[...])

    @pl.when(j == pl.num_programs(1) - 1)
    def _():
        y = DEEPNORM_ALPHA * x_ref[...] + acc_ref[...]
        o_ref[...] = _layer_norm(y, g_ref[...], b_ref[...])


def _ffn_ln(x, wg, wu, wd, g, b, *, tm=512, tf=512):
    t, d = x.shape
    f = wg.shape[1]
    return pl.pallas_call(
        _ffn_ln_body,
        out_shape=jax.ShapeDtypeStruct((t, d), F32),
        grid=(t // tm, f // tf),
        in_specs=[
            pl.BlockSpec((tm, d), lambda i, j: (i, 0)),
            pl.BlockSpec((d, tf), lambda i, j: (0, j)),
            pl.BlockSpec((d, tf), lambda i, j: (0, j)),
            pl.BlockSpec((tf, d), lambda i, j: (j, 0)),
            pl.BlockSpec((1, d), lambda i, j: (0, 0)),
            pl.BlockSpec((1, d), lambda i, j: (0, 0)),
        ],
        out_specs=pl.BlockSpec((tm, d), lambda i, j: (i, 0)),
        scratch_shapes=[pltpu.VMEM((tm, d), BF16), pltpu.VMEM((tm, d), F32)],
        compiler_params=_params("parallel", "arbitrary"),
        name="ffn_ln",
    )(x, wg, wu, wd, g, b)


def _in_proj_body(x_ref, w_ref, o_ref, *, tn):
    xb = x_ref[...].astype(BF16)
    for c in range(0, w_ref.shape[1], tn):
        o_ref[:, c:c + tn] = _dot(xb, w_ref[:, c:c + tn])


def _in_proj(x, w, *, tm=512, tn=1024):
    t, d = x.shape
    n = w.shape[1]
    return pl.pallas_call(
        functools.partial(_in_proj_body, tn=tn),
        out_shape=jax.ShapeDtypeStruct((t, n), F32),
        grid=(t // tm,),
        in_specs=[
            pl.BlockSpec((tm, d), lambda i: (i, 0)),
            pl.BlockSpec((d, n), lambda i: (0, 0), pipeline_mode=pl.Buffered(1)),
        ],
        out_specs=pl.BlockSpec((tm, n), lambda i: (i, 0)),
        compiler_params=_params("parallel"),
        name="in_proj",
    )(x, w)


def _dsa_prep_body(cq_ref, ckv_ref, kidx_ref, small_ref, qg_ref, kvg_ref, wuq_ref, wuk_ref,
                   wqi_ref, lng_ref, lnb_ref,
                   qlat_ref, qidx_ref, ckv_o_ref, ckvt_o_ref, kidx_o_ref, wt_o_ref):
    tm = cq_ref.shape[0]
    nb = tm // Q_BLOCK

    cq = cq_ref[...]
    cq = cq * lax.rsqrt(jnp.mean(cq * cq, axis=-1, keepdims=True) + EPS) * qg_ref[...]
    cqb = cq.astype(BF16)

    q = _dot(cqb, wuq_ref[...]).astype(BF16)
    for h in range(A_HEADS):
        ql = _dot(q[:, h * A_HEAD_DIM:(h + 1) * A_HEAD_DIM], wuk_ref[h])
        ql = ql * (A_HEAD_DIM ** -0.5 * LOG2E)
        qlat_ref[:, h] = ql.astype(BF16).reshape(nb, Q_BLOCK, KV_LORA)

    qi = _dot(cqb, wqi_ref[...]).astype(BF16)
    for h in range(IDX_HEADS):
        qidx_ref[:, h] = qi[:, h * IDX_DIM:(h + 1) * IDX_DIM].reshape(nb, Q_BLOCK, IDX_DIM)

    ckv = ckv_ref[...]
    ckv = ckv * lax.rsqrt(jnp.mean(ckv * ckv, axis=-1, keepdims=True) + EPS) * kvg_ref[...]
    ckv_o_ref[...] = ckv.astype(BF16)
    ckvt_o_ref[0] = ckv.T.astype(BF16)

    kidx_o_ref[...] = _layer_norm(kidx_ref[...], lng_ref[...], lnb_ref[...]).astype(BF16)

    wt = small_ref[...].T
    wt_o_ref[...] = wt[SMALL_W:SMALL_W + IDX_HEADS, :] * (IDX_HEADS ** -0.5 * IDX_DIM ** -0.5)


def _dsa_prep(proj, qg, kvg, wuq, wuk, wqi, lng, lnb, *, tm=S_TILE):
    t = proj.shape[0]
    nq = t // Q_BLOCK
    full = lambda *shape: pl.BlockSpec(shape, lambda i: (0,) * len(shape))
    return pl.pallas_call(
        _dsa_prep_body,
        out_shape=(
            jax.ShapeDtypeStruct((nq, A_HEADS, Q_BLOCK, KV_LORA), BF16),
            jax.ShapeDtypeStruct((nq, IDX_HEADS, Q_BLOCK, IDX_DIM), BF16),
            jax.ShapeDtypeStruct((t, KV_LORA), BF16),
            jax.ShapeDtypeStruct((t // tm, KV_LORA, tm), BF16),
            jax.ShapeDtypeStruct((t, IDX_DIM), BF16),
            jax.ShapeDtypeStruct((IDX_HEADS, t), F32),
        ),
        grid=(t // tm,),
        in_specs=[
            pl.BlockSpec((tm, Q_LORA), lambda i: (i, COL_CQ // Q_LORA)),
            pl.BlockSpec((tm, KV_LORA), lambda i: (i, COL_CKV // KV_LORA)),
            pl.BlockSpec((tm, IDX_DIM), lambda i: (i, COL_KIDX // IDX_DIM)),
            pl.BlockSpec((tm, LANES), lambda i: (i, COL_SMALL // LANES)),
            full(1, Q_LORA), full(1, KV_LORA),
            full(Q_LORA, A_HEADS * A_HEAD_DIM),
            full(A_HEADS, A_HEAD_DIM, KV_LORA),
            full(Q_LORA, IDX_HEADS * IDX_DIM),
            full(1, IDX_DIM), full(1, IDX_DIM),
        ],
        out_specs=(
            pl.BlockSpec((tm // Q_BLOCK, A_HEADS, Q_BLOCK, KV_LORA), lambda i: (i, 0, 0, 0)),
            pl.BlockSpec((tm // Q_BLOCK, IDX_HEADS, Q_BLOCK, IDX_DIM), lambda i: (i, 0, 0, 0)),
            pl.BlockSpec((tm, KV_LORA), lambda i: (i, 0)),
            pl.BlockSpec((1, KV_LORA, tm), lambda i: (i, 0, 0)),
            pl.BlockSpec((tm, IDX_DIM), lambda i: (i, 0)),
            pl.BlockSpec((IDX_HEADS, tm), lambda i: (0, i)),
        ),
        compiler_params=_params("parallel"),
        name="dsa_prep",
    )(proj, proj, proj, proj, qg, kvg, wuq, wuk, wqi, lng, lnb)


def _key_to_f32(key):
    return pltpu.bitcast(key ^ ((key >> 31) & 0x7FFFFFFF), F32)


def _dsa_attn_body(qidx_ref, qlat_ref, wt_ref, kidx_ref, ckv_ref, ckvt_ref, wuv_ref, o_ref,
                   sc_ref, bias_ref, m_ref, l_ref, acc_ref, *, n_sel):
    i = pl.program_id(1)
    t0 = i * Q_BLOCK
    n_tiles = (t0 + Q_BLOCK + S_TILE - 1) // S_TILE

    t_lane = t0 + lax.broadcasted_iota(jnp.int32, (S_TILE, Q_BLOCK), 1)
    s_row = lax.broadcasted_iota(jnp.int32, (S_TILE, Q_BLOCK), 0)

    def tile_rows(j):
        return pl.ds(pl.multiple_of(j * S_TILE, S_TILE), S_TILE)

    wt = wt_ref[...]

    def score_tile(j, carry):
        k_tile = kidx_ref[tile_rows(j), :]
        sc = jnp.zeros((S_TILE, Q_BLOCK), F32)
        for g in range(IDX_HEADS // IDX_GROUP):
            qg = qidx_ref[0, g * IDX_GROUP:(g + 1) * IDX_GROUP].reshape(IDX_GROUP * Q_BLOCK, IDX_DIM)
            logits = _dot_nt(k_tile, qg)
            for u in range(IDX_GROUP):
                h = g * IDX_GROUP + u
                sc = sc + wt[h:h + 1, :] * jnp.maximum(logits[:, u * Q_BLOCK:(u + 1) * Q_BLOCK], 0.0)
        sc_ref[tile_rows(j), :] = jnp.where(s_row + j * S_TILE <= t_lane, sc, -jnp.inf)
        return carry

    lax.fori_loop(0, n_tiles, score_tile, 0)

    def count(pred):
        def body(j, acc):
            hit = jnp.where(pred(sc_ref[tile_rows(j), :]), 1, 0)
            return acc + jnp.sum(hit.reshape(S_TILE // SUBLANES, SUBLANES, Q_BLOCK), axis=0)

        acc = lax.fori_loop(0, n_tiles, body, jnp.zeros((SUBLANES, Q_BLOCK), jnp.int32))
        return jnp.sum(acc, axis=0, keepdims=True)

    def try_key(cand_key, key, n_ge):
        cand = _key_to_f32(cand_key)
        n = count(lambda sc: sc >= cand)
        ok = n >= n_sel
        return jnp.where(ok, cand_key, key), jnp.where(ok, n, n_ge)

    zero = jnp.zeros((1, Q_BLOCK), jnp.int32)
    key, n_ge = try_key(zero, zero + INT_MIN, zero)

    def bit_step(b, carry):
        return try_key(carry[0] | (1 << (30 - b)), *carry)

    key, n_ge = lax.fori_loop(0, 31, bit_step, (key, n_ge))
    thr = _key_to_f32(key)
    thr = jnp.maximum(jnp.where(thr != thr, F32_LOWEST, thr), F32_LOWEST)

    @pl.when(jnp.max(n_ge) > n_sel)
    def _():
        need = (n_sel - count(lambda sc: sc > thr)).astype(F32)
        r = lax.broadcasted_iota(jnp.int32, (S_TILE, S_TILE), 0)
        c = lax.broadcasted_iota(jnp.int32, (S_TILE, S_TILE), 1)
        lower_incl = jnp.where(c <= r, 1.0, 0.0).astype(BF16)

        def body(j, run):
            sc = sc_ref[tile_rows(j), :]
            tied = sc == thr
            rank = run + _dot(lower_incl, jnp.where(tied, 1.0, 0.0).astype(BF16))
            sc_ref[tile_rows(j), :] = jnp.where(tied & (rank > need), -jnp.inf, sc)
            return rank[S_TILE - 1:S_TILE, :]

        lax.fori_loop(0, n_tiles, body, jnp.zeros((1, Q_BLOCK), F32))

    gw = ATT_GROUP * Q_BLOCK
    groups = range(A_HEADS // ATT_GROUP)
    lane = lax.broadcasted_iota(jnp.int32, (1, gw), 1)
    slope_rows = []
    for g in groups:
        row_g = jnp.zeros((1, gw), F32)
        for u in range(ATT_GROUP):
            slope = 2.0 ** (-8.0 * (g * ATT_GROUP + u + 1) / A_HEADS) * LOG2E
            row_g = jnp.where(lane >= u * Q_BLOCK, slope, row_g)
        slope_rows.append(row_g)

    @pl.when(i == 0)
    def _():
        s_rel = lax.broadcasted_iota(jnp.int32, (S_TILE, gw), 0).astype(F32)
        for g in groups:
            bias_ref[g] = slope_rows[g] * s_rel

    m_ref[...] = jnp.full(m_ref.shape, -1e30, F32)
    l_ref[...] = jnp.zeros_like(l_ref)
    acc_ref[...] = jnp.zeros_like(acc_ref)

    def att_tile(j, carry):
        drop = jnp.where(sc_ref[tile_rows(j), :] >= thr, 0.0, -jnp.inf)
        drop = jnp.concatenate([drop] * ATT_GROUP, axis=1)
        ckv_tile = ckv_ref[tile_rows(j), :]
        ckvt_tile = ckvt_ref[j]
        off = (j * S_TILE - t0).astype(F32)
        logits = [_dot_nt(ckv_tile, qlat_ref[0, g * ATT_GROUP:(g + 1) * ATT_GROUP].reshape(gw, KV_LORA))
                  for g in groups]
        probs, alphas = [], []
        for g in groups:
            a = logits[g] + bias_ref[g] + drop
            shift = slope_rows[g] * off
            m_old = m_ref[g:g + 1, :]
            m_new = jnp.maximum(m_old, jnp.max(a, axis=0, keepdims=True) + shift)
            alpha = jnp.exp2(m_old - m_new)
            p = jnp.exp2(a - (m_new - shift))
            l_ref[g:g + 1, :] = alpha * l_ref[g:g + 1, :] + jnp.sum(p, axis=0, keepdims=True)
            m_ref[g:g + 1, :] = m_new
            probs.append(p.astype(BF16))
            alphas.append(alpha)
        for g in groups:
            acc_ref[g] = alphas[g] * acc_ref[g] + _dot(ckvt_tile, probs[g])
        return carry

    lax.fori_loop(0, n_tiles, att_tile, 0)

    for g in groups:
        o_lat_t = acc_ref[g] * (1.0 / l_ref[g:g + 1, :])
        for u in range(ATT_GROUP):
            h = g * ATT_GROUP + u
            o_lat = o_lat_t[:, u * Q_BLOCK:(u + 1) * Q_BLOCK].T.astype(BF16)
            o_ref[:, h * A_HEAD_DIM:(h + 1) * A_HEAD_DIM] = _dot(o_lat, wuv_ref[h]).astype(BF16)


def _dsa_attn(qlat, qidx, wt, kidx, ckv, ckvt, wuv, *, bsz, seq):
    nq = seq // Q_BLOCK
    ns = seq // S_TILE
    n_sel = min(TOPK_MAX, seq // 4)
    ng, gw = A_HEADS // ATT_GROUP, ATT_GROUP * Q_BLOCK
    return pl.pallas_call(
        functools.partial(_dsa_attn_body, n_sel=n_sel),
        out_shape=jax.ShapeDtypeStruct((bsz * seq, A_HEADS * A_HEAD_DIM), BF16),
        grid=(bsz, nq),
        in_specs=[
            pl.BlockSpec((1, IDX_HEADS, Q_BLOCK, IDX_DIM), lambda b, i: (b * nq + i, 0, 0, 0)),
            pl.BlockSpec((1, A_HEADS, Q_BLOCK, KV_LORA), lambda b, i: (b * nq + i, 0, 0, 0)),
            pl.BlockSpec((IDX_HEADS, Q_BLOCK), lambda b, i: (0, b * nq + i)),
            pl.BlockSpec((seq, IDX_DIM), lambda b, i: (b, 0)),
            pl.BlockSpec((seq, KV_LORA), lambda b, i: (b, 0)),
            pl.BlockSpec((ns, KV_LORA, S_TILE), lambda b, i: (b, 0, 0)),
            pl.BlockSpec((A_HEADS, KV_LORA, A_HEAD_DIM), lambda b, i: (0, 0, 0)),
        ],
        out_specs=pl.BlockSpec((Q_BLOCK, A_HEADS * A_HEAD_DIM), lambda b, i: (b * nq + i, 0)),
        scratch_shapes=[
            pltpu.VMEM((seq, Q_BLOCK), F32),
            pltpu.VMEM((ng, S_TILE, gw), F32),
            pltpu.VMEM((ng, gw), F32),
            pltpu.VMEM((ng, gw), F32),
            pltpu.VMEM((ng, KV_LORA, gw), F32),
        ],
        compiler_params=_params("parallel", "arbitrary"),
        name="dsa_attn",
    )(qidx, qlat, wt, kidx, ckv, ckvt, wuv)


def _gdn_prep_body(x_ref, halo_ref, small_ref, cw_ref, alog_ref, dtb_ref,
                   q_ref, k_ref, v_ref, gcol_ref, grow_ref, xx_ref):
    ts = x_ref.shape[0]
    first = pl.program_id(1) == 0
    halo = halo_ref[...]
    xx_ref[0:SUBLANES, :] = jnp.where(first, jnp.zeros_like(halo), halo)
    xx_ref[SUBLANES:, :] = x_ref[...]
    cw = cw_ref[...]
    y = jnp.zeros(x_ref.shape, F32)
    for tap in range(CONV_WIDTH):
        off = SUBLANES - (CONV_WIDTH - 1) + tap
        y = y + cw[tap:tap + 1, :] * xx_ref[off:off + ts, :]
    y = y * _sigmoid(y)

    for h in range(B_HEADS):
        lo, hi = h * B_HEAD_DIM, (h + 1) * B_HEAD_DIM
        qh = y[:, lo:hi]
        q_ref[:, lo:hi] = qh * lax.rsqrt(jnp.sum(qh * qh, axis=-1, keepdims=True) + EPS) * (
            B_HEAD_DIM ** -0.5)
        kh = y[:, B_WIDTH + lo:B_WIDTH + hi]
        k_ref[:, lo:hi] = kh * lax.rsqrt(jnp.sum(kh * kh, axis=-1, keepdims=True) + EPS)
    v_ref[...] = y[:, 2 * B_WIDTH:]

    small = small_ref[...]
    pre = small + dtb_ref[...]
    softplus = jnp.maximum(pre, 0.0) + jnp.log1p(jnp.exp(-jnp.abs(pre)))
    g = -jnp.exp(alog_ref[...]) * softplus
    row = lax.broadcasted_iota(jnp.int32, g.shape, 0) & (GDN_CHUNK - 1)
    shift = 1
    while shift < GDN_CHUNK:
        g = g + jnp.where(row >= shift, pltpu.roll(g, shift, 0), 0.0)
        shift *= 2
    lane = lax.broadcasted_iota(jnp.int32, g.shape, 1)
    is_a = (lane >= SMALL_A) & (lane < SMALL_A + B_HEADS)
    gb = jnp.where(is_a, g, _sigmoid(small))
    gcol_ref[...] = gb
    grow_ref[...] = gb.T


def _gdn_prep(proj, conv_w, alog_p, dtb_p, *, bsz, seq, ts=256):
    t = bsz * seq
    ns = seq // ts
    c = 3 * B_WIDTH
    return pl.pallas_call(
        _gdn_prep_body,
        out_shape=(
            jax.ShapeDtypeStruct((t, B_WIDTH), F32),
            jax.ShapeDtypeStruct((t, B_WIDTH), F32),
            jax.ShapeDtypeStruct((t, B_WIDTH), F32),
            jax.ShapeDtypeStruct((t, LANES), F32),
            jax.ShapeDtypeStruct((LANES, t), F32),
        ),
        grid=(bsz, ns),
        in_specs=[
            pl.BlockSpec((ts, c), lambda b, i: (b * ns + i, COL_QKV // c)),
            pl.BlockSpec((SUBLANES, c),
                         lambda b, i: (jnp.maximum((b * ns + i) * (ts // SUBLANES) - 1, 0), 0)),
            pl.BlockSpec((ts, LANES), lambda b, i: (b * ns + i, COL_SMALL // LANES)),
            pl.BlockSpec((CONV_WIDTH, c), lambda b, i: (0, 0)),
            pl.BlockSpec((1, LANES), lambda b, i: (0, 0)),
            pl.BlockSpec((1, LANES), lambda b, i: (0, 0)),
        ],
        out_specs=(
            pl.BlockSpec((ts, B_WIDTH), lambda b, i: (b * ns + i, 0)),
            pl.BlockSpec((ts, B_WIDTH), lambda b, i: (b * ns + i, 0)),
            pl.BlockSpec((ts, B_WIDTH), lambda b, i: (b * ns + i, 0)),
            pl.BlockSpec((ts, LANES), lambda b, i: (b * ns + i, 0)),
            pl.BlockSpec((LANES, ts), lambda b, i: (0, b * ns + i)),
        ),
        scratch_shapes=[pltpu.VMEM((ts + SUBLANES, c), F32)],
        compiler_params=_params("parallel", "arbitrary"),
        name="gdn_prep",
    )(proj, proj, proj, conv_w, alog_p, dtb_p)


def _split_bf16(a):
    hi = a.astype(BF16)
    lo = (a - hi.astype(F32)).astype(BF16)
    return hi, lo


def _dot3(a, b):
    ah, al = _split_bf16(a)
    bh, bl = _split_bf16(b)
    return _dot(ah, bh) + (_dot(ah, bl) + _dot(al, bh))


def _dot1(a, b):
    return _dot(a.astype(BF16), b.astype(BF16))


def _each(fn, *lists):
    return [fn(*args) for args in zip(*lists)]


def _unit_lower_inverses(mats, row, col):
    base_log2 = 4
    eye = jnp.where(row == col, 1.0, 0.0)
    in_block = (row >> base_log2) == (col >> base_log2)
    d = [jnp.where(in_block, a, 0.0) for a in mats]
    x = [eye - dh for dh in d]
    p = _each(_dot1, d, d)
    for step in range(base_log2 - 1):
        x = _each(lambda xh, th: xh + th, x, _each(_dot1, x, p))
        if step < base_log2 - 2:
            p = _each(_dot1, p, p)
    sh = base_log2
    while (1 << sh) < GDN_CHUNK:
        same_parent = (row >> (sh + 1)) == (col >> (sh + 1))
        same_block = (row >> sh) == (col >> sh)
        quad = [jnp.where(same_block, 0.0, jnp.where(same_parent, a, 0.0)) for a in mats]
        x = _each(lambda xh, th: xh - th, x, _each(_dot1, x, _each(_dot1, quad, x)))
        sh += 1
    return x


def _gdn_chunk_body(q_ref, k_ref, v_ref, z_ref, gcol_ref, grow_ref, ng_ref, o_ref, state_ref):
    @pl.when(pl.program_id(1) == 0)
    def _():
        state_ref[...] = jnp.zeros_like(state_ref)

    c = GDN_CHUNK
    heads = range(B_HEADS)
    span = lambda h: slice(h * B_HEAD_DIM, (h + 1) * B_HEAD_DIM)
    row = lax.broadcasted_iota(jnp.int32, (c, c), 0)
    col = lax.broadcasted_iota(jnp.int32, (c, c), 1)
    gcol = gcol_ref[...]
    grow = grow_ref[...]
    e_cum = jnp.exp(gcol)
    e_rest = jnp.exp(gcol[c - 1:c, :] - gcol)
    e_last = jnp.exp(grow[:, c - 1:c])

    q = [q_ref[:, span(h)] for h in heads]
    k = [k_ref[:, span(h)] for h in heads]
    v = [v_ref[:, span(h)] for h in heads]
    beta = [gcol[:, SMALL_B + h:SMALL_B + h + 1] for h in heads]
    decay = [jnp.exp(jnp.where(row >= col,
                               gcol[:, SMALL_A + h:SMALL_A + h + 1] - grow[SMALL_A + h:SMALL_A + h + 1, :],
                               -jnp.inf)) for h in heads]
    k_beta = _each(lambda kh, bh: kh * bh, k, beta)
    k16 = [kh.astype(BF16) for kh in k]
    kk = _each(_dot_nt, [kb.astype(BF16) for kb in k_beta], k16)
    qk = _each(_dot_nt, [qh.astype(BF16) for qh in q], k16)
    a_mat = _each(lambda m, dh: jnp.where(row > col, m * dh, 0.0), kk, decay)
    attn = _each(lambda m, dh: (m * dh).astype(BF16), qk, decay)
    t_inv = _unit_lower_inverses(a_mat, row, col)
    rhs = [jnp.concatenate([v[h] * beta[h], k_beta[h] * e_cum[:, SMALL_A + h:SMALL_A + h + 1]], axis=1)
           for h in heads]
    sol = _each(_dot3, t_inv, rhs)
    q_dec = [(q[h] * e_cum[:, SMALL_A + h:SMALL_A + h + 1]).astype(BF16) for h in heads]
    k_dec_t = [(k[h] * e_rest[:, SMALL_A + h:SMALL_A + h + 1]).T.astype(BF16) for h in heads]

    state = [state_ref[h] for h in heads]
    s16 = [s.astype(BF16) for s in state]
    v_new = [sol[h][:, :B_HEAD_DIM] - _dot(sol[h][:, B_HEAD_DIM:].astype(BF16), s16[h]) for h in heads]
    v16 = [x.astype(BF16) for x in v_new]
    for h in heads:
        state_ref[h] = state[h] * e_last[SMALL_A + h:SMALL_A + h + 1, :] + _dot(k_dec_t[h], v16[h])
    out = [_dot(q_dec[h], s16[h]) + _dot(attn[h], v16[h]) for h in heads]
    for h in heads:
        o = out[h]
        y = o * lax.rsqrt(jnp.mean(o * o, axis=-1, keepdims=True) + EPS) * ng_ref[...]
        zh = z_ref[:, span(h)]
        o_ref[:, span(h)] = (y * (zh * _sigmoid(zh))).astype(BF16)


def _gdn_chunk(q, k, v, proj, gcol, grow, ng, *, bsz, seq):
    c = GDN_CHUNK
    nc = seq // c
    tok = lambda b, i: (b * nc + i, 0)
    return pl.pallas_call(
        _gdn_chunk_body,
        out_shape=jax.ShapeDtypeStruct((bsz * seq, B_WIDTH), BF16),
        grid=(bsz, nc),
        in_specs=[
            pl.BlockSpec((c, B_WIDTH), tok),
            pl.BlockSpec((c, B_WIDTH), tok),
            pl.BlockSpec((c, B_WIDTH), tok),
            pl.BlockSpec((c, B_WIDTH), lambda b, i: (b * nc + i, COL_Z // B_WIDTH)),
            pl.BlockSpec((c, LANES), tok),
            pl.BlockSpec((LANES, c), lambda b, i: (0, b * nc + i)),
            pl.BlockSpec((1, B_HEAD_DIM), lambda b, i: (0, 0)),
        ],
        out_specs=pl.BlockSpec((c, B_WIDTH), tok),
        scratch_shapes=[pltpu.VMEM((B_HEADS, B_HEAD_DIM, B_HEAD_DIM), F32)],
        compiler_params=_params("parallel", "arbitrary"),
        name="gdn_chunk",
    )(q, k, v, proj, gcol, grow, ng)


def _out_ln_body(h_ref, oa_ref, ob_ref, wa_ref, wb_ref, g_ref, b_ref, o_ref, *, sub):
    spans = [slice(r, r + sub) for r in range(0, h_ref.shape[0], sub)]
    mixes = [_dot(oa_ref[s, :], wa_ref[...]) + _dot(ob_ref[s, :], wb_ref[...]) for s in spans]
    for s, mix in zip(spans, mixes):
        o_ref[s, :] = _layer_norm(DEEPNORM_ALPHA * h_ref[s, :] + mix, g_ref[...], b_ref[...])


def _out_ln(h, oa, ob, w, g, b, *, tm=512, sub=256):
    t, d = h.shape
    da, db = oa.shape[1], ob.shape[1]
    assert da == db and w.shape == (da + db, d)
    return pl.pallas_call(
        functools.partial(_out_ln_body, sub=sub),
        out_shape=jax.ShapeDtypeStruct((t, d), F32),
        grid=(t // tm,),
        in_specs=[
            pl.BlockSpec((tm, d), lambda i: (i, 0)),
            pl.BlockSpec((tm, da), lambda i: (i, 0)),
            pl.BlockSpec((tm, db), lambda i: (i, 0)),
            pl.BlockSpec((da, d), lambda i: (0, 0)),
            pl.BlockSpec((db, d), lambda i: (1, 0)),
            pl.BlockSpec((1, d), lambda i: (0, 0)),
            pl.BlockSpec((1, d), lambda i: (0, 0)),
        ],
        out_specs=pl.BlockSpec((tm, d), lambda i: (i, 0)),
        compiler_params=_params("parallel"),
        name="out_ln",
    )(h, oa, ob, w, w, g, b)


def _regroup_w_in(w_in):
    cuts = [0, Q_LORA, KV_LORA, IDX_DIM, IDX_HEADS, B_WIDTH, B_WIDTH, B_WIDTH, B_WIDTH, B_HEADS, B_HEADS]
    offs = [sum(cuts[:n + 1]) for n in range(len(cuts))]
    w16 = w_in.astype(BF16)
    part = lambda n: w16[:, offs[n]:offs[n + 1]]
    c_q, c_kv, k_idx, w_idx, q_b, k_b, v_b, z_b, a_b, b_b = [part(n) for n in range(10)]
    pad = jnp.zeros((w_in.shape[0], LANES - IDX_HEADS - 2 * B_HEADS), BF16)
    return jnp.concatenate([q_b, k_b, v_b, z_b, c_q, c_kv, k_idx, w_idx, a_b, b_b, pad], axis=1)


def _lane_pad(vec, offset):
    out = jnp.zeros((1, LANES), F32)
    return out.at[0, offset:offset + vec.shape[0]].set(vec.astype(F32))


def kernel(x, ffn1_w_gate, ffn1_w_up, ffn1_w_down, ln1_g, ln1_b, w_in, q_norm_g, kv_norm_g, w_uq, w_uk, w_uv, w_q_idx, k_idx_ln_g, k_idx_ln_b, conv_w, a_log, dt_bias, o_norm_g, w_out, ln2_g, ln2_b, ffn2_w_gate, ffn2_w_up, ffn2_w_down, ln3_g, ln3_b):
    bsz, seq, d = x.shape
    assert seq % S_TILE == 0 and seq % GDN_CHUNK == 0 and x.dtype == F32
    t = bsz * seq
    row = lambda p: p.reshape(1, -1).astype(F32)
    h = x.reshape(t, d)
    for l in range(DEPTH):
        h = _ffn_ln(h, ffn1_w_gate[l].astype(BF16), ffn1_w_up[l].astype(BF16),
                    (0.5 * ffn1_w_down[l]).astype(BF16), row(ln1_g[l]), row(ln1_b[l]))
        proj = _in_proj(h, _regroup_w_in(w_in[l]))
        qlat, qidx, ckv, ckvt, kidx, wt = _dsa_prep(
            proj, row(q_norm_g[l]), row(kv_norm_g[l]),
            w_uq[l].reshape(Q_LORA, A_HEADS * A_HEAD_DIM).astype(BF16), w_uk[l].astype(BF16),
            w_q_idx[l].astype(BF16), row(k_idx_ln_g[l]), row(k_idx_ln_b[l]))
        o_a = _dsa_attn(qlat, qidx, wt, kidx, ckv, ckvt, w_uv[l].astype(BF16), bsz=bsz, seq=seq)
        gq, gk, gv, gcol, grow = _gdn_prep(proj, conv_w[l].astype(F32), _lane_pad(a_log[l], SMALL_A),
                                           _lane_pad(dt_bias[l], SMALL_A), bsz=bsz, seq=seq)
        o_b = _gdn_chunk(gq, gk, gv, proj, gcol, grow, row(o_norm_g[l]), bsz=bsz, seq=seq)
        h = _out_ln(h, o_a, o_b, w_out[l].astype(BF16), row(ln2_g[l]), row(ln2_b[l]))
        h = _ffn_ln(h, ffn2_w_gate[l].astype(BF16), ffn2_w_up[l].astype(BF16),
                    (0.5 * ffn2_w_down[l]).astype(BF16), row(ln3_g[l]), row(ln3_b[l]))
    return h.reshape(bsz, seq, d)
```

```python
import functools
import math

import jax
import jax.numpy as jnp
from jax import lax
from jax.experimental import pallas as pl
from jax.experimental.pallas import tpu as pltpu

F32 = jnp.float32
BF16 = jnp.bfloat16

A_HEADS = 8
A_HEAD_DIM = 128
Q_LORA = 512
KV_LORA = 256
IDX_HEADS = 16
IDX_DIM = 128
TOPK_MAX = 256
B_HEADS = 8
B_HEAD_DIM = 128
B_WIDTH = B_HEADS * B_HEAD_DIM
CONV_WIDTH = 4
DEPTH = 1
DEEPNORM_ALPHA = (2.0 * DEPTH) ** 0.25
EPS = 1e-6

LANES = 128
SUBLANES = 8
VMEM_LIMIT_BYTES = 60 * 1024 * 1024

COL_QKV = 0
COL_Z = 3 * B_WIDTH
COL_CQ = COL_Z + B_WIDTH
COL_CKV = COL_CQ + Q_LORA
COL_KIDX = COL_CKV + KV_LORA
COL_SMALL = COL_KIDX + IDX_DIM
D_IN_PAD = COL_SMALL + LANES
SMALL_W = 0
SMALL_A = IDX_HEADS
SMALL_B = IDX_HEADS + B_HEADS

Q_BLOCK = 128
S_TILE = 512
IDX_GROUP = 2
ATT_GROUP = 2
GDN_CHUNK = 128
INT_MIN = -2 ** 31
F32_LOWEST = -3.4028234663852886e38
LOG2E = math.log2(math.e)


def _dot(a, b):
    return jnp.dot(a, b, preferred_element_type=F32)


def _dot_nt(a, b):
    return lax.dot_general(a, b, (((1,), (1,)), ((), ())), preferred_element_type=F32)


def _sigmoid(x):
    return 1.0 / (1.0 + jnp.exp(-x))


def _layer_norm(y, g, b):
    mu = jnp.mean(y, axis=-1, keepdims=True)
    yc = y - mu
    var = jnp.mean(yc * yc, axis=-1, keepdims=True)
    return yc * lax.rsqrt(var + EPS) * g + b


def _params(*sem):
    return pltpu.CompilerParams(dimension_semantics=sem, vmem_limit_bytes=VMEM_LIMIT_BYTES)


def _ffn_ln_body(x_ref, wg_ref, wu_ref, wd_ref, g_ref, b_ref, o_ref, xb_ref, acc_ref):
    j = pl.program_id(1)

    @pl.when(j == 0)
    def _():
        xb_ref[...] = x_ref[...].astype(BF16)
        acc_ref[...] = jnp.zeros_like(acc_ref)

    xb = xb_ref[...]
    gate = _dot(xb, wg_ref[...].astype(BF16))
    up = _dot(xb, wu_ref[...].astype(BF16))
    act = gate * _sigmoid(gate) * up
    acc_ref[...] += _dot(act.astype(BF16), wd_ref[...].astype(BF16))

    @pl.when(j == pl.num_programs(1) - 1)
    def _():
        y = DEEPNORM_ALPHA * x_ref[...] + 0.5 * acc_ref[...]
        o_ref[...] = _layer_norm(y, g_ref[...], b_ref[...])


def _ffn_ln(x, wg, wu, wd, g, b, *, tm=1024, tf=256):
    t, d = x.shape
    f = wg.shape[1]
    return pl.pallas_call(
        _ffn_ln_body,
        out_shape=jax.ShapeDtypeStruct((t, d), F32),
        grid=(t // tm, f // tf),
        in_specs=[
            pl.BlockSpec((tm, d), lambda i, j: (i, 0), pipeline_mode=pl.Buffered(1)),
            pl.BlockSpec((d, tf), lambda i, j: (0, j)),
            pl.BlockSpec((d, tf), lambda i, j: (0, j)),
            pl.BlockSpec((tf, d), lambda i, j: (j, 0)),
            pl.BlockSpec((1, d), lambda i, j: (0, 0)),
            pl.BlockSpec((1, d), lambda i, j: (0, 0)),
        ],
        out_specs=pl.BlockSpec((tm, d), lambda i, j: (i, 0)),
        scratch_shapes=[pltpu.VMEM((tm, d), BF16), pltpu.VMEM((tm, d), F32)],
        compiler_params=_params("parallel", "arbitrary"),
        name="ffn_ln",
    )(x, wg, wu, wd, g, b)


def _in_proj_body(x_ref, *refs, tn):
    w_refs, o_ref = refs[:-1], refs[-1]
    xb = x_ref[...].astype(BF16)
    col = 0
    for w_ref in w_refs:
        for c in range(0, w_ref.shape[1], tn):
            o_ref[:, col + c:col + c + tn] = _dot(xb, w_ref[:, c:c + tn])
        col += w_ref.shape[1]


def _in_proj(x, weights, *, tm=512, tn=1024):
    t, d = x.shape
    n = sum(w.shape[1] for w in weights)
    return pl.pallas_call(
        functools.partial(_in_proj_body, tn=tn),
        out_shape=jax.ShapeDtypeStruct((t, n), F32),
        grid=(t // tm,),
        in_specs=[pl.BlockSpec((tm, d), lambda i: (i, 0))] + [
            pl.BlockSpec(w.shape, lambda i: (0, 0), pipeline_mode=pl.Buffered(1)) for w in weights],
        out_specs=pl.BlockSpec((tm, n), lambda i: (i, 0)),
        compiler_params=_params("parallel"),
        name="in_proj",
    )(x, *weights)


def _dsa_prep_body(cq_ref, ckv_ref, kidx_ref, small_ref, qg_ref, kvg_ref, wuq_ref, wuk_ref,
                   wqi_ref, lng_ref, lnb_ref,
                   qlat_ref, qidx_ref, ckv_o_ref, ckvt_o_ref, kidx_o_ref, wt_o_ref):
    tm = cq_ref.shape[0]
    nb = tm // Q_BLOCK

    cq = cq_ref[...]
    cq = cq * lax.rsqrt(jnp.mean(cq * cq, axis=-1, keepdims=True) + EPS) * qg_ref[...]
    cqb = cq.astype(BF16)

    q = _dot(cqb, wuq_ref[...]).astype(BF16)
    for h in range(A_HEADS):
        ql = _dot(q[:, h * A_HEAD_DIM:(h + 1) * A_HEAD_DIM], wuk_ref[h])
        ql = ql * (A_HEAD_DIM ** -0.5 * LOG2E)
        qlat_ref[:, h] = ql.astype(BF16).reshape(nb, Q_BLOCK, KV_LORA)

    qi = _dot(cqb, wqi_ref[...]).astype(BF16)
    for h in range(IDX_HEADS):
        qidx_ref[:, h] = qi[:, h * IDX_DIM:(h + 1) * IDX_DIM].reshape(nb, Q_BLOCK, IDX_DIM)

    ckv = ckv_ref[...]
    ckv = ckv * lax.rsqrt(jnp.mean(ckv * ckv, axis=-1, keepdims=True) + EPS) * kvg_ref[...]
    ckv_o_ref[...] = ckv.astype(BF16)
    ckvt_o_ref[0] = ckv.T.astype(BF16)

    kidx_o_ref[...] = _layer_norm(kidx_ref[...], lng_ref[...], lnb_ref[...]).astype(BF16)

    wt = small_ref[...].T
    wt_o_ref[...] = wt[SMALL_W:SMALL_W + IDX_HEADS, :] * (IDX_HEADS ** -0.5 * IDX_DIM ** -0.5)


def _dsa_prep(proj, qg, kvg, wuq, wuk, wqi, lng, lnb, *, tm=S_TILE):
    t = proj.shape[0]
    nq = t // Q_BLOCK
    full = lambda *shape: pl.BlockSpec(shape, lambda i: (0,) * len(shape))
    return pl.pallas_call(
        _dsa_prep_body,
        out_shape=(
            jax.ShapeDtypeStruct((nq, A_HEADS, Q_BLOCK, KV_LORA), BF16),
            jax.ShapeDtypeStruct((nq, IDX_HEADS, Q_BLOCK, IDX_DIM), BF16),
            jax.ShapeDtypeStruct((t, KV_LORA), BF16),
            jax.ShapeDtypeStruct((t // tm, KV_LORA, tm), BF16),
            jax.ShapeDtypeStruct((t, IDX_DIM), BF16),
            jax.ShapeDtypeStruct((IDX_HEADS, t), F32),
        ),
        grid=(t // tm,),
        in_specs=[
            pl.BlockSpec((tm, Q_LORA), lambda i: (i, COL_CQ // Q_LORA)),
            pl.BlockSpec((tm, KV_LORA), lambda i: (i, COL_CKV // KV_LORA)),
            pl.BlockSpec((tm, IDX_DIM), lambda i: (i, COL_KIDX // IDX_DIM)),
            pl.BlockSpec((tm, LANES), lambda i: (i, COL_SMALL // LANES)),
            full(1, Q_LORA), full(1, KV_LORA),
            full(Q_LORA, A_HEADS * A_HEAD_DIM),
            full(A_HEADS, A_HEAD_DIM, KV_LORA),
            full(Q_LORA, IDX_HEADS * IDX_DIM),
            full(1, IDX_DIM), full(1, IDX_DIM),
        ],
        out_specs=(
            pl.BlockSpec((tm // Q_BLOCK, A_HEADS, Q_BLOCK, KV_LORA), lambda i: (i, 0, 0, 0)),
            pl.BlockSpec((tm // Q_BLOCK, IDX_HEADS, Q_BLOCK, IDX_DIM), lambda i: (i, 0, 0, 0)),
            pl.BlockSpec((tm, KV_LORA), lambda i: (i, 0)),
            pl.BlockSpec((1, KV_LORA, tm), lambda i: (i, 0, 0)),
            pl.BlockSpec((tm, IDX_DIM), lambda i: (i, 0)),
            pl.BlockSpec((IDX_HEADS, tm), lambda i: (0, i)),
        ),
        compiler_params=_params("parallel"),
        name="dsa_prep",
    )(proj, proj, proj, proj, qg, kvg, wuq, wuk, wqi, lng, lnb)


def _key_to_f32(key):
    return pltpu.bitcast(key ^ ((key >> 31) & 0x7FFFFFFF), F32)


def _dsa_attn_body(qidx_ref, qlat_ref, wt_ref, kidx_ref, ckv_ref, ckvt_ref, wuv_ref, o_ref,
                   sc_ref, bias_ref, m_ref, l_ref, acc_ref, *, n_sel):
    i = pl.program_id(1)
    t0 = i * Q_BLOCK
    n_tiles = (t0 + Q_BLOCK + S_TILE - 1) // S_TILE

    t_lane = t0 + lax.broadcasted_iota(jnp.int32, (S_TILE, Q_BLOCK), 1)
    s_row = lax.broadcasted_iota(jnp.int32, (S_TILE, Q_BLOCK), 0)

    def tile_rows(j):
        return pl.ds(pl.multiple_of(j * S_TILE, S_TILE), S_TILE)

    wt = wt_ref[...]

    def score_tile(j, carry):
        k_tile = kidx_ref[tile_rows(j), :]
        sc = jnp.zeros((S_TILE, Q_BLOCK), F32)
        for g in range(IDX_HEADS // IDX_GROUP):
            qg = qidx_ref[0, g * IDX_GROUP:(g + 1) * IDX_GROUP].reshape(IDX_GROUP * Q_BLOCK, IDX_DIM)
            logits = _dot_nt(k_tile, qg)
            for u in range(IDX_GROUP):
                h = g * IDX_GROUP + u
                sc = sc + wt[h:h + 1, :] * jnp.maximum(logits[:, u * Q_BLOCK:(u + 1) * Q_BLOCK], 0.0)
        sc_ref[tile_rows(j), :] = jnp.where(s_row + j * S_TILE <= t_lane, sc, -jnp.inf)
        return carry

    lax.fori_loop(0, n_tiles, score_tile, 0)

    def count(pred):
        def body(j, acc):
            hit = jnp.where(pred(sc_ref[tile_rows(j), :]), 1, 0)
            return acc + jnp.sum(hit.reshape(S_TILE // SUBLANES, SUBLANES, Q_BLOCK), axis=0)

        acc = lax.fori_loop(0, n_tiles, body, jnp.zeros((SUBLANES, Q_BLOCK), jnp.int32))
        return jnp.sum(acc, axis=0, keepdims=True)

    def try_key(cand_key, key, n_ge):
        cand = _key_to_f32(cand_key)
        n = count(lambda sc: sc >= cand)
        ok = n >= n_sel
        return jnp.where(ok, cand_key, key), jnp.where(ok, n, n_ge)

    zero = jnp.zeros((1, Q_BLOCK), jnp.int32)
    key, n_ge = try_key(zero, zero + INT_MIN, zero)

    def bit_step(b, carry):
        return try_key(carry[0] | (1 << (30 - b)), *carry)

    key, n_ge = lax.fori_loop(0, 31, bit_step, (key, n_ge))
    thr = _key_to_f32(key)
    thr = jnp.maximum(jnp.where(thr != thr, F32_LOWEST, thr), F32_LOWEST)

    @pl.when(jnp.max(n_ge) > n_sel)
    def _():
        need = (n_sel - count(lambda sc: sc > thr)).astype(F32)
        r = lax.broadcasted_iota(jnp.int32, (S_TILE, S_TILE), 0)
        c = lax.broadcasted_iota(jnp.int32, (S_TILE, S_TILE), 1)
        lower_incl = jnp.where(c <= r, 1.0, 0.0).astype(BF16)

        def body(j, run):
            sc = sc_ref[tile_rows(j), :]
            tied = sc == thr
            rank = run + _dot(lower_incl, jnp.where(tied, 1.0, 0.0).astype(BF16))
            sc_ref[tile_rows(j), :] = jnp.where(tied & (rank > need), -jnp.inf, sc)
            return rank[S_TILE - 1:S_TILE, :]

        lax.fori_loop(0, n_tiles, body, jnp.zeros((1, Q_BLOCK), F32))

    gw = ATT_GROUP * Q_BLOCK
    groups = range(A_HEADS // ATT_GROUP)
    lane = lax.broadcasted_iota(jnp.int32, (1, gw), 1)
    slope_rows = []
    for g in groups:
        row_g = jnp.zeros((1, gw), F32)
        for u in range(ATT_GROUP):
            slope = 2.0 ** (-8.0 * (g * ATT_GROUP + u + 1) / A_HEADS) * LOG2E
            row_g = jnp.where(lane >= u * Q_BLOCK, slope, row_g)
        slope_rows.append(row_g)

    @pl.when(i == 0)
    def _():
        s_rel = lax.broadcasted_iota(jnp.int32, (S_TILE, gw), 0).astype(F32)
        for g in groups:
            bias_ref[g] = slope_rows[g] * s_rel

    m_ref[...] = jnp.full(m_ref.shape, -1e30, F32)
    l_ref[...] = jnp.zeros_like(l_ref)
    acc_ref[...] = jnp.zeros_like(acc_ref)

    def att_tile(j, carry):
        drop = jnp.where(sc_ref[tile_rows(j), :] >= thr, 0.0, -jnp.inf)
        drop = jnp.concatenate([drop] * ATT_GROUP, axis=1)
        ckv_tile = ckv_ref[tile_rows(j), :]
        ckvt_tile = ckvt_ref[j]
        off = (j * S_TILE - t0).astype(F32)
        logits = [_dot_nt(ckv_tile, qlat_ref[0, g * ATT_GROUP:(g + 1) * ATT_GROUP].reshape(gw, KV_LORA))
                  for g in groups]
        probs, alphas = [], []
        for g in groups:
            a = logits[g] + bias_ref[g] + drop
            shift = slope_rows[g] * off
            m_old = m_ref[g:g + 1, :]
            m_new = jnp.maximum(m_old, jnp.max(a, axis=0, keepdims=True) + shift)
            alpha = jnp.exp2(m_old - m_new)
            p = jnp.exp2(a - (m_new - shift))
            l_ref[g:g + 1, :] = alpha * l_ref[g:g + 1, :] + jnp.sum(p, axis=0, keepdims=True)
            m_ref[g:g + 1, :] = m_new
            probs.append(p.astype(BF16))
            alphas.append(alpha)
        for g in groups:
            acc_ref[g] = alphas[g] * acc_ref[g] + _dot(ckvt_tile, probs[g])
        return carry

    lax.fori_loop(0, n_tiles, att_tile, 0)

    for g in groups:
        o_lat_t = acc_ref[g] * (1.0 / l_ref[g:g + 1, :])
        for u in range(ATT_GROUP):
            h = g * ATT_GROUP + u
            o_lat = o_lat_t[:, u * Q_BLOCK:(u + 1) * Q_BLOCK].T.astype(BF16)
            o_ref[:, h * A_HEAD_DIM:(h + 1) * A_HEAD_DIM] = _dot(o_lat, wuv_ref[h]).astype(BF16)


def _dsa_attn(qlat, qidx, wt, kidx, ckv, ckvt, wuv, *, bsz, seq):
    nq = seq // Q_BLOCK
    ns = seq // S_TILE
    n_sel = min(TOPK_MAX, seq // 4)
    ng, gw = A_HEADS // ATT_GROUP, ATT_GROUP * Q_BLOCK
    return pl.pallas_call(
        functools.partial(_dsa_attn_body, n_sel=n_sel),
        out_shape=jax.ShapeDtypeStruct((bsz * seq, A_HEADS * A_HEAD_DIM), BF16),
        grid=(bsz, nq),
        in_specs=[
            pl.BlockSpec((1, IDX_HEADS, Q_BLOCK, IDX_DIM), lambda b, i: (b * nq + i, 0, 0, 0)),
            pl.BlockSpec((1, A_HEADS, Q_BLOCK, KV_LORA), lambda b, i: (b * nq + i, 0, 0, 0)),
            pl.BlockSpec((IDX_HEADS, Q_BLOCK), lambda b, i: (0, b * nq + i)),
            pl.BlockSpec((seq, IDX_DIM), lambda b, i: (b, 0)),
            pl.BlockSpec((seq, KV_LORA), lambda b, i: (b, 0)),
            pl.BlockSpec((ns, KV_LORA, S_TILE), lambda b, i: (b, 0, 0)),
            pl.BlockSpec((A_HEADS, KV_LORA, A_HEAD_DIM), lambda b, i: (0, 0, 0)),
        ],
        out_specs=pl.BlockSpec((Q_BLOCK, A_HEADS * A_HEAD_DIM), lambda b, i: (b * nq + i, 0)),
        scratch_shapes=[
            pltpu.VMEM((seq, Q_BLOCK), F32),
            pltpu.VMEM((ng, S_TILE, gw), F32),
            pltpu.VMEM((ng, gw), F32),
            pltpu.VMEM((ng, gw), F32),
            pltpu.VMEM((ng, KV_LORA, gw), F32),
        ],
        compiler_params=_params("parallel", "arbitrary"),
        name="dsa_attn",
    )(qidx, qlat, wt, kidx, ckv, ckvt, wuv)


def _gdn_prep_body(x_ref, halo_ref, small_ref, cw_ref, alog_ref, dtb_ref,
                   q_ref, k_ref, v_ref, gcol_ref, grow_ref, xx_ref):
    ts = x_ref.shape[0]
    first = pl.program_id(1) == 0
    halo = halo_ref[...]
    xx_ref[0:SUBLANES, :] = jnp.where(first, jnp.zeros_like(halo), halo)
    xx_ref[SUBLANES:, :] = x_ref[...]
    cw = cw_ref[...]
    y = jnp.zeros(x_ref.shape, F32)
    for tap in range(CONV_WIDTH):
        off = SUBLANES - (CONV_WIDTH - 1) + tap
        y = y + cw[tap:tap + 1, :] * xx_ref[off:off + ts, :]
    y = y * _sigmoid(y)

    for h in range(B_HEADS):
        lo, hi = h * B_HEAD_DIM, (h + 1) * B_HEAD_DIM
        qh = y[:, lo:hi]
        q_ref[:, lo:hi] = qh * lax.rsqrt(jnp.sum(qh * qh, axis=-1, keepdims=True) + EPS) * (
            B_HEAD_DIM ** -0.5)
        kh = y[:, B_WIDTH + lo:B_WIDTH + hi]
        k_ref[:, lo:hi] = kh * lax.rsqrt(jnp.sum(kh * kh, axis=-1, keepdims=True) + EPS)
    v_ref[...] = y[:, 2 * B_WIDTH:]

    small = small_ref[...]
    pre = small + dtb_ref[...]
    softplus = jnp.maximum(pre, 0.0) + jnp.log1p(jnp.exp(-jnp.abs(pre)))
    g = -jnp.exp(alog_ref[...]) * softplus
    row = lax.broadcasted_iota(jnp.int32, g.shape, 0) & (GDN_CHUNK - 1)
    shift = 1
    while shift < GDN_CHUNK:
        g = g + jnp.where(row >= shift, pltpu.roll(g, shift, 0), 0.0)
        shift *= 2
    lane = lax.broadcasted_iota(jnp.int32, g.shape, 1)
    is_a = (lane >= SMALL_A) & (lane < SMALL_A + B_HEADS)
    gb = jnp.where(is_a, g, _sigmoid(small))
    gcol_ref[...] = gb
    grow_ref[...] = gb.T


def _gdn_prep(proj, conv_w, alog_p, dtb_p, *, bsz, seq, ts=256):
    t = bsz * seq
    ns = seq // ts
    c = 3 * B_WIDTH
    return pl.pallas_call(
        _gdn_prep_body,
        out_shape=(
            jax.ShapeDtypeStruct((t, B_WIDTH), F32),
            jax.ShapeDtypeStruct((t, B_WIDTH), F32),
            jax.ShapeDtypeStruct((t, B_WIDTH), F32),
            jax.ShapeDtypeStruct((t, LANES), F32),
            jax.ShapeDtypeStruct((LANES, t), F32),
        ),
        grid=(bsz, ns),
        in_specs=[
            pl.BlockSpec((ts, c), lambda b, i: (b * ns + i, COL_QKV // c)),
            pl.BlockSpec((SUBLANES, c),
                         lambda b, i: (jnp.maximum((b * ns + i) * (ts // SUBLANES) - 1, 0), 0)),
            pl.BlockSpec((ts, LANES), lambda b, i: (b * ns + i, COL_SMALL // LANES)),
            pl.BlockSpec((CONV_WIDTH, c), lambda b, i: (0, 0)),
            pl.BlockSpec((1, LANES), lambda b, i: (0, 0)),
            pl.BlockSpec((1, LANES), lambda b, i: (0, 0)),
        ],
        out_specs=(
            pl.BlockSpec((ts, B_WIDTH), lambda b, i: (b * ns + i, 0)),
            pl.BlockSpec((ts, B_WIDTH), lambda b, i: (b * ns + i, 0)),
            pl.BlockSpec((ts, B_WIDTH), lambda b, i: (b * ns + i, 0)),
            pl.BlockSpec((ts, LANES), lambda b, i: (b * ns + i, 0)),
            pl.BlockSpec((LANES, ts), lambda b, i: (0, b * ns + i)),
        ),
        scratch_shapes=[pltpu.VMEM((ts + SUBLANES, c), F32)],
        compiler_params=_params("parallel", "arbitrary"),
        name="gdn_prep",
    )(proj, proj, proj, conv_w, alog_p, dtb_p)


def _split_bf16(a):
    hi = a.astype(BF16)
    lo = (a - hi.astype(F32)).astype(BF16)
    return hi, lo


def _dot3(a, b):
    ah, al = _split_bf16(a)
    bh, bl = _split_bf16(b)
    return _dot(ah, bh) + (_dot(ah, bl) + _dot(al, bh))


def _dot1(a, b):
    return _dot(a.astype(BF16), b.astype(BF16))


def _each(fn, *lists):
    return [fn(*args) for args in zip(*lists)]


def _unit_lower_inverses(mats, row, col):
    base_log2 = 4
    eye = jnp.where(row == col, 1.0, 0.0)
    in_block = (row >> base_log2) == (col >> base_log2)
    d = [jnp.where(in_block, a, 0.0) for a in mats]
    x = [eye - dh for dh in d]
    p = _each(_dot1, d, d)
    for step in range(base_log2 - 1):
        x = _each(lambda xh, th: xh + th, x, _each(_dot1, x, p))
        if step < base_log2 - 2:
            p = _each(_dot1, p, p)
    sh = base_log2
    while (1 << sh) < GDN_CHUNK:
        same_parent = (row >> (sh + 1)) == (col >> (sh + 1))
        same_block = (row >> sh) == (col >> sh)
        quad = [jnp.where(same_block, 0.0, jnp.where(same_parent, a, 0.0)) for a in mats]
        x = _each(lambda xh, th: xh - th, x, _each(_dot1, x, _each(_dot1, quad, x)))
        sh += 1
    return x


def _gdn_chunk_body(q_ref, k_ref, v_ref, z_ref, gcol_ref, grow_ref, ng_ref, o_ref, state_ref):
    @pl.when(pl.program_id(1) == 0)
    def _():
        state_ref[...] = jnp.zeros_like(state_ref)

    c = GDN_CHUNK
    heads = range(B_HEADS)
    span = lambda h: slice(h * B_HEAD_DIM, (h + 1) * B_HEAD_DIM)
    row = lax.broadcasted_iota(jnp.int32, (c, c), 0)
    col = lax.broadcasted_iota(jnp.int32, (c, c), 1)
    gcol = gcol_ref[...]
    grow = grow_ref[...]
    e_cum = jnp.exp(gcol)
    e_rest = jnp.exp(gcol[c - 1:c, :] - gcol)
    e_last = jnp.exp(grow[:, c - 1:c])

    q = [q_ref[:, span(h)] for h in heads]
    k = [k_ref[:, span(h)] for h in heads]
    v = [v_ref[:, span(h)] for h in heads]
    beta = [gcol[:, SMALL_B + h:SMALL_B + h + 1] for h in heads]
    decay = [jnp.exp(jnp.where(row >= col,
                               gcol[:, SMALL_A + h:SMALL_A + h + 1] - grow[SMALL_A + h:SMALL_A + h + 1, :],
                               -jnp.inf)) for h in heads]
    k_beta = _each(lambda kh, bh: kh * bh, k, beta)
    k16 = [kh.astype(BF16) for kh in k]
    kk = _each(_dot_nt, [kb.astype(BF16) for kb in k_beta], k16)
    qk = _each(_dot_nt, [qh.astype(BF16) for qh in q], k16)
    a_mat = _each(lambda m, dh: jnp.where(row > col, m * dh, 0.0), kk, decay)
    attn = _each(lambda m, dh: (m * dh).astype(BF16), qk, decay)
    t_inv = _unit_lower_inverses(a_mat, row, col)
    rhs = [jnp.concatenate([v[h] * beta[h], k_beta[h] * e_cum[:, SMALL_A + h:SMALL_A + h + 1]], axis=1)
           for h in heads]
    sol = _each(_dot3, t_inv, rhs)
    q_dec = [(q[h] * e_cum[:, SMALL_A + h:SMALL_A + h + 1]).astype(BF16) for h in heads]
    k_dec_t = [(k[h] * e_rest[:, SMALL_A + h:SMALL_A + h + 1]).T.astype(BF16) for h in heads]

    state = [state_ref[h] for h in heads]
    s16 = [s.astype(BF16) for s in state]
    v_new = [sol[h][:, :B_HEAD_DIM] - _dot(sol[h][:, B_HEAD_DIM:].astype(BF16), s16[h]) for h in heads]
    v16 = [x.astype(BF16) for x in v_new]
    for h in heads:
        state_ref[h] = state[h] * e_last[SMALL_A + h:SMALL_A + h + 1, :] + _dot(k_dec_t[h], v16[h])
    out = [_dot(q_dec[h], s16[h]) + _dot(attn[h], v16[h]) for h in heads]
    for h in heads:
        o = out[h]
        y = o * lax.rsqrt(jnp.mean(o * o, axis=-1, keepdims=True) + EPS) * ng_ref[...]
        zh = z_ref[:, span(h)]
        o_ref[:, span(h)] = (y * (zh * _sigmoid(zh))).astype(BF16)


def _gdn_chunk(q, k, v, proj, gcol, grow, ng, *, bsz, seq):
    c = GDN_CHUNK
    nc = seq // c
    tok = lambda b, i: (b * nc + i, 0)
    return pl.pallas_call(
        _gdn_chunk_body,
        out_shape=jax.ShapeDtypeStruct((bsz * seq, B_WIDTH), BF16),
        grid=(bsz, nc),
        in_specs=[
            pl.BlockSpec((c, B_WIDTH), tok),
            pl.BlockSpec((c, B_WIDTH), tok),
            pl.BlockSpec((c, B_WIDTH), tok),
            pl.BlockSpec((c, B_WIDTH), lambda b, i: (b * nc + i, COL_Z // B_WIDTH)),
            pl.BlockSpec((c, LANES), tok),
            pl.BlockSpec((LANES, c), lambda b, i: (0, b * nc + i)),
            pl.BlockSpec((1, B_HEAD_DIM), lambda b, i: (0, 0)),
        ],
        out_specs=pl.BlockSpec((c, B_WIDTH), tok),
        scratch_shapes=[pltpu.VMEM((B_HEADS, B_HEAD_DIM, B_HEAD_DIM), F32)],
        compiler_params=_params("parallel", "arbitrary"),
        name="gdn_chunk",
    )(q, k, v, proj, gcol, grow, ng)


def _out_ln_body(h_ref, oa_ref, ob_ref, wa_ref, wb_ref, g_ref, b_ref, o_ref, *, sub):
    spans = [slice(r, r + sub) for r in range(0, h_ref.shape[0], sub)]
    mixes = [_dot(oa_ref[s, :], wa_ref[...]) + _dot(ob_ref[s, :], wb_ref[...]) for s in spans]
    for s, mix in zip(spans, mixes):
        o_ref[s, :] = _layer_norm(DEEPNORM_ALPHA * h_ref[s, :] + mix, g_ref[...], b_ref[...])


def _out_ln(h, oa, ob, w, g, b, *, tm=512, sub=256):
    t, d = h.shape
    da, db = oa.shape[1], ob.shape[1]
    assert da == db and w.shape == (da + db, d)
    return pl.pallas_call(
        functools.partial(_out_ln_body, sub=sub),
        out_shape=jax.ShapeDtypeStruct((t, d), F32),
        grid=(t // tm,),
        in_specs=[
            pl.BlockSpec((tm, d), lambda i: (i, 0)),
            pl.BlockSpec((tm, da), lambda i: (i, 0)),
            pl.BlockSpec((tm, db), lambda i: (i, 0)),
            pl.BlockSpec((da, d), lambda i: (0, 0)),
            pl.BlockSpec((db, d), lambda i: (1, 0)),
            pl.BlockSpec((1, d), lambda i: (0, 0)),
            pl.BlockSpec((1, d), lambda i: (0, 0)),
        ],
        out_specs=pl.BlockSpec((tm, d), lambda i: (i, 0)),
        compiler_params=_params("parallel"),
        name="out_ln",
    )(h, oa, ob, w, w, g, b)


def _regroup_w_in(w_in):
    cuts = [0, Q_LORA, KV_LORA, IDX_DIM, IDX_HEADS, B_WIDTH, B_WIDTH, B_WIDTH, B_WIDTH, B_HEADS, B_HEADS]
    offs = [sum(cuts[:n + 1]) for n in range(len(cuts))]
    w16 = w_in.astype(BF16)
    part = lambda lo, hi: w16[:, offs[lo]:offs[hi]]
    pad = jnp.zeros((w_in.shape[0], LANES - IDX_HEADS - 2 * B_HEADS), BF16)
    small = jnp.concatenate([part(0, 3), part(3, 4), part(8, 10), pad], axis=1)
    return part(4, 8), small


def _lane_pad(vec, offset):
    out = jnp.zeros((1, LANES), F32)
    return out.at[0, offset:offset + vec.shape[0]].set(vec.astype(F32))


def kernel(x, ffn1_w_gate, ffn1_w_up, ffn1_w_down, ln1_g, ln1_b, w_in, q_norm_g, kv_norm_g, w_uq, w_uk, w_uv, w_q_idx, k_idx_ln_g, k_idx_ln_b, conv_w, a_log, dt_bias, o_norm_g, w_out, ln2_g, ln2_b, ffn2_w_gate, ffn2_w_up, ffn2_w_down, ln3_g, ln3_b):
    bsz, seq, d = x.shape
    assert seq % S_TILE == 0 and seq % GDN_CHUNK == 0 and x.dtype == F32
    t = bsz * seq
    row = lambda p: p.reshape(1, -1).astype(F32)
    h = x.reshape(t, d)
    for l in range(DEPTH):
        h = _ffn_ln(h, ffn1_w_gate[l], ffn1_w_up[l], ffn1_w_down[l], row(ln1_g[l]), row(ln1_b[l]))
        proj = _in_proj(h, _regroup_w_in(w_in[l]))
        qlat, qidx, ckv, ckvt, kidx, wt = _dsa_prep(
            proj, row(q_norm_g[l]), row(kv_norm_g[l]),
            w_uq[l].reshape(Q_LORA, A_HEADS * A_HEAD_DIM).astype(BF16), w_uk[l].astype(BF16),
            w_q_idx[l].astype(BF16), row(k_idx_ln_g[l]), row(k_idx_ln_b[l]))
        o_a = _dsa_attn(qlat, qidx, wt, kidx, ckv, ckvt, w_uv[l].astype(BF16), bsz=bsz, seq=seq)
        gq, gk, gv, gcol, grow = _gdn_prep(proj, conv_w[l].astype(F32), _lane_pad(a_log[l], SMALL_A),
                                           _lane_pad(dt_bias[l], SMALL_A), bsz=bsz, seq=seq)
        o_b = _gdn_chunk(gq, gk, gv, proj, gcol, grow, row(o_norm_g[l]), bsz=bsz, seq=seq)
        h = _out_ln(h, o_a, o_b, w_out[l].astype(BF16), row(ln2_g[l]), row(ln2_b[l]))
        h = _ffn_ln(h, ffn2_w_gate[l], ffn2_w_up[l], ffn2_w_down[l], row(ln3_g[l]), row(ln3_b[l]))
    return h.reshape(bsz, seq, d)
```

```python
import functools
import math

import jax
import jax.numpy as jnp
from jax import lax
from jax.experimental import pallas as pl
from jax.experimental.pallas import tpu as pltpu

F32 = jnp.float32
BF16 = jnp.bfloat16

A_HEADS = 8
A_HEAD_DIM = 128
Q_LORA = 512
KV_LORA = 256
IDX_HEADS = 16
IDX_DIM = 128
TOPK_MAX = 256
B_HEADS = 8
B_HEAD_DIM = 128
B_WIDTH = B_HEADS * B_HEAD_DIM
CONV_WIDTH = 4
DEPTH = 1
DEEPNORM_ALPHA = (2.0 * DEPTH) ** 0.25
EPS = 1e-6

LANES = 128
SUBLANES = 8
VMEM_LIMIT_BYTES = 60 * 1024 * 1024

COL_QKV = 0
COL_Z = 3 * B_WIDTH
COL_CQ = COL_Z + B_WIDTH
COL_CKV = COL_CQ + Q_LORA
COL_KIDX = COL_CKV + KV_LORA
COL_SMALL = COL_KIDX + IDX_DIM
D_IN_PAD = COL_SMALL + LANES
SMALL_W = 0
SMALL_A = IDX_HEADS
SMALL_B = IDX_HEADS + B_HEADS

Q_BLOCK = 128
S_TILE = 512
IDX_GROUP = 2
ATT_GROUP = 2
GDN_CHUNK = 128
INT_MIN = -2 ** 31
F32_LOWEST = -3.4028234663852886e38
LOG2E = math.log2(math.e)


def _dot(a, b):
    return jnp.dot(a, b, preferred_element_type=F32)


def _dot_nt(a, b):
    return lax.dot_general(a, b, (((1,), (1,)), ((), ())), preferred_element_type=F32)


def _sigmoid(x):
    return 1.0 / (1.0 + jnp.exp(-x))


def _layer_norm(y, g, b):
    mu = jnp.mean(y, axis=-1, keepdims=True)
    yc = y - mu
    var = jnp.mean(yc * yc, axis=-1, keepdims=True)
    return yc * lax.rsqrt(var + EPS) * g + b


def _params(*sem):
    return pltpu.CompilerParams(dimension_semantics=sem, vmem_limit_bytes=VMEM_LIMIT_BYTES)


def _ffn_ln_body(x_ref, wg_ref, wu_ref, wd_ref, g_ref, b_ref, o_ref, xb_ref):
    j = pl.program_id(1)

    @pl.when(j == 0)
    def _():
        xb_ref[...] = x_ref[...].astype(BF16)
        o_ref[...] = jnp.zeros_like(o_ref)

    xb = xb_ref[...]
    gate = _dot(xb, wg_ref[...].astype(BF16))
    up = _dot(xb, wu_ref[...].astype(BF16))
    act = gate * _sigmoid(gate) * up
    o_ref[...] += _dot(act.astype(BF16), wd_ref[...].astype(BF16))

    @pl.when(j == pl.num_programs(1) - 1)
    def _():
        y = DEEPNORM_ALPHA * x_ref[...] + 0.5 * o_ref[...]
        o_ref[...] = _layer_norm(y, g_ref[...], b_ref[...])


def _ffn_ln(x, wg, wu, wd, g, b, *, tm=1024, tf=256):
    t, d = x.shape
    f = wg.shape[1]
    return pl.pallas_call(
        _ffn_ln_body,
        out_shape=jax.ShapeDtypeStruct((t, d), F32),
        grid=(t // tm, f // tf),
        in_specs=[
            pl.BlockSpec((tm, d), lambda i, j: (i, 0)),
            pl.BlockSpec((d, tf), lambda i, j: (0, j)),
            pl.BlockSpec((d, tf), lambda i, j: (0, j)),
            pl.BlockSpec((tf, d), lambda i, j: (j, 0)),
            pl.BlockSpec((1, d), lambda i, j: (0, 0)),
            pl.BlockSpec((1, d), lambda i, j: (0, 0)),
        ],
        out_specs=pl.BlockSpec((tm, d), lambda i, j: (i, 0)),
        scratch_shapes=[pltpu.VMEM((tm, d), BF16)],
        compiler_params=_params("parallel", "arbitrary"),
        name="ffn_ln",
    )(x, wg, wu, wd, g, b)


def _in_proj_body(x_ref, *refs, tn):
    w_refs, o_ref = refs[:-1], refs[-1]
    xb = x_ref[...].astype(BF16)
    col = 0
    for w_ref in w_refs:
        for c in range(0, w_ref.shape[1], tn):
            o_ref[:, col + c:col + c + tn] = _dot(xb, w_ref[:, c:c + tn])
        col += w_ref.shape[1]


def _in_proj(x, weights, *, tm=512, tn=1024):
    t, d = x.shape
    n = sum(w.shape[1] for w in weights)
    return pl.pallas_call(
        functools.partial(_in_proj_body, tn=tn),
        out_shape=jax.ShapeDtypeStruct((t, n), F32),
        grid=(t // tm,),
        in_specs=[pl.BlockSpec((tm, d), lambda i: (i, 0))] + [
            pl.BlockSpec(w.shape, lambda i: (0, 0), pipeline_mode=pl.Buffered(1)) for w in weights],
        out_specs=pl.BlockSpec((tm, n), lambda i: (i, 0)),
        compiler_params=_params("parallel"),
        name="in_proj",
    )(x, *weights)


def _dsa_prep_body(cq_ref, ckv_ref, kidx_ref, small_ref, qg_ref, kvg_ref, wuq_ref, wuk_ref,
                   wqi_ref, lng_ref, lnb_ref,
                   qlat_ref, qidx_ref, ckv_o_ref, ckvt_o_ref, kidx_o_ref, wt_o_ref):
    tm = cq_ref.shape[0]
    nb = tm // Q_BLOCK

    cq = cq_ref[...]
    cq = cq * lax.rsqrt(jnp.mean(cq * cq, axis=-1, keepdims=True) + EPS) * qg_ref[...]
    cqb = cq.astype(BF16)

    q = _dot(cqb, wuq_ref[...]).astype(BF16)
    for h in range(A_HEADS):
        ql = _dot(q[:, h * A_HEAD_DIM:(h + 1) * A_HEAD_DIM], wuk_ref[h])
        ql = ql * (A_HEAD_DIM ** -0.5 * LOG2E)
        qlat_ref[:, h] = ql.astype(BF16).reshape(nb, Q_BLOCK, KV_LORA)

    qi = _dot(cqb, wqi_ref[...]).astype(BF16)
    for h in range(IDX_HEADS):
        qidx_ref[:, h] = qi[:, h * IDX_DIM:(h + 1) * IDX_DIM].reshape(nb, Q_BLOCK, IDX_DIM)

    ckv = ckv_ref[...]
    ckv = ckv * lax.rsqrt(jnp.mean(ckv * ckv, axis=-1, keepdims=True) + EPS) * kvg_ref[...]
    ckv_o_ref[...] = ckv.astype(BF16)
    ckvt_o_ref[0] = ckv.T.astype(BF16)

    kidx_o_ref[...] = _layer_norm(kidx_ref[...], lng_ref[...], lnb_ref[...]).astype(BF16)

    wt = small_ref[...].T
    wt_o_ref[...] = wt[SMALL_W:SMALL_W + IDX_HEADS, :] * (IDX_HEADS ** -0.5 * IDX_DIM ** -0.5)


def _dsa_prep(proj, qg, kvg, wuq, wuk, wqi, lng, lnb, *, tm=S_TILE):
    t = proj.shape[0]
    nq = t // Q_BLOCK
    full = lambda *shape: pl.BlockSpec(shape, lambda i: (0,) * len(shape))
    return pl.pallas_call(
        _dsa_prep_body,
        out_shape=(
            jax.ShapeDtypeStruct((nq, A_HEADS, Q_BLOCK, KV_LORA), BF16),
            jax.ShapeDtypeStruct((nq, IDX_HEADS, Q_BLOCK, IDX_DIM), BF16),
            jax.ShapeDtypeStruct((t, KV_LORA), BF16),
            jax.ShapeDtypeStruct((t // tm, KV_LORA, tm), BF16),
            jax.ShapeDtypeStruct((t, IDX_DIM), BF16),
            jax.ShapeDtypeStruct((IDX_HEADS, t), F32),
        ),
        grid=(t // tm,),
        in_specs=[
            pl.BlockSpec((tm, Q_LORA), lambda i: (i, COL_CQ // Q_LORA)),
            pl.BlockSpec((tm, KV_LORA), lambda i: (i, COL_CKV // KV_LORA)),
            pl.BlockSpec((tm, IDX_DIM), lambda i: (i, COL_KIDX // IDX_DIM)),
            pl.BlockSpec((tm, LANES), lambda i: (i, COL_SMALL // LANES)),
            full(1, Q_LORA), full(1, KV_LORA),
            full(Q_LORA, A_HEADS * A_HEAD_DIM),
            full(A_HEADS, A_HEAD_DIM, KV_LORA),
            full(Q_LORA, IDX_HEADS * IDX_DIM),
            full(1, IDX_DIM), full(1, IDX_DIM),
        ],
        out_specs=(
            pl.BlockSpec((tm // Q_BLOCK, A_HEADS, Q_BLOCK, KV_LORA), lambda i: (i, 0, 0, 0)),
            pl.BlockSpec((tm // Q_BLOCK, IDX_HEADS, Q_BLOCK, IDX_DIM), lambda i: (i, 0, 0, 0)),
            pl.BlockSpec((tm, KV_LORA), lambda i: (i, 0)),
            pl.BlockSpec((1, KV_LORA, tm), lambda i: (i, 0, 0)),
            pl.BlockSpec((tm, IDX_DIM), lambda i: (i, 0)),
            pl.BlockSpec((IDX_HEADS, tm), lambda i: (0, i)),
        ),
        compiler_params=_params("parallel"),
        name="dsa_prep",
    )(proj, proj, proj, proj, qg, kvg, wuq, wuk, wqi, lng, lnb)


def _key_to_f32(key):
    return pltpu.bitcast(key ^ ((key >> 31) & 0x7FFFFFFF), F32)


def _dsa_attn_body(qidx_ref, qlat_ref, wt_ref, kidx_ref, ckv_ref, ckvt_ref, wuv_ref, o_ref,
                   sc_ref, bias_ref, m_ref, l_ref, acc_ref, *, n_sel):
    i = pl.program_id(1)
    t0 = i * Q_BLOCK
    n_tiles = (t0 + Q_BLOCK + S_TILE - 1) // S_TILE

    t_lane = t0 + lax.broadcasted_iota(jnp.int32, (S_TILE, Q_BLOCK), 1)
    s_row = lax.broadcasted_iota(jnp.int32, (S_TILE, Q_BLOCK), 0)

    def tile_rows(j):
        return pl.ds(pl.multiple_of(j * S_TILE, S_TILE), S_TILE)

    wt = wt_ref[...]

    def score_tile(j, carry):
        k_tile = kidx_ref[tile_rows(j), :]
        sc = jnp.zeros((S_TILE, Q_BLOCK), F32)
        for g in range(IDX_HEADS // IDX_GROUP):
            qg = qidx_ref[0, g * IDX_GROUP:(g + 1) * IDX_GROUP].reshape(IDX_GROUP * Q_BLOCK, IDX_DIM)
            logits = _dot_nt(k_tile, qg)
            for u in range(IDX_GROUP):
                h = g * IDX_GROUP + u
                sc = sc + wt[h:h + 1, :] * jnp.maximum(logits[:, u * Q_BLOCK:(u + 1) * Q_BLOCK], 0.0)
        sc_ref[tile_rows(j), :] = jnp.where(s_row + j * S_TILE <= t_lane, sc, -jnp.inf)
        return carry

    lax.fori_loop(0, n_tiles, score_tile, 0)

    def count(pred):
        def body(j, acc):
            hit = jnp.where(pred(sc_ref[tile_rows(j), :]), 1, 0)
            return acc + jnp.sum(hit.reshape(S_TILE // SUBLANES, SUBLANES, Q_BLOCK), axis=0)

        acc = lax.fori_loop(0, n_tiles, body, jnp.zeros((SUBLANES, Q_BLOCK), jnp.int32))
        return jnp.sum(acc, axis=0, keepdims=True)

    def try_key(cand_key, key, n_ge):
        cand = _key_to_f32(cand_key)
        n = count(lambda sc: sc >= cand)
        ok = n >= n_sel
        return jnp.where(ok, cand_key, key), jnp.where(ok, n, n_ge)

    zero = jnp.zeros((1, Q_BLOCK), jnp.int32)
    key, n_ge = try_key(zero, zero + INT_MIN, zero)

    def bit_step(b, carry):
        return try_key(carry[0] | (1 << (30 - b)), *carry)

    key, n_ge = lax.fori_loop(0, 31, bit_step, (key, n_ge))
    thr = _key_to_f32(key)
    thr = jnp.maximum(jnp.where(thr != thr, F32_LOWEST, thr), F32_LOWEST)

    @pl.when(jnp.max(n_ge) > n_sel)
    def _():
        need = (n_sel - count(lambda sc: sc > thr)).astype(F32)
        r = lax.broadcasted_iota(jnp.int32, (S_TILE, S_TILE), 0)
        c = lax.broadcasted_iota(jnp.int32, (S_TILE, S_TILE), 1)
        lower_incl = jnp.where(c <= r, 1.0, 0.0).astype(BF16)

        def body(j, run):
            sc = sc_ref[tile_rows(j), :]
            tied = sc == thr
            rank = run + _dot(lower_incl, jnp.where(tied, 1.0, 0.0).astype(BF16))
            sc_ref[tile_rows(j), :] = jnp.where(tied & (rank > need), -jnp.inf, sc)
            return rank[S_TILE - 1:S_TILE, :]

        lax.fori_loop(0, n_tiles, body, jnp.zeros((1, Q_BLOCK), F32))

    gw = ATT_GROUP * Q_BLOCK
    groups = range(A_HEADS // ATT_GROUP)
    lane = lax.broadcasted_iota(jnp.int32, (1, gw), 1)
    slope_rows = []
    for g in groups:
        row_g = jnp.zeros((1, gw), F32)
        for u in range(ATT_GROUP):
            slope = 2.0 ** (-8.0 * (g * ATT_GROUP + u + 1) / A_HEADS) * LOG2E
            row_g = jnp.where(lane >= u * Q_BLOCK, slope, row_g)
        slope_rows.append(row_g)

    @pl.when(i == 0)
    def _():
        s_rel = lax.broadcasted_iota(jnp.int32, (S_TILE, gw), 0).astype(F32)
        for g in groups:
            bias_ref[g] = slope_rows[g] * s_rel

    m_ref[...] = jnp.full(m_ref.shape, -1e30, F32)
    l_ref[...] = jnp.zeros_like(l_ref)
    acc_ref[...] = jnp.zeros_like(acc_ref)

    def att_tile(j, carry):
        drop = jnp.where(sc_ref[tile_rows(j), :] >= thr, 0.0, -jnp.inf)
        drop = jnp.concatenate([drop] * ATT_GROUP, axis=1)
        ckv_tile = ckv_ref[tile_rows(j), :]
        ckvt_tile = ckvt_ref[j]
        off = (j * S_TILE - t0).astype(F32)
        logits = [_dot_nt(ckv_tile, qlat_ref[0, g * ATT_GROUP:(g + 1) * ATT_GROUP].reshape(gw, KV_LORA))
                  for g in groups]
        probs, alphas = [], []
        for g in groups:
            a = logits[g] + bias_ref[g] + drop
            shift = slope_rows[g] * off
            m_old = m_ref[g:g + 1, :]
            m_new = jnp.maximum(m_old, jnp.max(a, axis=0, keepdims=True) + shift)
            alpha = jnp.exp2(m_old - m_new)
            p = jnp.exp2(a - (m_new - shift))
            l_ref[g:g + 1, :] = alpha * l_ref[g:g + 1, :] + jnp.sum(p, axis=0, keepdims=True)
            m_ref[g:g + 1, :] = m_new
            probs.append(p.astype(BF16))
            alphas.append(alpha)
        for g in groups:
            acc_ref[g] = alphas[g] * acc_ref[g] + _dot(ckvt_tile, probs[g])
        return carry

    lax.fori_loop(0, n_tiles, att_tile, 0)

    for g in groups:
        o_lat_t = acc_ref[g] * (1.0 / l_ref[g:g + 1, :])
        for u in range(ATT_GROUP):
            h = g * ATT_GROUP + u
            o_lat = o_lat_t[:, u * Q_BLOCK:(u + 1) * Q_BLOCK].T.astype(BF16)
            o_ref[:, h * A_HEAD_DIM:(h + 1) * A_HEAD_DIM] = _dot(o_lat, wuv_ref[h]).astype(BF16)


def _dsa_attn(qlat, qidx, wt, kidx, ckv, ckvt, wuv, *, bsz, seq):
    nq = seq // Q_BLOCK
    ns = seq // S_TILE
    n_sel = min(TOPK_MAX, seq // 4)
    ng, gw = A_HEADS // ATT_GROUP, ATT_GROUP * Q_BLOCK
    return pl.pallas_call(
        functools.partial(_dsa_attn_body, n_sel=n_sel),
        out_shape=jax.ShapeDtypeStruct((bsz * seq, A_HEADS * A_HEAD_DIM), BF16),
        grid=(bsz, nq),
        in_specs=[
            pl.BlockSpec((1, IDX_HEADS, Q_BLOCK, IDX_DIM), lambda b, i: (b * nq + i, 0, 0, 0)),
            pl.BlockSpec((1, A_HEADS, Q_BLOCK, KV_LORA), lambda b, i: (b * nq + i, 0, 0, 0)),
            pl.BlockSpec((IDX_HEADS, Q_BLOCK), lambda b, i: (0, b * nq + i)),
            pl.BlockSpec((seq, IDX_DIM), lambda b, i: (b, 0)),
            pl.BlockSpec((seq, KV_LORA), lambda b, i: (b, 0)),
            pl.BlockSpec((ns, KV_LORA, S_TILE), lambda b, i: (b, 0, 0)),
            pl.BlockSpec((A_HEADS, KV_LORA, A_HEAD_DIM), lambda b, i: (0, 0, 0)),
        ],
        out_specs=pl.BlockSpec((Q_BLOCK, A_HEADS * A_HEAD_DIM), lambda b, i: (b * nq + i, 0)),
        scratch_shapes=[
            pltpu.VMEM((seq, Q_BLOCK), F32),
            pltpu.VMEM((ng, S_TILE, gw), F32),
            pltpu.VMEM((ng, gw), F32),
            pltpu.VMEM((ng, gw), F32),
            pltpu.VMEM((ng, KV_LORA, gw), F32),
        ],
        compiler_params=_params("parallel", "arbitrary"),
        name="dsa_attn",
    )(qidx, qlat, wt, kidx, ckv, ckvt, wuv)


def _gdn_prep_body(x_ref, halo_ref, small_ref, cw_ref, alog_ref, dtb_ref,
                   q_ref, k_ref, v_ref, gcol_ref, grow_ref, xx_ref):
    ts = x_ref.shape[0]
    first = pl.program_id(1) == 0
    halo = halo_ref[...]
    xx_ref[0:SUBLANES, :] = jnp.where(first, jnp.zeros_like(halo), halo)
    xx_ref[SUBLANES:, :] = x_ref[...]
    cw = cw_ref[...]
    y = jnp.zeros(x_ref.shape, F32)
    for tap in range(CONV_WIDTH):
        off = SUBLANES - (CONV_WIDTH - 1) + tap
        y = y + cw[tap:tap + 1, :] * xx_ref[off:off + ts, :]
    y = y * _sigmoid(y)

    for h in range(B_HEADS):
        lo, hi = h * B_HEAD_DIM, (h + 1) * B_HEAD_DIM
        qh = y[:, lo:hi]
        q_ref[:, lo:hi] = qh * lax.rsqrt(jnp.sum(qh * qh, axis=-1, keepdims=True) + EPS) * (
            B_HEAD_DIM ** -0.5)
        kh = y[:, B_WIDTH + lo:B_WIDTH + hi]
        k_ref[:, lo:hi] = kh * lax.rsqrt(jnp.sum(kh * kh, axis=-1, keepdims=True) + EPS)
    v_ref[...] = y[:, 2 * B_WIDTH:]

    small = small_ref[...]
    pre = small + dtb_ref[...]
    softplus = jnp.maximum(pre, 0.0) + jnp.log1p(jnp.exp(-jnp.abs(pre)))
    g = -jnp.exp(alog_ref[...]) * softplus
    row = lax.broadcasted_iota(jnp.int32, g.shape, 0) & (GDN_CHUNK - 1)
    shift = 1
    while shift < GDN_CHUNK:
        g = g + jnp.where(row >= shift, pltpu.roll(g, shift, 0), 0.0)
        shift *= 2
    lane = lax.broadcasted_iota(jnp.int32, g.shape, 1)
    is_a = (lane >= SMALL_A) & (lane < SMALL_A + B_HEADS)
    gb = jnp.where(is_a, g, _sigmoid(small))
    gcol_ref[...] = gb
    grow_ref[...] = gb.T


def _gdn_prep(proj, conv_w, alog_p, dtb_p, *, bsz, seq, ts=256):
    t = bsz * seq
    ns = seq // ts
    c = 3 * B_WIDTH
    return pl.pallas_call(
        _gdn_prep_body,
        out_shape=(
            jax.ShapeDtypeStruct((t, B_WIDTH), F32),
            jax.ShapeDtypeStruct((t, B_WIDTH), F32),
            jax.ShapeDtypeStruct((t, B_WIDTH), F32),
            jax.ShapeDtypeStruct((t, LANES), F32),
            jax.ShapeDtypeStruct((LANES, t), F32),
        ),
        grid=(bsz, ns),
        in_specs=[
            pl.BlockSpec((ts, c), lambda b, i: (b * ns + i, COL_QKV // c)),
            pl.BlockSpec((SUBLANES, c),
                         lambda b, i: (jnp.maximum((b * ns + i) * (ts // SUBLANES) - 1, 0), 0)),
            pl.BlockSpec((ts, LANES), lambda b, i: (b * ns + i, COL_SMALL // LANES)),
            pl.BlockSpec((CONV_WIDTH, c), lambda b, i: (0, 0)),
            pl.BlockSpec((1, LANES), lambda b, i: (0, 0)),
            pl.BlockSpec((1, LANES), lambda b, i: (0, 0)),
        ],
        out_specs=(
            pl.BlockSpec((ts, B_WIDTH), lambda b, i: (b * ns + i, 0)),
            pl.BlockSpec((ts, B_WIDTH), lambda b, i: (b * ns + i, 0)),
            pl.BlockSpec((ts, B_WIDTH), lambda b, i: (b * ns + i, 0)),
            pl.BlockSpec((ts, LANES), lambda b, i: (b * ns + i, 0)),
            pl.BlockSpec((LANES, ts), lambda b, i: (0, b * ns + i)),
        ),
        scratch_shapes=[pltpu.VMEM((ts + SUBLANES, c), F32)],
        compiler_params=_params("parallel", "arbitrary"),
        name="gdn_prep",
    )(proj, proj, proj, conv_w, alog_p, dtb_p)


def _dot1(a, b):
    return _dot(a.astype(BF16), b.astype(BF16))


def _each(fn, *lists):
    return [fn(*args) for args in zip(*lists)]


def _unit_lower_inverses(mats, row, col):
    base_log2 = 4
    eye = jnp.where(row == col, 1.0, 0.0)
    in_block = (row >> base_log2) == (col >> base_log2)
    d = [jnp.where(in_block, a, 0.0) for a in mats]
    x = [eye - dh for dh in d]
    p = _each(_dot1, d, d)
    for step in range(base_log2 - 1):
        x = _each(lambda xh, th: xh + th, x, _each(_dot1, x, p))
        if step < base_log2 - 2:
            p = _each(_dot1, p, p)
    sh = base_log2
    while (1 << sh) < GDN_CHUNK:
        same_parent = (row >> (sh + 1)) == (col >> (sh + 1))
        same_block = (row >> sh) == (col >> sh)
        quad = [jnp.where(same_block, 0.0, jnp.where(same_parent, a, 0.0)) for a in mats]
        x = _each(lambda xh, th: xh - th, x, _each(_dot1, x, _each(_dot1, quad, x)))
        sh += 1
    return x


def _gdn_chunk_body(q_ref, k_ref, v_ref, z_ref, gcol_ref, grow_ref, ng_ref, o_ref, state_ref):
    @pl.when(pl.program_id(1) == 0)
    def _():
        state_ref[...] = jnp.zeros_like(state_ref)

    c = GDN_CHUNK
    heads = range(B_HEADS)
    span = lambda h: slice(h * B_HEAD_DIM, (h + 1) * B_HEAD_DIM)
    row = lax.broadcasted_iota(jnp.int32, (c, c), 0)
    col = lax.broadcasted_iota(jnp.int32, (c, c), 1)
    gcol = gcol_ref[...]
    grow = grow_ref[...]
    e_cum = jnp.exp(gcol)
    e_rest = jnp.exp(gcol[c - 1:c, :] - gcol)
    e_last = jnp.exp(grow[:, c - 1:c])

    q = [q_ref[:, span(h)] for h in heads]
    k = [k_ref[:, span(h)] for h in heads]
    v = [v_ref[:, span(h)] for h in heads]
    beta = [gcol[:, SMALL_B + h:SMALL_B + h + 1] for h in heads]
    decay = [jnp.exp(jnp.where(row >= col,
                               gcol[:, SMALL_A + h:SMALL_A + h + 1] - grow[SMALL_A + h:SMALL_A + h + 1, :],
                               -jnp.inf)) for h in heads]
    k_beta = _each(lambda kh, bh: kh * bh, k, beta)
    k16 = [kh.astype(BF16) for kh in k]
    kk = _each(_dot_nt, [kb.astype(BF16) for kb in k_beta], k16)
    qk = _each(_dot_nt, [qh.astype(BF16) for qh in q], k16)
    a_mat = _each(lambda m, dh: jnp.where(row > col, m * dh, 0.0), kk, decay)
    attn = _each(lambda m, dh: (m * dh).astype(BF16), qk, decay)
    t_inv = _unit_lower_inverses(a_mat, row, col)
    rhs = [jnp.concatenate([v[h] * beta[h], k_beta[h] * e_cum[:, SMALL_A + h:SMALL_A + h + 1]], axis=1)
           for h in heads]
    sol = _each(_dot1, t_inv, rhs)
    q_dec = [(q[h] * e_cum[:, SMALL_A + h:SMALL_A + h + 1]).astype(BF16) for h in heads]
    k_dec_t = [(k[h] * e_rest[:, SMALL_A + h:SMALL_A + h + 1]).T.astype(BF16) for h in heads]

    state = [state_ref[h] for h in heads]
    s16 = [s.astype(BF16) for s in state]
    v_new = [sol[h][:, :B_HEAD_DIM] - _dot(sol[h][:, B_HEAD_DIM:].astype(BF16), s16[h]) for h in heads]
    v16 = [x.astype(BF16) for x in v_new]
    for h in heads:
        state_ref[h] = state[h] * e_last[SMALL_A + h:SMALL_A + h + 1, :] + _dot(k_dec_t[h], v16[h])
    out = [_dot(q_dec[h], s16[h]) + _dot(attn[h], v16[h]) for h in heads]
    for h in heads:
        o = out[h]
        y = o * lax.rsqrt(jnp.mean(o * o, axis=-1, keepdims=True) + EPS) * ng_ref[...]
        zh = z_ref[:, span(h)]
        o_ref[:, span(h)] = (y * (zh * _sigmoid(zh))).astype(BF16)


def _gdn_chunk(q, k, v, proj, gcol, grow, ng, *, bsz, seq):
    c = GDN_CHUNK
    nc = seq // c
    tok = lambda b, i: (b * nc + i, 0)
    return pl.pallas_call(
        _gdn_chunk_body,
        out_shape=jax.ShapeDtypeStruct((bsz * seq, B_WIDTH), BF16),
        grid=(bsz, nc),
        in_specs=[
            pl.BlockSpec((c, B_WIDTH), tok),
            pl.BlockSpec((c, B_WIDTH), tok),
            pl.BlockSpec((c, B_WIDTH), tok),
            pl.BlockSpec((c, B_WIDTH), lambda b, i: (b * nc + i, COL_Z // B_WIDTH)),
            pl.BlockSpec((c, LANES), tok),
            pl.BlockSpec((LANES, c), lambda b, i: (0, b * nc + i)),
            pl.BlockSpec((1, B_HEAD_DIM), lambda b, i: (0, 0)),
        ],
        out_specs=pl.BlockSpec((c, B_WIDTH), tok),
        scratch_shapes=[pltpu.VMEM((B_HEADS, B_HEAD_DIM, B_HEAD_DIM), F32)],
        compiler_params=_params("parallel", "arbitrary"),
        name="gdn_chunk",
    )(q, k, v, proj, gcol, grow, ng)


def _out_ln_body(h_ref, oa_ref, ob_ref, wa_ref, wb_ref, g_ref, b_ref, o_ref, *, sub):
    spans = [slice(r, r + sub) for r in range(0, h_ref.shape[0], sub)]
    mixes = [_dot(oa_ref[s, :], wa_ref[...]) + _dot(ob_ref[s, :], wb_ref[...]) for s in spans]
    for s, mix in zip(spans, mixes):
        o_ref[s, :] = _layer_norm(DEEPNORM_ALPHA * h_ref[s, :] + mix, g_ref[...], b_ref[...])


def _out_ln(h, oa, ob, w, g, b, *, tm=512, sub=256):
    t, d = h.shape
    da, db = oa.shape[1], ob.shape[1]
    assert da == db and w.shape == (da + db, d)
    return pl.pallas_call(
        functools.partial(_out_ln_body, sub=sub),
        out_shape=jax.ShapeDtypeStruct((t, d), F32),
        grid=(t // tm,),
        in_specs=[
            pl.BlockSpec((tm, d), lambda i: (i, 0)),
            pl.BlockSpec((tm, da), lambda i: (i, 0)),
            pl.BlockSpec((tm, db), lambda i: (i, 0)),
            pl.BlockSpec((da, d), lambda i: (0, 0)),
            pl.BlockSpec((db, d), lambda i: (1, 0)),
            pl.BlockSpec((1, d), lambda i: (0, 0)),
            pl.BlockSpec((1, d), lambda i: (0, 0)),
        ],
        out_specs=pl.BlockSpec((tm, d), lambda i: (i, 0)),
        compiler_params=_params("parallel"),
        name="out_ln",
    )(h, oa, ob, w, w, g, b)


def _regroup_w_in(w_in):
    cuts = [0, Q_LORA, KV_LORA, IDX_DIM, IDX_HEADS, B_WIDTH, B_WIDTH, B_WIDTH, B_WIDTH, B_HEADS, B_HEADS]
    offs = [sum(cuts[:n + 1]) for n in range(len(cuts))]
    w16 = w_in.astype(BF16)
    part = lambda lo, hi: w16[:, offs[lo]:offs[hi]]
    pad = jnp.zeros((w_in.shape[0], LANES - IDX_HEADS - 2 * B_HEADS), BF16)
    small = jnp.concatenate([part(0, 3), part(3, 4), part(8, 10), pad], axis=1)
    return part(4, 8), small


def _lane_pad(vec, offset):
    out = jnp.zeros((1, LANES), F32)
    return out.at[0, offset:offset + vec.shape[0]].set(vec.astype(F32))


def kernel(x, ffn1_w_gate, ffn1_w_up, ffn1_w_down, ln1_g, ln1_b, w_in, q_norm_g, kv_norm_g, w_uq, w_uk, w_uv, w_q_idx, k_idx_ln_g, k_idx_ln_b, conv_w, a_log, dt_bias, o_norm_g, w_out, ln2_g, ln2_b, ffn2_w_gate, ffn2_w_up, ffn2_w_down, ln3_g, ln3_b):
    bsz, seq, d = x.shape
    assert seq % S_TILE == 0 and seq % GDN_CHUNK == 0 and x.dtype == F32
    t = bsz * seq
    row = lambda p: p.reshape(1, -1).astype(F32)
    h = x.reshape(t, d)
    for l in range(DEPTH):
        h = _ffn_ln(h, ffn1_w_gate[l], ffn1_w_up[l], ffn1_w_down[l], row(ln1_g[l]), row(ln1_b[l]))
        proj = _in_proj(h, _regroup_w_in(w_in[l]))
        qlat, qidx, ckv, ckvt, kidx, wt = _dsa_prep(
            proj, row(q_norm_g[l]), row(kv_norm_g[l]),
            w_uq[l].reshape(Q_LORA, A_HEADS * A_HEAD_DIM).astype(BF16), w_uk[l].astype(BF16),
            w_q_idx[l].astype(BF16), row(k_idx_ln_g[l]), row(k_idx_ln_b[l]))
        o_a = _dsa_attn(qlat, qidx, wt, kidx, ckv, ckvt, w_uv[l].astype(BF16), bsz=bsz, seq=seq)
        gq, gk, gv, gcol, grow = _gdn_prep(proj, conv_w[l].astype(F32), _lane_pad(a_log[l], SMALL_A),
                                           _lane_pad(dt_bias[l], SMALL_A), bsz=bsz, seq=seq)
        o_b = _gdn_chunk(gq, gk, gv, proj, gcol, grow, row(o_norm_g[l]), bsz=bsz, seq=seq)
        h = _out_ln(h, o_a, o_b, w_out[l].astype(BF16), row(ln2_g[l]), row(ln2_b[l]))
        h = _ffn_ln(h, ffn2_w_gate[l], ffn2_w_up[l], ffn2_w_down[l], row(ln3_g[l]), row(ln3_b[l]))
    return h.reshape(bsz, seq, d)
```

```python
import functools
import math

import jax
import jax.numpy as jnp
from jax import lax
from jax.experimental import pallas as pl
from jax.experimental.pallas import tpu as pltpu

F32 = jnp.float32
BF16 = jnp.bfloat16

A_HEADS = 8
A_HEAD_DIM = 128
Q_LORA = 512
KV_LORA = 256
IDX_HEADS = 16
IDX_DIM = 128
TOPK_MAX = 256
B_HEADS = 8
B_HEAD_DIM = 128
B_WIDTH = B_HEADS * B_HEAD_DIM
CONV_WIDTH = 4
DEPTH = 1
DEEPNORM_ALPHA = (2.0 * DEPTH) ** 0.25
EPS = 1e-6

LANES = 128
SUBLANES = 8
BF16_ROWS = 16
VMEM_LIMIT_BYTES = 60 * 1024 * 1024

COL_QKV = 0
COL_Z = 3 * B_WIDTH
COL_CQ = COL_Z + B_WIDTH
COL_CKV = COL_CQ + Q_LORA
COL_KIDX = COL_CKV + KV_LORA
COL_SMALL = COL_KIDX + IDX_DIM
D_IN_PAD = COL_SMALL + LANES
SMALL_W = 0
SMALL_A = IDX_HEADS
SMALL_B = IDX_HEADS + B_HEADS

Q_BLOCK = 256
S_TILE = 512
IDX_GROUP = 1
ATT_GROUP = 1
GDN_CHUNK = 128
INT_MIN = -2 ** 31
F32_LOWEST = -3.4028234663852886e38
LOG2E = math.log2(math.e)


def _dot(a, b):
    return jnp.dot(a, b, preferred_element_type=F32)


def _dot_nt(a, b):
    return lax.dot_general(a, b, (((1,), (1,)), ((), ())), preferred_element_type=F32)


def _sigmoid(x):
    return 1.0 / (1.0 + jnp.exp(-x))


def _layer_norm(y, g, b):
    mu = jnp.mean(y, axis=-1, keepdims=True)
    yc = y - mu
    var = jnp.mean(yc * yc, axis=-1, keepdims=True)
    return yc * lax.rsqrt(var + EPS) * g + b


def _params(*sem):
    return pltpu.CompilerParams(dimension_semantics=sem, vmem_limit_bytes=VMEM_LIMIT_BYTES)


def _ffn_ln_body(x_ref, wg_ref, wu_ref, wd_ref, g_ref, b_ref, o_ref, xb_ref):
    j = pl.program_id(1)

    @pl.when(j == 0)
    def _():
        xb_ref[...] = x_ref[...].astype(BF16)
        o_ref[...] = jnp.zeros_like(o_ref)

    xb = xb_ref[...]
    gate = _dot(xb, wg_ref[...].astype(BF16))
    up = _dot(xb, wu_ref[...].astype(BF16))
    act = gate * _sigmoid(gate) * up
    o_ref[...] += _dot(act.astype(BF16), wd_ref[...].astype(BF16))

    @pl.when(j == pl.num_programs(1) - 1)
    def _():
        y = DEEPNORM_ALPHA * x_ref[...] + 0.5 * o_ref[...]
        o_ref[...] = _layer_norm(y, g_ref[...], b_ref[...])


def _ffn_ln(x, wg, wu, wd, g, b, *, tm=1024, tf=256):
    t, d = x.shape
    f = wg.shape[1]
    return pl.pallas_call(
        _ffn_ln_body,
        out_shape=jax.ShapeDtypeStruct((t, d), F32),
        grid=(t // tm, f // tf),
        in_specs=[
            pl.BlockSpec((tm, d), lambda i, j: (i, 0)),
            pl.BlockSpec((d, tf), lambda i, j: (0, j)),
            pl.BlockSpec((d, tf), lambda i, j: (0, j)),
            pl.BlockSpec((tf, d), lambda i, j: (j, 0)),
            pl.BlockSpec((1, d), lambda i, j: (0, 0)),
            pl.BlockSpec((1, d), lambda i, j: (0, 0)),
        ],
        out_specs=pl.BlockSpec((tm, d), lambda i, j: (i, 0)),
        scratch_shapes=[pltpu.VMEM((tm, d), BF16)],
        compiler_params=_params("parallel", "arbitrary"),
        name="ffn_ln",
    )(x, wg, wu, wd, g, b)


def _in_proj_body(x_ref, *refs, tn):
    w_refs, o_ref = refs[:-1], refs[-1]
    xb = x_ref[...].astype(BF16)
    col = 0
    for w_ref in w_refs:
        for c in range(0, w_ref.shape[1], tn):
            o_ref[:, col + c:col + c + tn] = _dot(xb, w_ref[:, c:c + tn])
        col += w_ref.shape[1]


def _in_proj(x, weights, *, tm=512, tn=1024):
    t, d = x.shape
    n = sum(w.shape[1] for w in weights)
    return pl.pallas_call(
        functools.partial(_in_proj_body, tn=tn),
        out_shape=jax.ShapeDtypeStruct((t, n), F32),
        grid=(t // tm,),
        in_specs=[pl.BlockSpec((tm, d), lambda i: (i, 0))] + [
            pl.BlockSpec(w.shape, lambda i: (0, 0), pipeline_mode=pl.Buffered(1)) for w in weights],
        out_specs=pl.BlockSpec((tm, n), lambda i: (i, 0)),
        compiler_params=_params("parallel"),
        name="in_proj",
    )(x, *weights)


def _dsa_prep_body(cq_ref, ckv_ref, kidx_ref, small_ref, qg_ref, kvg_ref, wuq_ref, wuk_ref,
                   wqi_ref, lng_ref, lnb_ref,
                   qlat_ref, qidx_ref, ckv_o_ref, ckvt_o_ref, kidx_o_ref, wt_o_ref):
    tm = cq_ref.shape[0]
    nb = tm // Q_BLOCK

    cq = cq_ref[...]
    cq = cq * lax.rsqrt(jnp.mean(cq * cq, axis=-1, keepdims=True) + EPS) * qg_ref[...]
    cqb = cq.astype(BF16)

    q = _dot(cqb, wuq_ref[...]).astype(BF16)
    for h in range(A_HEADS):
        ql = _dot(q[:, h * A_HEAD_DIM:(h + 1) * A_HEAD_DIM], wuk_ref[h])
        ql = ql * (A_HEAD_DIM ** -0.5 * LOG2E)
        qlat_ref[:, h] = ql.astype(BF16).reshape(nb, Q_BLOCK, KV_LORA)

    qi = _dot(cqb, wqi_ref[...]).astype(BF16)
    for h in range(IDX_HEADS):
        qidx_ref[:, h] = qi[:, h * IDX_DIM:(h + 1) * IDX_DIM].reshape(nb, Q_BLOCK, IDX_DIM)

    ckv = ckv_ref[...]
    ckv = ckv * lax.rsqrt(jnp.mean(ckv * ckv, axis=-1, keepdims=True) + EPS) * kvg_ref[...]
    ckv_o_ref[...] = ckv.astype(BF16)
    ckvt_o_ref[0] = ckv.T.astype(BF16)

    kidx_o_ref[...] = _layer_norm(kidx_ref[...], lng_ref[...], lnb_ref[...]).astype(BF16)

    wt = small_ref[...].T
    wt_o_ref[...] = wt[SMALL_W:SMALL_W + IDX_HEADS, :] * (IDX_HEADS ** -0.5 * IDX_DIM ** -0.5)


def _dsa_prep(proj, qg, kvg, wuq, wuk, wqi, lng, lnb, *, tm=S_TILE):
    t = proj.shape[0]
    nq = t // Q_BLOCK
    full = lambda *shape: pl.BlockSpec(shape, lambda i: (0,) * len(shape))
    return pl.pallas_call(
        _dsa_prep_body,
        out_shape=(
            jax.ShapeDtypeStruct((nq, A_HEADS, Q_BLOCK, KV_LORA), BF16),
            jax.ShapeDtypeStruct((nq, IDX_HEADS, Q_BLOCK, IDX_DIM), BF16),
            jax.ShapeDtypeStruct((t, KV_LORA), BF16),
            jax.ShapeDtypeStruct((t // tm, KV_LORA, tm), BF16),
            jax.ShapeDtypeStruct((t, IDX_DIM), BF16),
            jax.ShapeDtypeStruct((IDX_HEADS, t), F32),
        ),
        grid=(t // tm,),
        in_specs=[
            pl.BlockSpec((tm, Q_LORA), lambda i: (i, COL_CQ // Q_LORA)),
            pl.BlockSpec((tm, KV_LORA), lambda i: (i, COL_CKV // KV_LORA)),
            pl.BlockSpec((tm, IDX_DIM), lambda i: (i, COL_KIDX // IDX_DIM)),
            pl.BlockSpec((tm, LANES), lambda i: (i, COL_SMALL // LANES)),
            full(1, Q_LORA), full(1, KV_LORA),
            full(Q_LORA, A_HEADS * A_HEAD_DIM),
            full(A_HEADS, A_HEAD_DIM, KV_LORA),
            full(Q_LORA, IDX_HEADS * IDX_DIM),
            full(1, IDX_DIM), full(1, IDX_DIM),
        ],
        out_specs=(
            pl.BlockSpec((tm // Q_BLOCK, A_HEADS, Q_BLOCK, KV_LORA), lambda i: (i, 0, 0, 0)),
            pl.BlockSpec((tm // Q_BLOCK, IDX_HEADS, Q_BLOCK, IDX_DIM), lambda i: (i, 0, 0, 0)),
            pl.BlockSpec((tm, KV_LORA), lambda i: (i, 0)),
            pl.BlockSpec((1, KV_LORA, tm), lambda i: (i, 0, 0)),
            pl.BlockSpec((tm, IDX_DIM), lambda i: (i, 0)),
            pl.BlockSpec((IDX_HEADS, tm), lambda i: (0, i)),
        ),
        compiler_params=_params("parallel"),
        name="dsa_prep",
    )(proj, proj, proj, proj, qg, kvg, wuq, wuk, wqi, lng, lnb)


def _key_to_f32(key):
    return pltpu.bitcast(key ^ ((key >> 31) & 0x7FFFFFFF), F32)


def _dsa_attn_body(qidx_ref, qlat_ref, wt_ref, kidx_ref, ckv_ref, ckvt_ref, wuv_ref, o_ref,
                   sc_ref, sh_ref, bias_ref, m_ref, l_ref, acc_ref, *, n_sel):
    i = pl.program_id(1)
    t0 = i * Q_BLOCK
    n_tiles = (t0 + Q_BLOCK + S_TILE - 1) // S_TILE

    t_lane = t0 + lax.broadcasted_iota(jnp.int32, (S_TILE, Q_BLOCK), 1)
    s_row = lax.broadcasted_iota(jnp.int32, (S_TILE, Q_BLOCK), 0)

    def tile_rows(j):
        return pl.ds(pl.multiple_of(j * S_TILE, S_TILE), S_TILE)

    wt = wt_ref[...]

    def score_tile(j, carry):
        k_tile = kidx_ref[tile_rows(j), :]
        sc = jnp.zeros((S_TILE, Q_BLOCK), F32)
        for g in range(IDX_HEADS // IDX_GROUP):
            qg = qidx_ref[0, g * IDX_GROUP:(g + 1) * IDX_GROUP].reshape(IDX_GROUP * Q_BLOCK, IDX_DIM)
            logits = _dot_nt(k_tile, qg)
            for u in range(IDX_GROUP):
                h = g * IDX_GROUP + u
                sc = sc + wt[h:h + 1, :] * jnp.maximum(logits[:, u * Q_BLOCK:(u + 1) * Q_BLOCK], 0.0)
        sc = jnp.where(s_row + j * S_TILE <= t_lane, sc, -jnp.inf)
        sc_ref[tile_rows(j), :] = sc
        sh_ref[tile_rows(j), :] = sc.astype(BF16)
        return carry

    lax.fori_loop(0, n_tiles, score_tile, 0)

    def count(pred):
        def body(j, acc):
            hit = jnp.where(pred(sc_ref[tile_rows(j), :]), 1, 0)
            return acc + jnp.sum(hit.reshape(S_TILE // SUBLANES, SUBLANES, Q_BLOCK), axis=0)

        acc = lax.fori_loop(0, n_tiles, body, jnp.zeros((SUBLANES, Q_BLOCK), jnp.int32))
        return jnp.sum(acc, axis=0, keepdims=True)

    one16, zero16 = jnp.ones((), BF16), jnp.zeros((), BF16)

    def count_rounded_ge(key16):
        cand = _key_to_f32((key16 << 16) | ((key16 >> 31) & 0xFFFF)).astype(BF16)

        def body(j, acc):
            hit = jnp.where(sh_ref[tile_rows(j), :] >= cand, one16, zero16)
            parts = [hit[r:r + BF16_ROWS, :] for r in range(0, S_TILE, BF16_ROWS)]
            while len(parts) > 1:
                parts = [a + b for a, b in zip(parts[0::2], parts[1::2])]
            return acc + parts[0]

        acc = lax.fori_loop(0, n_tiles, body, jnp.zeros((BF16_ROWS, Q_BLOCK), BF16))
        return jnp.sum(acc.astype(F32), axis=0, keepdims=True).astype(jnp.int32)

    def try_key16(cand16, key16):
        return jnp.where(count_rounded_ge(cand16) >= n_sel, cand16, key16)

    zero = jnp.zeros((1, Q_BLOCK), jnp.int32)
    key16 = try_key16(zero, zero - 2 ** 15)
    key16 = lax.fori_loop(0, 15, lambda b, k: try_key16(k | (1 << (14 - b)), k), key16)

    coarse = (key16 << 16) | ((key16 >> 31) & 0xFFFF)
    base = coarse - 2 ** 15

    def try_offset(cand_off, off, key, n_ge):
        cand_key = base + cand_off
        cand = _key_to_f32(cand_key)
        n = count(lambda sc: sc >= cand)
        ok = n >= n_sel
        return jnp.where(ok, cand_off, off), jnp.where(ok, cand_key, key), jnp.where(ok, n, n_ge)

    carry = try_offset(zero, zero, zero + INT_MIN, zero)
    _, key, n_ge = lax.fori_loop(0, 17, lambda b, c: try_offset(c[0] | (1 << (16 - b)), *c), carry)
    thr = _key_to_f32(key)
    thr = jnp.maximum(jnp.where(thr != thr, F32_LOWEST, thr), F32_LOWEST)

    @pl.when(jnp.max(n_ge) > n_sel)
    def _():
        need = (n_sel - count(lambda sc: sc > thr)).astype(F32)
        r = lax.broadcasted_iota(jnp.int32, (S_TILE, S_TILE), 0)
        c = lax.broadcasted_iota(jnp.int32, (S_TILE, S_TILE), 1)
        lower_incl = jnp.where(c <= r, 1.0, 0.0).astype(BF16)

        def body(j, run):
            sc = sc_ref[tile_rows(j), :]
            tied = sc == thr
            rank = run + _dot(lower_incl, jnp.where(tied, 1.0, 0.0).astype(BF16))
            sc_ref[tile_rows(j), :] = jnp.where(tied & (rank > need), -jnp.inf, sc)
            return rank[S_TILE - 1:S_TILE, :]

        lax.fori_loop(0, n_tiles, body, jnp.zeros((1, Q_BLOCK), F32))

    gw = ATT_GROUP * Q_BLOCK
    groups = range(A_HEADS // ATT_GROUP)
    lane = lax.broadcasted_iota(jnp.int32, (1, gw), 1)
    slope_rows = []
    for g in groups:
        row_g = jnp.zeros((1, gw), F32)
        for u in range(ATT_GROUP):
            slope = 2.0 ** (-8.0 * (g * ATT_GROUP + u + 1) / A_HEADS) * LOG2E
            row_g = jnp.where(lane >= u * Q_BLOCK, slope, row_g)
        slope_rows.append(row_g)

    @pl.when(i == 0)
    def _():
        s_rel = lax.broadcasted_iota(jnp.int32, (S_TILE, gw), 0).astype(F32)
        for g in groups:
            bias_ref[g] = slope_rows[g] * s_rel

    m_ref[...] = jnp.full(m_ref.shape, -1e30, F32)
    l_ref[...] = jnp.zeros_like(l_ref)
    acc_ref[...] = jnp.zeros_like(acc_ref)

    def att_tile(j, carry):
        drop = jnp.where(sc_ref[tile_rows(j), :] >= thr, 0.0, -jnp.inf)
        drop = jnp.concatenate([drop] * ATT_GROUP, axis=1)
        ckv_tile = ckv_ref[tile_rows(j), :]
        ckvt_tile = ckvt_ref[j]
        off = (j * S_TILE - t0).astype(F32)
        logits = [_dot_nt(ckv_tile, qlat_ref[0, g * ATT_GROUP:(g + 1) * ATT_GROUP].reshape(gw, KV_LORA))
                  for g in groups]
        probs, alphas = [], []
        for g in groups:
            a = logits[g] + bias_ref[g] + drop
            shift = slope_rows[g] * off
            m_old = m_ref[g:g + 1, :]
            m_new = jnp.maximum(m_old, jnp.max(a, axis=0, keepdims=True) + shift)
            alpha = jnp.exp2(m_old - m_new)
            p = jnp.exp2(a - (m_new - shift))
            l_ref[g:g + 1, :] = alpha * l_ref[g:g + 1, :] + jnp.sum(p, axis=0, keepdims=True)
            m_ref[g:g + 1, :] = m_new
            probs.append(p.astype(BF16))
            alphas.append(alpha)
        for g in groups:
            acc_ref[g] = alphas[g] * acc_ref[g] + _dot(ckvt_tile, probs[g])
        return carry

    lax.fori_loop(0, n_tiles, att_tile, 0)

    for g in groups:
        o_lat_t = acc_ref[g] * (1.0 / l_ref[g:g + 1, :])
        for u in range(ATT_GROUP):
            h = g * ATT_GROUP + u
            o_lat = o_lat_t[:, u * Q_BLOCK:(u + 1) * Q_BLOCK].T.astype(BF16)
            o_ref[:, h * A_HEAD_DIM:(h + 1) * A_HEAD_DIM] = _dot(o_lat, wuv_ref[h]).astype(BF16)


def _dsa_attn(qlat, qidx, wt, kidx, ckv, ckvt, wuv, *, bsz, seq):
    nq = seq // Q_BLOCK
    ns = seq // S_TILE
    n_sel = min(TOPK_MAX, seq // 4)
    ng, gw = A_HEADS // ATT_GROUP, ATT_GROUP * Q_BLOCK
    return pl.pallas_call(
        functools.partial(_dsa_attn_body, n_sel=n_sel),
        out_shape=jax.ShapeDtypeStruct((bsz * seq, A_HEADS * A_HEAD_DIM), BF16),
        grid=(bsz, nq),
        in_specs=[
            pl.BlockSpec((1, IDX_HEADS, Q_BLOCK, IDX_DIM), lambda b, i: (b * nq + i, 0, 0, 0)),
            pl.BlockSpec((1, A_HEADS, Q_BLOCK, KV_LORA), lambda b, i: (b * nq + i, 0, 0, 0)),
            pl.BlockSpec((IDX_HEADS, Q_BLOCK), lambda b, i: (0, b * nq + i)),
            pl.BlockSpec((seq, IDX_DIM), lambda b, i: (b, 0)),
            pl.BlockSpec((seq, KV_LORA), lambda b, i: (b, 0)),
            pl.BlockSpec((ns, KV_LORA, S_TILE), lambda b, i: (b, 0, 0)),
            pl.BlockSpec((A_HEADS, KV_LORA, A_HEAD_DIM), lambda b, i: (0, 0, 0)),
        ],
        out_specs=pl.BlockSpec((Q_BLOCK, A_HEADS * A_HEAD_DIM), lambda b, i: (b * nq + i, 0)),
        scratch_shapes=[
            pltpu.VMEM((seq, Q_BLOCK), F32),
            pltpu.VMEM((seq, Q_BLOCK), BF16),
            pltpu.VMEM((ng, S_TILE, gw), F32),
            pltpu.VMEM((ng, gw), F32),
            pltpu.VMEM((ng, gw), F32),
            pltpu.VMEM((ng, KV_LORA, gw), F32),
        ],
        compiler_params=_params("parallel", "arbitrary"),
        name="dsa_attn",
    )(qidx, qlat, wt, kidx, ckv, ckvt, wuv)


def _gdn_prep_body(x_ref, halo_ref, small_ref, cw_ref, alog_ref, dtb_ref,
                   q_ref, k_ref, v_ref, gcol_ref, grow_ref, xx_ref):
    ts = x_ref.shape[0]
    first = pl.program_id(1) == 0
    halo = halo_ref[...]
    xx_ref[0:SUBLANES, :] = jnp.where(first, jnp.zeros_like(halo), halo)
    xx_ref[SUBLANES:, :] = x_ref[...]
    cw = cw_ref[...]
    y = jnp.zeros(x_ref.shape, F32)
    for tap in range(CONV_WIDTH):
        off = SUBLANES - (CONV_WIDTH - 1) + tap
        y = y + cw[tap:tap + 1, :] * xx_ref[off:off + ts, :]
    y = y * _sigmoid(y)

    for h in range(B_HEADS):
        lo, hi = h * B_HEAD_DIM, (h + 1) * B_HEAD_DIM
        qh = y[:, lo:hi]
        q_ref[:, lo:hi] = qh * lax.rsqrt(jnp.sum(qh * qh, axis=-1, keepdims=True) + EPS) * (
            B_HEAD_DIM ** -0.5)
        kh = y[:, B_WIDTH + lo:B_WIDTH + hi]
        k_ref[:, lo:hi] = kh * lax.rsqrt(jnp.sum(kh * kh, axis=-1, keepdims=True) + EPS)
    v_ref[...] = y[:, 2 * B_WIDTH:]

    small = small_ref[...]
    pre = small + dtb_ref[...]
    softplus = jnp.maximum(pre, 0.0) + jnp.log1p(jnp.exp(-jnp.abs(pre)))
    g = -jnp.exp(alog_ref[...]) * softplus
    row = lax.broadcasted_iota(jnp.int32, g.shape, 0) & (GDN_CHUNK - 1)
    shift = 1
    while shift < GDN_CHUNK:
        g = g + jnp.where(row >= shift, pltpu.roll(g, shift, 0), 0.0)
        shift *= 2
    lane = lax.broadcasted_iota(jnp.int32, g.shape, 1)
    is_a = (lane >= SMALL_A) & (lane < SMALL_A + B_HEADS)
    gb = jnp.where(is_a, g, _sigmoid(small))
    gcol_ref[...] = gb
    grow_ref[...] = gb.T


def _gdn_prep(proj, conv_w, alog_p, dtb_p, *, bsz, seq, ts=256):
    t = bsz * seq
    ns = seq // ts
    c = 3 * B_WIDTH
    return pl.pallas_call(
        _gdn_prep_body,
        out_shape=(
            jax.ShapeDtypeStruct((t, B_WIDTH), F32),
            jax.ShapeDtypeStruct((t, B_WIDTH), F32),
            jax.ShapeDtypeStruct((t, B_WIDTH), F32),
            jax.ShapeDtypeStruct((t, LANES), F32),
            jax.ShapeDtypeStruct((LANES, t), F32),
        ),
        grid=(bsz, ns),
        in_specs=[
            pl.BlockSpec((ts, c), lambda b, i: (b * ns + i, COL_QKV // c)),
            pl.BlockSpec((SUBLANES, c),
                         lambda b, i: (jnp.maximum((b * ns + i) * (ts // SUBLANES) - 1, 0), 0)),
            pl.BlockSpec((ts, LANES), lambda b, i: (b * ns + i, COL_SMALL // LANES)),
            pl.BlockSpec((CONV_WIDTH, c), lambda b, i: (0, 0)),
            pl.BlockSpec((1, LANES), lambda b, i: (0, 0)),
            pl.BlockSpec((1, LANES), lambda b, i: (0, 0)),
        ],
        out_specs=(
            pl.BlockSpec((ts, B_WIDTH), lambda b, i: (b * ns + i, 0)),
            pl.BlockSpec((ts, B_WIDTH), lambda b, i: (b * ns + i, 0)),
            pl.BlockSpec((ts, B_WIDTH), lambda b, i: (b * ns + i, 0)),
            pl.BlockSpec((ts, LANES), lambda b, i: (b * ns + i, 0)),
            pl.BlockSpec((LANES, ts), lambda b, i: (0, b * ns + i)),
        ),
        scratch_shapes=[pltpu.VMEM((ts + SUBLANES, c), F32)],
        compiler_params=_params("parallel", "arbitrary"),
        name="gdn_prep",
    )(proj, proj, proj, conv_w, alog_p, dtb_p)


def _dot1(a, b):
    return _dot(a.astype(BF16), b.astype(BF16))


def _each(fn, *lists):
    return [fn(*args) for args in zip(*lists)]


def _unit_lower_inverses(mats, row, col):
    base_log2 = 4
    eye = jnp.where(row == col, 1.0, 0.0)
    in_block = (row >> base_log2) == (col >> base_log2)
    d = [jnp.where(in_block, a, 0.0) for a in mats]
    x = [eye - dh for dh in d]
    p = _each(_dot1, d, d)
    for step in range(base_log2 - 1):
        x = _each(lambda xh, th: xh + th, x, _each(_dot1, x, p))
        if step < base_log2 - 2:
            p = _each(_dot1, p, p)
    sh = base_log2
    while (1 << sh) < GDN_CHUNK:
        same_parent = (row >> (sh + 1)) == (col >> (sh + 1))
        same_block = (row >> sh) == (col >> sh)
        quad = [jnp.where(same_block, 0.0, jnp.where(same_parent, a, 0.0)) for a in mats]
        x = _each(lambda xh, th: xh - th, x, _each(_dot1, x, _each(_dot1, quad, x)))
        sh += 1
    return x


def _gdn_chunk_body(q_ref, k_ref, v_ref, z_ref, gcol_ref, grow_ref, ng_ref, o_ref, state_ref):
    @pl.when(pl.program_id(1) == 0)
    def _():
        state_ref[...] = jnp.zeros_like(state_ref)

    c = GDN_CHUNK
    heads = range(B_HEADS)
    span = lambda h: slice(h * B_HEAD_DIM, (h + 1) * B_HEAD_DIM)
    row = lax.broadcasted_iota(jnp.int32, (c, c), 0)
    col = lax.broadcasted_iota(jnp.int32, (c, c), 1)
    gcol = gcol_ref[...]
    grow = grow_ref[...]
    e_cum = jnp.exp(gcol)
    e_rest = jnp.exp(gcol[c - 1:c, :] - gcol)
    e_last = jnp.exp(grow[:, c - 1:c])

    q = [q_ref[:, span(h)] for h in heads]
    k = [k_ref[:, span(h)] for h in heads]
    v = [v_ref[:, span(h)] for h in heads]
    beta = [gcol[:, SMALL_B + h:SMALL_B + h + 1] for h in heads]
    decay = [jnp.exp(jnp.where(row >= col,
                               gcol[:, SMALL_A + h:SMALL_A + h + 1] - grow[SMALL_A + h:SMALL_A + h + 1, :],
                               -jnp.inf)) for h in heads]
    k_beta = _each(lambda kh, bh: kh * bh, k, beta)
    k16 = [kh.astype(BF16) for kh in k]
    kk = _each(_dot_nt, [kb.astype(BF16) for kb in k_beta], k16)
    qk = _each(_dot_nt, [qh.astype(BF16) for qh in q], k16)
    a_mat = _each(lambda m, dh: jnp.where(row > col, m * dh, 0.0), kk, decay)
    attn = _each(lambda m, dh: (m * dh).astype(BF16), qk, decay)
    t_inv = _unit_lower_inverses(a_mat, row, col)
    rhs = [jnp.concatenate([v[h] * beta[h], k_beta[h] * e_cum[:, SMALL_A + h:SMALL_A + h + 1]], axis=1)
           for h in heads]
    sol = _each(_dot1, t_inv, rhs)
    q_dec = [(q[h] * e_cum[:, SMALL_A + h:SMALL_A + h + 1]).astype(BF16) for h in heads]
    k_dec_t = [(k[h] * e_rest[:, SMALL_A + h:SMALL_A + h + 1]).T.astype(BF16) for h in heads]

    state = [state_ref[h] for h in heads]
    s16 = [s.astype(BF16) for s in state]
    v_new = [sol[h][:, :B_HEAD_DIM] - _dot(sol[h][:, B_HEAD_DIM:].astype(BF16), s16[h]) for h in heads]
    v16 = [x.astype(BF16) for x in v_new]
    for h in heads:
        state_ref[h] = state[h] * e_last[SMALL_A + h:SMALL_A + h + 1, :] + _dot(k_dec_t[h], v16[h])
    out = [_dot(q_dec[h], s16[h]) + _dot(attn[h], v16[h]) for h in heads]
    for h in heads:
        o = out[h]
        y = o * lax.rsqrt(jnp.mean(o * o, axis=-1, keepdims=True) + EPS) * ng_ref[...]
        zh = z_ref[:, span(h)]
        o_ref[:, span(h)] = (y * (zh * _sigmoid(zh))).astype(BF16)


def _gdn_chunk(q, k, v, proj, gcol, grow, ng, *, bsz, seq):
    c = GDN_CHUNK
    nc = seq // c
    tok = lambda b, i: (b * nc + i, 0)
    return pl.pallas_call(
        _gdn_chunk_body,
        out_shape=jax.ShapeDtypeStruct((bsz * seq, B_WIDTH), BF16),
        grid=(bsz, nc),
        in_specs=[
            pl.BlockSpec((c, B_WIDTH), tok),
            pl.BlockSpec((c, B_WIDTH), tok),
            pl.BlockSpec((c, B_WIDTH), tok),
            pl.BlockSpec((c, B_WIDTH), lambda b, i: (b * nc + i, COL_Z // B_WIDTH)),
            pl.BlockSpec((c, LANES), tok),
            pl.BlockSpec((LANES, c), lambda b, i: (0, b * nc + i)),
            pl.BlockSpec((1, B_HEAD_DIM), lambda b, i: (0, 0)),
        ],
        out_specs=pl.BlockSpec((c, B_WIDTH), tok),
        scratch_shapes=[pltpu.VMEM((B_HEADS, B_HEAD_DIM, B_HEAD_DIM), F32)],
        compiler_params=_params("parallel", "arbitrary"),
        name="gdn_chunk",
    )(q, k, v, proj, gcol, grow, ng)


def _out_ln_body(h_ref, oa_ref, ob_ref, wa_ref, wb_ref, g_ref, b_ref, o_ref, *, sub):
    spans = [slice(r, r + sub) for r in range(0, h_ref.shape[0], sub)]
    mixes = [_dot(oa_ref[s, :], wa_ref[...]) + _dot(ob_ref[s, :], wb_ref[...]) for s in spans]
    for s, mix in zip(spans, mixes):
        o_ref[s, :] = _layer_norm(DEEPNORM_ALPHA * h_ref[s, :] + mix, g_ref[...], b_ref[...])


def _out_ln(h, oa, ob, w, g, b, *, tm=512, sub=256):
    t, d = h.shape
    da, db = oa.shape[1], ob.shape[1]
    assert da == db and w.shape == (da + db, d)
    return pl.pallas_call(
        functools.partial(_out_ln_body, sub=sub),
        out_shape=jax.ShapeDtypeStruct((t, d), F32),
        grid=(t // tm,),
        in_specs=[
            pl.BlockSpec((tm, d), lambda i: (i, 0)),
            pl.BlockSpec((tm, da), lambda i: (i, 0)),
            pl.BlockSpec((tm, db), lambda i: (i, 0)),
            pl.BlockSpec((da, d), lambda i: (0, 0)),
            pl.BlockSpec((db, d), lambda i: (1, 0)),
            pl.BlockSpec((1, d), lambda i: (0, 0)),
            pl.BlockSpec((1, d), lambda i: (0, 0)),
        ],
        out_specs=pl.BlockSpec((tm, d), lambda i: (i, 0)),
        compiler_params=_params("parallel"),
        name="out_ln",
    )(h, oa, ob, w, w, g, b)


def _regroup_w_in(w_in):
    cuts = [0, Q_LORA, KV_LORA, IDX_DIM, IDX_HEADS, B_WIDTH, B_WIDTH, B_WIDTH, B_WIDTH, B_HEADS, B_HEADS]
    offs = [sum(cuts[:n + 1]) for n in range(len(cuts))]
    w16 = w_in.astype(BF16)
    part = lambda lo, hi: w16[:, offs[lo]:offs[hi]]
    pad = jnp.zeros((w_in.shape[0], LANES - IDX_HEADS - 2 * B_HEADS), BF16)
    small = jnp.concatenate([part(0, 3), part(3, 4), part(8, 10), pad], axis=1)
    return part(4, 8), small


def _lane_pad(vec, offset):
    out = jnp.zeros((1, LANES), F32)
    return out.at[0, offset:offset + vec.shape[0]].set(vec.astype(F32))


def kernel(x, ffn1_w_gate, ffn1_w_up, ffn1_w_down, ln1_g, ln1_b, w_in, q_norm_g, kv_norm_g, w_uq, w_uk, w_uv, w_q_idx, k_idx_ln_g, k_idx_ln_b, conv_w, a_log, dt_bias, o_norm_g, w_out, ln2_g, ln2_b, ffn2_w_gate, ffn2_w_up, ffn2_w_down, ln3_g, ln3_b):
    bsz, seq, d = x.shape
    assert seq % S_TILE == 0 and seq % GDN_CHUNK == 0 and x.dtype == F32
    t = bsz * seq
    row = lambda p: p.reshape(1, -1).astype(F32)
    h = x.reshape(t, d)
    for l in range(DEPTH):
        h = _ffn_ln(h, ffn1_w_gate[l], ffn1_w_up[l], ffn1_w_down[l], row(ln1_g[l]), row(ln1_b[l]))
        proj = _in_proj(h, _regroup_w_in(w_in[l]))
        qlat, qidx, ckv, ckvt, kidx, wt = _dsa_prep(
            proj, row(q_norm_g[l]), row(kv_norm_g[l]),
            w_uq[l].reshape(Q_LORA, A_HEADS * A_HEAD_DIM).astype(BF16), w_uk[l].astype(BF16),
            w_q_idx[l].astype(BF16), row(k_idx_ln_g[l]), row(k_idx_ln_b[l]))
        o_a = _dsa_attn(qlat, qidx, wt, kidx, ckv, ckvt, w_uv[l].astype(BF16), bsz=bsz, seq=seq)
        gq, gk, gv, gcol, grow = _gdn_prep(proj, conv_w[l].astype(F32), _lane_pad(a_log[l], SMALL_A),
                                           _lane_pad(dt_bias[l], SMALL_A), bsz=bsz, seq=seq)
        o_b = _gdn_chunk(gq, gk, gv, proj, gcol, grow, row(o_norm_g[l]), bsz=bsz, seq=seq)
        h = _out_ln(h, o_a, o_b, w_out[l].astype(BF16), row(ln2_g[l]), row(ln2_b[l]))
        h = _ffn_ln(h, ffn2_w_gate[l], ffn2_w_up[l], ffn2_w_down[l], row(ln3_g[l]), row(ln3_b[l]))
    return h.reshape(bsz, seq, d)
```

```python
import functools
import math

import jax
import jax.numpy as jnp
from jax import lax
from jax.experimental import pallas as pl
from jax.experimental.pallas import tpu as pltpu

F32 = jnp.float32
BF16 = jnp.bfloat16

A_HEADS = 8
A_HEAD_DIM = 128
Q_LORA = 512
KV_LORA = 256
IDX_HEADS = 16
IDX_DIM = 128
TOPK_MAX = 256
B_HEADS = 8
B_HEAD_DIM = 128
B_WIDTH = B_HEADS * B_HEAD_DIM
CONV_WIDTH = 4
DEPTH = 1
DEEPNORM_ALPHA = (2.0 * DEPTH) ** 0.25
EPS = 1e-6

LANES = 128
SUBLANES = 8
BF16_ROWS = 16
VMEM_LIMIT_BYTES = 62 * 1024 * 1024

COL_QKV = 0
COL_Z = 3 * B_WIDTH
COL_CQ = COL_Z + B_WIDTH
COL_CKV = COL_CQ + Q_LORA
COL_KIDX = COL_CKV + KV_LORA
COL_SMALL = COL_KIDX + IDX_DIM
D_IN_PAD = COL_SMALL + LANES
SMALL_W = 0
SMALL_A = IDX_HEADS
SMALL_B = IDX_HEADS + B_HEADS

Q_BLOCK = 256
S_TILE = 512
IDX_GROUP = 1
ATT_GROUP = 1
GDN_CHUNK = 128
INT_MIN = -2 ** 31
F32_LOWEST = -3.4028234663852886e38
LOG2E = math.log2(math.e)


def _dot(a, b):
    return jnp.dot(a, b, preferred_element_type=F32)


def _dot_nt(a, b):
    return lax.dot_general(a, b, (((1,), (1,)), ((), ())), preferred_element_type=F32)


def _sigmoid(x):
    return 1.0 / (1.0 + jnp.exp(-x))


def _layer_norm(y, g, b):
    mu = jnp.mean(y, axis=-1, keepdims=True)
    yc = y - mu
    var = jnp.mean(yc * yc, axis=-1, keepdims=True)
    return yc * lax.rsqrt(var + EPS) * g + b


def _params(*sem):
    return pltpu.CompilerParams(dimension_semantics=sem, vmem_limit_bytes=VMEM_LIMIT_BYTES)


def _ffn_ln_body(x_ref, wg_ref, wu_ref, wd_ref, g_ref, b_ref, o_ref, xb_ref):
    j = pl.program_id(1)

    @pl.when(j == 0)
    def _():
        xb_ref[...] = x_ref[...].astype(BF16)
        o_ref[...] = jnp.zeros_like(o_ref)

    xb = xb_ref[...]
    gate = _dot(xb, wg_ref[...].astype(BF16))
    up = _dot(xb, wu_ref[...].astype(BF16))
    act = gate * _sigmoid(gate) * up
    o_ref[...] += _dot(act.astype(BF16), wd_ref[...].astype(BF16))

    @pl.when(j == pl.num_programs(1) - 1)
    def _():
        y = DEEPNORM_ALPHA * x_ref[...] + 0.5 * o_ref[...]
        o_ref[...] = _layer_norm(y, g_ref[...], b_ref[...])


def _ffn_ln(x, wg, wu, wd, g, b, *, tm=1024, tf=256):
    t, d = x.shape
    f = wg.shape[1]
    return pl.pallas_call(
        _ffn_ln_body,
        out_shape=jax.ShapeDtypeStruct((t, d), F32),
        grid=(t // tm, f // tf),
        in_specs=[
            pl.BlockSpec((tm, d), lambda i, j: (i, 0)),
            pl.BlockSpec((d, tf), lambda i, j: (0, j)),
            pl.BlockSpec((d, tf), lambda i, j: (0, j)),
            pl.BlockSpec((tf, d), lambda i, j: (j, 0)),
            pl.BlockSpec((1, d), lambda i, j: (0, 0)),
            pl.BlockSpec((1, d), lambda i, j: (0, 0)),
        ],
        out_specs=pl.BlockSpec((tm, d), lambda i, j: (i, 0)),
        scratch_shapes=[pltpu.VMEM((tm, d), BF16)],
        compiler_params=_params("parallel", "arbitrary"),
        name="ffn_ln",
    )(x, wg, wu, wd, g, b)


def _in_proj_body(x_ref, *refs, tn):
    w_refs, o_ref = refs[:-1], refs[-1]
    xb = x_ref[...].astype(BF16)
    col = 0
    for w_ref in w_refs:
        for c in range(0, w_ref.shape[1], tn):
            o_ref[:, col + c:col + c + tn] = _dot(xb, w_ref[:, c:c + tn])
        col += w_ref.shape[1]


def _in_proj(x, weights, *, tm=512, tn=1024):
    t, d = x.shape
    n = sum(w.shape[1] for w in weights)
    return pl.pallas_call(
        functools.partial(_in_proj_body, tn=tn),
        out_shape=jax.ShapeDtypeStruct((t, n), F32),
        grid=(t // tm,),
        in_specs=[pl.BlockSpec((tm, d), lambda i: (i, 0))] + [
            pl.BlockSpec(w.shape, lambda i: (0, 0), pipeline_mode=pl.Buffered(1)) for w in weights],
        out_specs=pl.BlockSpec((tm, n), lambda i: (i, 0)),
        compiler_params=_params("parallel"),
        name="in_proj",
    )(x, *weights)


def _dsa_prep_body(cq_ref, ckv_ref, kidx_ref, small_ref, qg_ref, kvg_ref, wuq_ref, wuk_ref,
                   wqi_ref, lng_ref, lnb_ref,
                   qlat_ref, qidx_ref, ckv_o_ref, ckvt_o_ref, kidx_o_ref, wt_o_ref):
    tm = cq_ref.shape[0]
    nb = tm // Q_BLOCK

    cq = cq_ref[...]
    cq = cq * lax.rsqrt(jnp.mean(cq * cq, axis=-1, keepdims=True) + EPS) * qg_ref[...]
    cqb = cq.astype(BF16)

    q = _dot(cqb, wuq_ref[...]).astype(BF16)
    for h in range(A_HEADS):
        ql = _dot(q[:, h * A_HEAD_DIM:(h + 1) * A_HEAD_DIM], wuk_ref[h])
        ql = ql * (A_HEAD_DIM ** -0.5 * LOG2E)
        qlat_ref[:, h] = ql.astype(BF16).reshape(nb, Q_BLOCK, KV_LORA)

    qi = _dot(cqb, wqi_ref[...]).astype(BF16)
    for h in range(IDX_HEADS):
        qidx_ref[:, h] = qi[:, h * IDX_DIM:(h + 1) * IDX_DIM].reshape(nb, Q_BLOCK, IDX_DIM)

    ckv = ckv_ref[...]
    ckv = ckv * lax.rsqrt(jnp.mean(ckv * ckv, axis=-1, keepdims=True) + EPS) * kvg_ref[...]
    ckv_o_ref[...] = ckv.astype(BF16)
    ckvt_o_ref[0] = ckv.T.astype(BF16)

    kidx_o_ref[...] = _layer_norm(kidx_ref[...], lng_ref[...], lnb_ref[...]).astype(BF16)

    wt = small_ref[...].T
    wt_o_ref[...] = wt[SMALL_W:SMALL_W + IDX_HEADS, :] * (IDX_HEADS ** -0.5 * IDX_DIM ** -0.5)


def _dsa_prep(proj, qg, kvg, wuq, wuk, wqi, lng, lnb, *, tm=S_TILE):
    t = proj.shape[0]
    nq = t // Q_BLOCK
    full = lambda *shape: pl.BlockSpec(shape, lambda i: (0,) * len(shape))
    return pl.pallas_call(
        _dsa_prep_body,
        out_shape=(
            jax.ShapeDtypeStruct((nq, A_HEADS, Q_BLOCK, KV_LORA), BF16),
            jax.ShapeDtypeStruct((nq, IDX_HEADS, Q_BLOCK, IDX_DIM), BF16),
            jax.ShapeDtypeStruct((t, KV_LORA), BF16),
            jax.ShapeDtypeStruct((t // tm, KV_LORA, tm), BF16),
            jax.ShapeDtypeStruct((t, IDX_DIM), BF16),
            jax.ShapeDtypeStruct((IDX_HEADS, t), F32),
        ),
        grid=(t // tm,),
        in_specs=[
            pl.BlockSpec((tm, Q_LORA), lambda i: (i, COL_CQ // Q_LORA)),
            pl.BlockSpec((tm, KV_LORA), lambda i: (i, COL_CKV // KV_LORA)),
            pl.BlockSpec((tm, IDX_DIM), lambda i: (i, COL_KIDX // IDX_DIM)),
            pl.BlockSpec((tm, LANES), lambda i: (i, COL_SMALL // LANES)),
            full(1, Q_LORA), full(1, KV_LORA),
            full(Q_LORA, A_HEADS * A_HEAD_DIM),
            full(A_HEADS, A_HEAD_DIM, KV_LORA),
            full(Q_LORA, IDX_HEADS * IDX_DIM),
            full(1, IDX_DIM), full(1, IDX_DIM),
        ],
        out_specs=(
            pl.BlockSpec((tm // Q_BLOCK, A_HEADS, Q_BLOCK, KV_LORA), lambda i: (i, 0, 0, 0)),
            pl.BlockSpec((tm // Q_BLOCK, IDX_HEADS, Q_BLOCK, IDX_DIM), lambda i: (i, 0, 0, 0)),
            pl.BlockSpec((tm, KV_LORA), lambda i: (i, 0)),
            pl.BlockSpec((1, KV_LORA, tm), lambda i: (i, 0, 0)),
            pl.BlockSpec((tm, IDX_DIM), lambda i: (i, 0)),
            pl.BlockSpec((IDX_HEADS, tm), lambda i: (0, i)),
        ),
        compiler_params=_params("parallel"),
        name="dsa_prep",
    )(proj, proj, proj, proj, qg, kvg, wuq, wuk, wqi, lng, lnb)


def _key_to_f32(key):
    return pltpu.bitcast(key ^ ((key >> 31) & 0x7FFFFFFF), F32)


def _dsa_attn_body(qidx_ref, qlat_ref, wt_ref, kidx_ref, ckv_ref, ckvt_ref, wuv_ref, o_ref,
                   sc_ref, sh_ref, bias_ref, m_ref, l_ref, acc_ref, *, n_sel):
    i = pl.program_id(1)
    t0 = i * Q_BLOCK
    n_tiles = (t0 + Q_BLOCK + S_TILE - 1) // S_TILE

    t_lane = t0 + lax.broadcasted_iota(jnp.int32, (S_TILE, Q_BLOCK), 1)
    s_row = lax.broadcasted_iota(jnp.int32, (S_TILE, Q_BLOCK), 0)

    def tile_rows(j):
        return pl.ds(pl.multiple_of(j * S_TILE, S_TILE), S_TILE)

    wt = wt_ref[...]

    def score_tile(j, carry):
        k_tile = kidx_ref[tile_rows(j), :]
        sc = jnp.zeros((S_TILE, Q_BLOCK), F32)
        for g in range(IDX_HEADS // IDX_GROUP):
            qg = qidx_ref[0, g * IDX_GROUP:(g + 1) * IDX_GROUP].reshape(IDX_GROUP * Q_BLOCK, IDX_DIM)
            logits = _dot_nt(k_tile, qg)
            for u in range(IDX_GROUP):
                h = g * IDX_GROUP + u
                sc = sc + wt[h:h + 1, :] * jnp.maximum(logits[:, u * Q_BLOCK:(u + 1) * Q_BLOCK], 0.0)
        sc = jnp.where(s_row + j * S_TILE <= t_lane, sc, -jnp.inf)
        sc_ref[tile_rows(j), :] = sc
        sh_ref[tile_rows(j), :] = sc.astype(BF16)
        return carry

    lax.fori_loop(0, n_tiles, score_tile, 0)

    def tree_sum(parts, chains=8):
        sums = list(parts[:chains])
        for n, piece in enumerate(parts[chains:]):
            sums[n % chains] = sums[n % chains] + piece
        while len(sums) > 1:
            sums = [a + b for a, b in zip(sums[0::2], sums[1::2])]
        return sums[0]

    def count(pred):
        def body(j, acc):
            hit = jnp.where(pred(sc_ref[tile_rows(j), :]), 1, 0)
            return acc + tree_sum([hit[r:r + SUBLANES, :] for r in range(0, S_TILE, SUBLANES)])

        acc = lax.fori_loop(0, n_tiles, body, jnp.zeros((SUBLANES, Q_BLOCK), jnp.int32))
        return jnp.sum(acc, axis=0, keepdims=True)

    one16, zero16 = jnp.ones((), BF16), jnp.zeros((), BF16)

    def count_rounded_ge(key16):
        cand = _key_to_f32((key16 << 16) | ((key16 >> 31) & 0xFFFF)).astype(BF16)

        def body(j, acc):
            hit = jnp.where(sh_ref[tile_rows(j), :] >= cand, one16, zero16)
            return acc + tree_sum([hit[r:r + BF16_ROWS, :] for r in range(0, S_TILE, BF16_ROWS)])

        acc = lax.fori_loop(0, n_tiles, body, jnp.zeros((BF16_ROWS, Q_BLOCK), BF16))
        return jnp.sum(acc.astype(F32), axis=0, keepdims=True).astype(jnp.int32)

    def try_key16(cand16, key16):
        return jnp.where(count_rounded_ge(cand16) >= n_sel, cand16, key16)

    zero = jnp.zeros((1, Q_BLOCK), jnp.int32)
    key16 = try_key16(zero, zero - 2 ** 15)
    key16 = lax.fori_loop(0, 15, lambda b, k: try_key16(k | (1 << (14 - b)), k), key16)

    coarse = (key16 << 16) | ((key16 >> 31) & 0xFFFF)
    base = coarse - 2 ** 15

    def try_offset(cand_off, off, key, n_ge):
        cand_key = base + cand_off
        cand = _key_to_f32(cand_key)
        n = count(lambda sc: sc >= cand)
        ok = n >= n_sel
        return jnp.where(ok, cand_off, off), jnp.where(ok, cand_key, key), jnp.where(ok, n, n_ge)

    carry = try_offset(zero, zero, zero + INT_MIN, zero)
    _, key, n_ge = lax.fori_loop(0, 17, lambda b, c: try_offset(c[0] | (1 << (16 - b)), *c), carry)
    thr = _key_to_f32(key)
    thr = jnp.maximum(jnp.where(thr != thr, F32_LOWEST, thr), F32_LOWEST)

    @pl.when(jnp.max(n_ge) > n_sel)
    def _():
        need = (n_sel - count(lambda sc: sc > thr)).astype(F32)
        r = lax.broadcasted_iota(jnp.int32, (S_TILE, S_TILE), 0)
        c = lax.broadcasted_iota(jnp.int32, (S_TILE, S_TILE), 1)
        lower_incl = jnp.where(c <= r, 1.0, 0.0).astype(BF16)

        def body(j, run):
            sc = sc_ref[tile_rows(j), :]
            tied = sc == thr
            rank = run + _dot(lower_incl, jnp.where(tied, 1.0, 0.0).astype(BF16))
            sc_ref[tile_rows(j), :] = jnp.where(tied & (rank > need), -jnp.inf, sc)
            return rank[S_TILE - 1:S_TILE, :]

        lax.fori_loop(0, n_tiles, body, jnp.zeros((1, Q_BLOCK), F32))

    gw = ATT_GROUP * Q_BLOCK
    groups = range(A_HEADS // ATT_GROUP)
    lane = lax.broadcasted_iota(jnp.int32, (1, gw), 1)
    slope_rows = []
    for g in groups:
        row_g = jnp.zeros((1, gw), F32)
        for u in range(ATT_GROUP):
            slope = 2.0 ** (-8.0 * (g * ATT_GROUP + u + 1) / A_HEADS) * LOG2E
            row_g = jnp.where(lane >= u * Q_BLOCK, slope, row_g)
        slope_rows.append(row_g)

    @pl.when(i == 0)
    def _():
        s_rel = lax.broadcasted_iota(jnp.int32, (S_TILE, gw), 0).astype(F32)
        for g in groups:
            bias_ref[g] = slope_rows[g] * s_rel

    m_ref[...] = jnp.full(m_ref.shape, -1e30, F32)
    l_ref[...] = jnp.zeros_like(l_ref)
    acc_ref[...] = jnp.zeros_like(acc_ref)

    def att_tile(j, carry):
        drop = jnp.where(sc_ref[tile_rows(j), :] >= thr, 0.0, -jnp.inf)
        drop = jnp.concatenate([drop] * ATT_GROUP, axis=1)
        ckv_tile = ckv_ref[tile_rows(j), :]
        ckvt_tile = ckvt_ref[j]
        off = (j * S_TILE - t0).astype(F32)
        logits = [_dot_nt(ckv_tile, qlat_ref[0, g * ATT_GROUP:(g + 1) * ATT_GROUP].reshape(gw, KV_LORA))
                  for g in groups]
        probs, alphas = [], []
        for g in groups:
            a = logits[g] + bias_ref[g] + drop
            shift = slope_rows[g] * off
            m_old = m_ref[g:g + 1, :]
            m_new = jnp.maximum(m_old, jnp.max(a, axis=0, keepdims=True) + shift)
            alpha = jnp.exp2(m_old - m_new)
            p = jnp.exp2(a - (m_new - shift))
            l_ref[g:g + 1, :] = alpha * l_ref[g:g + 1, :] + jnp.sum(p, axis=0, keepdims=True)
            m_ref[g:g + 1, :] = m_new
            probs.append(p.astype(BF16))
            alphas.append(alpha)
        for g in groups:
            acc_ref[g] = alphas[g] * acc_ref[g] + _dot(ckvt_tile, probs[g])
        return carry

    lax.fori_loop(0, n_tiles, att_tile, 0)

    for g in groups:
        o_lat_t = acc_ref[g] * (1.0 / l_ref[g:g + 1, :])
        for u in range(ATT_GROUP):
            h = g * ATT_GROUP + u
            o_lat = o_lat_t[:, u * Q_BLOCK:(u + 1) * Q_BLOCK].T.astype(BF16)
            o_ref[:, h * A_HEAD_DIM:(h + 1) * A_HEAD_DIM] = _dot(o_lat, wuv_ref[h]).astype(BF16)


def _dsa_attn(qlat, qidx, wt, kidx, ckv, ckvt, wuv, *, bsz, seq):
    nq = seq // Q_BLOCK
    ns = seq // S_TILE
    n_sel = min(TOPK_MAX, seq // 4)
    ng, gw = A_HEADS // ATT_GROUP, ATT_GROUP * Q_BLOCK
    return pl.pallas_call(
        functools.partial(_dsa_attn_body, n_sel=n_sel),
        out_shape=jax.ShapeDtypeStruct((bsz * seq, A_HEADS * A_HEAD_DIM), BF16),
        grid=(bsz, nq),
        in_specs=[
            pl.BlockSpec((1, IDX_HEADS, Q_BLOCK, IDX_DIM), lambda b, i: (b * nq + i, 0, 0, 0)),
            pl.BlockSpec((1, A_HEADS, Q_BLOCK, KV_LORA), lambda b, i: (b * nq + i, 0, 0, 0)),
            pl.BlockSpec((IDX_HEADS, Q_BLOCK), lambda b, i: (0, b * nq + i)),
            pl.BlockSpec((seq, IDX_DIM), lambda b, i: (b, 0)),
            pl.BlockSpec((seq, KV_LORA), lambda b, i: (b, 0)),
            pl.BlockSpec((ns, KV_LORA, S_TILE), lambda b, i: (b, 0, 0)),
            pl.BlockSpec((A_HEADS, KV_LORA, A_HEAD_DIM), lambda b, i: (0, 0, 0)),
        ],
        out_specs=pl.BlockSpec((Q_BLOCK, A_HEADS * A_HEAD_DIM), lambda b, i: (b * nq + i, 0)),
        scratch_shapes=[
            pltpu.VMEM((seq, Q_BLOCK), F32),
            pltpu.VMEM((seq, Q_BLOCK), BF16),
            pltpu.VMEM((ng, S_TILE, gw), F32),
            pltpu.VMEM((ng, gw), F32),
            pltpu.VMEM((ng, gw), F32),
            pltpu.VMEM((ng, KV_LORA, gw), F32),
        ],
        compiler_params=_params("parallel", "arbitrary"),
        name="dsa_attn",
    )(qidx, qlat, wt, kidx, ckv, ckvt, wuv)


def _gdn_prep_body(x_ref, halo_ref, small_ref, cw_ref, alog_ref, dtb_ref,
                   q_ref, k_ref, v_ref, gcol_ref, grow_ref, xx_ref):
    ts = x_ref.shape[0]
    first = pl.program_id(1) == 0
    halo = halo_ref[...]
    xx_ref[0:SUBLANES, :] = jnp.where(first, jnp.zeros_like(halo), halo)
    xx_ref[SUBLANES:, :] = x_ref[...]
    cw = cw_ref[...]
    y = jnp.zeros(x_ref.shape, F32)
    for tap in range(CONV_WIDTH):
        off = SUBLANES - (CONV_WIDTH - 1) + tap
        y = y + cw[tap:tap + 1, :] * xx_ref[off:off + ts, :]
    y = y * _sigmoid(y)

    for h in range(B_HEADS):
        lo, hi = h * B_HEAD_DIM, (h + 1) * B_HEAD_DIM
        qh = y[:, lo:hi]
        q_ref[:, lo:hi] = qh * lax.rsqrt(jnp.sum(qh * qh, axis=-1, keepdims=True) + EPS) * (
            B_HEAD_DIM ** -0.5)
        kh = y[:, B_WIDTH + lo:B_WIDTH + hi]
        k_ref[:, lo:hi] = kh * lax.rsqrt(jnp.sum(kh * kh, axis=-1, keepdims=True) + EPS)
    v_ref[...] = y[:, 2 * B_WIDTH:]

    small = small_ref[...]
    pre = small + dtb_ref[...]
    softplus = jnp.maximum(pre, 0.0) + jnp.log1p(jnp.exp(-jnp.abs(pre)))
    g = -jnp.exp(alog_ref[...]) * softplus
    row = lax.broadcasted_iota(jnp.int32, g.shape, 0) & (GDN_CHUNK - 1)
    shift = 1
    while shift < GDN_CHUNK:
        g = g + jnp.where(row >= shift, pltpu.roll(g, shift, 0), 0.0)
        shift *= 2
    lane = lax.broadcasted_iota(jnp.int32, g.shape, 1)
    is_a = (lane >= SMALL_A) & (lane < SMALL_A + B_HEADS)
    gb = jnp.where(is_a, g, _sigmoid(small))
    gcol_ref[...] = gb
    grow_ref[0] = gb.T


def _gdn_prep(proj, conv_w, alog_p, dtb_p, *, bsz, seq, ts=256):
    t = bsz * seq
    ns = seq // ts
    c = 3 * B_WIDTH
    return pl.pallas_call(
        _gdn_prep_body,
        out_shape=(
            jax.ShapeDtypeStruct((t, B_WIDTH), F32),
            jax.ShapeDtypeStruct((t, B_WIDTH), F32),
            jax.ShapeDtypeStruct((t, B_WIDTH), F32),
            jax.ShapeDtypeStruct((t, LANES), F32),
            jax.ShapeDtypeStruct((bsz, LANES, seq), F32),
        ),
        grid=(bsz, ns),
        in_specs=[
            pl.BlockSpec((ts, c), lambda b, i: (b * ns + i, COL_QKV // c)),
            pl.BlockSpec((SUBLANES, c),
                         lambda b, i: (jnp.maximum((b * ns + i) * (ts // SUBLANES) - 1, 0), 0)),
            pl.BlockSpec((ts, LANES), lambda b, i: (b * ns + i, COL_SMALL // LANES)),
            pl.BlockSpec((CONV_WIDTH, c), lambda b, i: (0, 0)),
            pl.BlockSpec((1, LANES), lambda b, i: (0, 0)),
            pl.BlockSpec((1, LANES), lambda b, i: (0, 0)),
        ],
        out_specs=(
            pl.BlockSpec((ts, B_WIDTH), lambda b, i: (b * ns + i, 0)),
            pl.BlockSpec((ts, B_WIDTH), lambda b, i: (b * ns + i, 0)),
            pl.BlockSpec((ts, B_WIDTH), lambda b, i: (b * ns + i, 0)),
            pl.BlockSpec((ts, LANES), lambda b, i: (b * ns + i, 0)),
            pl.BlockSpec((1, LANES, ts), lambda b, i: (b, 0, i)),
        ),
        scratch_shapes=[pltpu.VMEM((ts + SUBLANES, c), F32)],
        compiler_params=_params("parallel", "arbitrary"),
        name="gdn_prep",
    )(proj, proj, proj, conv_w, alog_p, dtb_p)


def _dot1(a, b):
    return _dot(a.astype(BF16), b.astype(BF16))


def _each(fn, *lists):
    return [fn(*args) for args in zip(*lists)]


def _unit_lower_inverses(mats, row, col):
    base_log2 = 4
    eye = jnp.where(row == col, 1.0, 0.0)
    in_block = (row >> base_log2) == (col >> base_log2)
    d = [jnp.where(in_block, a, 0.0) for a in mats]
    x = [eye - dh for dh in d]
    p = _each(_dot1, d, d)
    for step in range(base_log2 - 1):
        x = _each(lambda xh, th: xh + th, x, _each(_dot1, x, p))
        if step < base_log2 - 2:
            p = _each(_dot1, p, p)
    sh = base_log2
    while (1 << sh) < GDN_CHUNK:
        same_parent = (row >> (sh + 1)) == (col >> (sh + 1))
        same_block = (row >> sh) == (col >> sh)
        quad = [jnp.where(same_block, 0.0, jnp.where(same_parent, a, 0.0)) for a in mats]
        x = _each(lambda xh, th: xh - th, x, _each(_dot1, x, _each(_dot1, quad, x)))
        sh += 1
    return x


def _gdn_chunk_body(q_ref, k_ref, v_ref, z_ref, gcol_ref, grow_ref, ng_ref, o_ref, state_ref):
    @pl.when(pl.program_id(0) == 0)
    def _():
        state_ref[...] = jnp.zeros_like(state_ref)

    c = GDN_CHUNK
    bsz = q_ref.shape[0]
    units = [(b, h) for b in range(bsz) for h in range(B_HEADS)]
    span = lambda h: slice(h * B_HEAD_DIM, (h + 1) * B_HEAD_DIM)
    row = lax.broadcasted_iota(jnp.int32, (c, c), 0)
    col = lax.broadcasted_iota(jnp.int32, (c, c), 1)
    gcol = [gcol_ref[b] for b in range(bsz)]
    grow = [grow_ref[b] for b in range(bsz)]
    e_cum = [jnp.exp(g) for g in gcol]
    e_rest = [jnp.exp(g[c - 1:c, :] - g) for g in gcol]
    e_last = [jnp.exp(g[:, c - 1:c]) for g in grow]
    col_a = lambda tab, b, h: tab[b][:, SMALL_A + h:SMALL_A + h + 1]

    q = [q_ref[b, :, span(h)] for b, h in units]
    k = [k_ref[b, :, span(h)] for b, h in units]
    v = [v_ref[b, :, span(h)] for b, h in units]
    beta = [gcol[b][:, SMALL_B + h:SMALL_B + h + 1] for b, h in units]
    decay = [jnp.exp(jnp.where(row >= col, col_a(gcol, b, h) - grow[b][SMALL_A + h:SMALL_A + h + 1, :],
                               -jnp.inf)) for b, h in units]
    k_beta = _each(lambda kh, bh: kh * bh, k, beta)
    k16 = [kh.astype(BF16) for kh in k]
    kk = _each(_dot_nt, [kb.astype(BF16) for kb in k_beta], k16)
    qk = _each(_dot_nt, [qh.astype(BF16) for qh in q], k16)
    a_mat = _each(lambda m, dh: jnp.where(row > col, m * dh, 0.0), kk, decay)
    attn = _each(lambda m, dh: (m * dh).astype(BF16), qk, decay)
    t_inv = _unit_lower_inverses(a_mat, row, col)
    rhs = [jnp.concatenate([v[u] * beta[u], k_beta[u] * col_a(e_cum, b, h)], axis=1)
           for u, (b, h) in enumerate(units)]
    sol = _each(_dot1, t_inv, rhs)
    q_dec = [(q[u] * col_a(e_cum, b, h)).astype(BF16) for u, (b, h) in enumerate(units)]
    k_dec_t = [(k[u] * col_a(e_rest, b, h)).T.astype(BF16) for u, (b, h) in enumerate(units)]

    state = [state_ref[u] for u in range(len(units))]
    s16 = [s.astype(BF16) for s in state]
    v_new = [sol[u][:, :B_HEAD_DIM] - _dot(sol[u][:, B_HEAD_DIM:].astype(BF16), s16[u])
             for u in range(len(units))]
    v16 = [x.astype(BF16) for x in v_new]
    for u, (b, h) in enumerate(units):
        state_ref[u] = state[u] * e_last[b][SMALL_A + h:SMALL_A + h + 1, :] + _dot(k_dec_t[u], v16[u])
    out = [_dot(q_dec[u], s16[u]) + _dot(attn[u], v16[u]) for u in range(len(units))]
    for u, (b, h) in enumerate(units):
        o = out[u]
        y = o * lax.rsqrt(jnp.mean(o * o, axis=-1, keepdims=True) + EPS) * ng_ref[...]
        zh = z_ref[b, :, span(h)]
        o_ref[b, :, span(h)] = (y * (zh * _sigmoid(zh))).astype(BF16)


def _gdn_chunk(q, k, v, proj, gcol, grow, ng, *, bsz, seq):
    c = GDN_CHUNK
    per_seq = lambda a: a.reshape(bsz, seq, a.shape[-1])
    tok = lambda i: (0, i, 0)
    out = pl.pallas_call(
        _gdn_chunk_body,
        out_shape=jax.ShapeDtypeStruct((bsz, seq, B_WIDTH), BF16),
        grid=(seq // c,),
        in_specs=[
            pl.BlockSpec((bsz, c, B_WIDTH), tok),
            pl.BlockSpec((bsz, c, B_WIDTH), tok),
            pl.BlockSpec((bsz, c, B_WIDTH), tok),
            pl.BlockSpec((bsz, c, B_WIDTH), lambda i: (0, i, COL_Z // B_WIDTH)),
            pl.BlockSpec((bsz, c, LANES), tok),
            pl.BlockSpec((bsz, LANES, c), lambda i: (0, 0, i)),
            pl.BlockSpec((1, B_HEAD_DIM), lambda i: (0, 0)),
        ],
        out_specs=pl.BlockSpec((bsz, c, B_WIDTH), tok),
        scratch_shapes=[pltpu.VMEM((bsz * B_HEADS, B_HEAD_DIM, B_HEAD_DIM), F32)],
        compiler_params=_params("arbitrary"),
        name="gdn_chunk",
    )(per_seq(q), per_seq(k), per_seq(v), per_seq(proj), per_seq(gcol), grow, ng)
    return out.reshape(bsz * seq, B_WIDTH)


def _out_ln_body(h_ref, oa_ref, ob_ref, wa_ref, wb_ref, g_ref, b_ref, o_ref, *, sub):
    spans = [slice(r, r + sub) for r in range(0, h_ref.shape[0], sub)]
    mixes = [_dot(oa_ref[s, :], wa_ref[...]) + _dot(ob_ref[s, :], wb_ref[...]) for s in spans]
    for s, mix in zip(spans, mixes):
        o_ref[s, :] = _layer_norm(DEEPNORM_ALPHA * h_ref[s, :] + mix, g_ref[...], b_ref[...])


def _out_ln(h, oa, ob, w, g, b, *, tm=1024, sub=256):
    t, d = h.shape
    da, db = oa.shape[1], ob.shape[1]
    assert da == db and w.shape == (da + db, d)
    return pl.pallas_call(
        functools.partial(_out_ln_body, sub=sub),
        out_shape=jax.ShapeDtypeStruct((t, d), F32),
        grid=(t // tm,),
        in_specs=[
            pl.BlockSpec((tm, d), lambda i: (i, 0)),
            pl.BlockSpec((tm, da), lambda i: (i, 0)),
            pl.BlockSpec((tm, db), lambda i: (i, 0)),
            pl.BlockSpec((da, d), lambda i: (0, 0), pipeline_mode=pl.Buffered(1)),
            pl.BlockSpec((db, d), lambda i: (1, 0), pipeline_mode=pl.Buffered(1)),
            pl.BlockSpec((1, d), lambda i: (0, 0)),
            pl.BlockSpec((1, d), lambda i: (0, 0)),
        ],
        out_specs=pl.BlockSpec((tm, d), lambda i: (i, 0)),
        compiler_params=_params("parallel"),
        name="out_ln",
    )(h, oa, ob, w, w, g, b)


def _regroup_w_in(w_in):
    cuts = [0, Q_LORA, KV_LORA, IDX_DIM, IDX_HEADS, B_WIDTH, B_WIDTH, B_WIDTH, B_WIDTH, B_HEADS, B_HEADS]
    offs = [sum(cuts[:n + 1]) for n in range(len(cuts))]
    w16 = w_in.astype(BF16)
    part = lambda lo, hi: w16[:, offs[lo]:offs[hi]]
    pad = jnp.zeros((w_in.shape[0], LANES - IDX_HEADS - 2 * B_HEADS), BF16)
    small = jnp.concatenate([part(0, 3), part(3, 4), part(8, 10), pad], axis=1)
    return part(4, 8), small


def _lane_pad(vec, offset):
    out = jnp.zeros((1, LANES), F32)
    return out.at[0, offset:offset + vec.shape[0]].set(vec.astype(F32))


def kernel(x, ffn1_w_gate, ffn1_w_up, ffn1_w_down, ln1_g, ln1_b, w_in, q_norm_g, kv_norm_g, w_uq, w_uk, w_uv, w_q_idx, k_idx_ln_g, k_idx_ln_b, conv_w, a_log, dt_bias, o_norm_g, w_out, ln2_g, ln2_b, ffn2_w_gate, ffn2_w_up, ffn2_w_down, ln3_g, ln3_b):
    bsz, seq, d = x.shape
    assert seq % S_TILE == 0 and seq % GDN_CHUNK == 0 and x.dtype == F32
    assert seq // BF16_ROWS <= 256, "packed bf16 hit counts must stay exactly representable"
    t = bsz * seq
    row = lambda p: p.reshape(1, -1).astype(F32)
    h = x.reshape(t, d)
    for l in range(DEPTH):
        h = _ffn_ln(h, ffn1_w_gate[l], ffn1_w_up[l], ffn1_w_down[l], row(ln1_g[l]), row(ln1_b[l]))
        proj = _in_proj(h, _regroup_w_in(w_in[l]))
        qlat, qidx, ckv, ckvt, kidx, wt = _dsa_prep(
            proj, row(q_norm_g[l]), row(kv_norm_g[l]),
            w_uq[l].reshape(Q_LORA, A_HEADS * A_HEAD_DIM).astype(BF16), w_uk[l].astype(BF16),
            w_q_idx[l].astype(BF16), row(k_idx_ln_g[l]), row(k_idx_ln_b[l]))
        o_a = _dsa_attn(qlat, qidx, wt, kidx, ckv, ckvt, w_uv[l].astype(BF16), bsz=bsz, seq=seq)
        gq, gk, gv, gcol, grow = _gdn_prep(proj, conv_w[l].astype(F32), _lane_pad(a_log[l], SMALL_A),
                                           _lane_pad(dt_bias[l], SMALL_A), bsz=bsz, seq=seq)
        o_b = _gdn_chunk(gq, gk, gv, proj, gcol, grow, row(o_norm_g[l]), bsz=bsz, seq=seq)
        h = _out_ln(h, o_a, o_b, w_out[l].astype(BF16), row(ln2_g[l]), row(ln2_b[l]))
        h = _ffn_ln(h, ffn2_w_gate[l], ffn2_w_up[l], ffn2_w_down[l], row(ln3_g[l]), row(ln3_b[l]))
    return h.reshape(bsz, seq, d)
```

```python
import functools
import math

import jax
import jax.numpy as jnp
from jax import lax
from jax.experimental import pallas as pl
from jax.experimental.pallas import tpu as pltpu

F32 = jnp.float32
BF16 = jnp.bfloat16

A_HEADS = 8
A_HEAD_DIM = 128
Q_LORA = 512
KV_LORA = 256
IDX_HEADS = 16
IDX_DIM = 128
TOPK_MAX = 256
B_HEADS = 8
B_HEAD_DIM = 128
B_WIDTH = B_HEADS * B_HEAD_DIM
CONV_WIDTH = 4
DEPTH = 1
DEEPNORM_ALPHA = (2.0 * DEPTH) ** 0.25
EPS = 1e-6

LANES = 128
SUBLANES = 8
BF16_ROWS = 16
VMEM_LIMIT_BYTES = 62 * 1024 * 1024

COL_QKV = 0
COL_Z = 3 * B_WIDTH
COL_CQ = COL_Z + B_WIDTH
COL_CKV = COL_CQ + Q_LORA
COL_KIDX = COL_CKV + KV_LORA
COL_SMALL = COL_KIDX + IDX_DIM
D_IN_PAD = COL_SMALL + LANES
SMALL_W = 0
SMALL_A = IDX_HEADS
SMALL_B = IDX_HEADS + B_HEADS

Q_BLOCK = 256
S_TILE = 512
IDX_GROUP = 1
ATT_GROUP = 1
GDN_CHUNK = 128
INT_MIN = -2 ** 31
F32_LOWEST = -3.4028234663852886e38
LOG2E = math.log2(math.e)


def _dot(a, b):
    return jnp.dot(a, b, preferred_element_type=F32)


def _dot_nt(a, b):
    return lax.dot_general(a, b, (((1,), (1,)), ((), ())), preferred_element_type=F32)


def _sigmoid(x):
    return 1.0 / (1.0 + jnp.exp(-x))


def _layer_norm(y, g, b):
    mu = jnp.mean(y, axis=-1, keepdims=True)
    yc = y - mu
    var = jnp.mean(yc * yc, axis=-1, keepdims=True)
    return yc * lax.rsqrt(var + EPS) * g + b


def _params(*sem):
    return pltpu.CompilerParams(dimension_semantics=sem, vmem_limit_bytes=VMEM_LIMIT_BYTES)


def _ffn_ln_body(x_ref, wg_ref, wu_ref, wd_ref, g_ref, b_ref, o_ref, xb_ref):
    j = pl.program_id(1)

    @pl.when(j == 0)
    def _():
        xb_ref[...] = x_ref[...].astype(BF16)
        o_ref[...] = jnp.zeros_like(o_ref)

    xb = xb_ref[...]
    gate = _dot(xb, wg_ref[...].astype(BF16))
    up = _dot(xb, wu_ref[...].astype(BF16))
    act = gate * _sigmoid(gate) * up
    o_ref[...] += _dot(act.astype(BF16), wd_ref[...].astype(BF16))

    @pl.when(j == pl.num_programs(1) - 1)
    def _():
        y = DEEPNORM_ALPHA * x_ref[...] + 0.5 * o_ref[...]
        o_ref[...] = _layer_norm(y, g_ref[...], b_ref[...])


def _ffn_ln(x, wg, wu, wd, g, b, *, tm=1024, tf=256):
    t, d = x.shape
    f = wg.shape[1]
    return pl.pallas_call(
        _ffn_ln_body,
        out_shape=jax.ShapeDtypeStruct((t, d), F32),
        grid=(t // tm, f // tf),
        in_specs=[
            pl.BlockSpec((tm, d), lambda i, j: (i, 0)),
            pl.BlockSpec((d, tf), lambda i, j: (0, j)),
            pl.BlockSpec((d, tf), lambda i, j: (0, j)),
            pl.BlockSpec((tf, d), lambda i, j: (j, 0)),
            pl.BlockSpec((1, d), lambda i, j: (0, 0)),
            pl.BlockSpec((1, d), lambda i, j: (0, 0)),
        ],
        out_specs=pl.BlockSpec((tm, d), lambda i, j: (i, 0)),
        scratch_shapes=[pltpu.VMEM((tm, d), BF16)],
        compiler_params=_params("parallel", "arbitrary"),
        name="ffn_ln",
    )(x, wg, wu, wd, g, b)


def _in_proj_body(x_ref, *refs, tn):
    w_refs, o_ref = refs[:-1], refs[-1]
    xb = x_ref[...].astype(BF16)
    col = 0
    for w_ref in w_refs:
        for c in range(0, w_ref.shape[1], tn):
            o_ref[:, col + c:col + c + tn] = _dot(xb, w_ref[:, c:c + tn])
        col += w_ref.shape[1]


def _in_proj(x, weights, *, tm=512, tn=1024):
    t, d = x.shape
    n = sum(w.shape[1] for w in weights)
    return pl.pallas_call(
        functools.partial(_in_proj_body, tn=tn),
        out_shape=jax.ShapeDtypeStruct((t, n), F32),
        grid=(t // tm,),
        in_specs=[pl.BlockSpec((tm, d), lambda i: (i, 0))] + [
            pl.BlockSpec(w.shape, lambda i: (0, 0), pipeline_mode=pl.Buffered(1)) for w in weights],
        out_specs=pl.BlockSpec((tm, n), lambda i: (i, 0)),
        compiler_params=_params("parallel"),
        name="in_proj",
    )(x, *weights)


def _dsa_prep_body(cq_ref, ckv_ref, kidx_ref, small_ref, qg_ref, kvg_ref, wuq_ref, wuk_ref,
                   wqi_ref, lng_ref, lnb_ref,
                   qlat_ref, qidx_ref, ckv_o_ref, ckvt_o_ref, kidx_o_ref, wt_o_ref):
    tm = cq_ref.shape[0]
    nb = tm // Q_BLOCK

    cq = cq_ref[...]
    cq = cq * lax.rsqrt(jnp.mean(cq * cq, axis=-1, keepdims=True) + EPS) * qg_ref[...]
    cqb = cq.astype(BF16)

    q = _dot(cqb, wuq_ref[...]).astype(BF16)
    for h in range(A_HEADS):
        ql = _dot(q[:, h * A_HEAD_DIM:(h + 1) * A_HEAD_DIM], wuk_ref[h])
        ql = ql * (A_HEAD_DIM ** -0.5 * LOG2E)
        qlat_ref[:, h] = ql.astype(BF16).reshape(nb, Q_BLOCK, KV_LORA)

    qi = _dot(cqb, wqi_ref[...]).astype(BF16)
    for h in range(IDX_HEADS):
        qidx_ref[:, h] = qi[:, h * IDX_DIM:(h + 1) * IDX_DIM].reshape(nb, Q_BLOCK, IDX_DIM)

    ckv = ckv_ref[...]
    ckv = ckv * lax.rsqrt(jnp.mean(ckv * ckv, axis=-1, keepdims=True) + EPS) * kvg_ref[...]
    ckv_o_ref[...] = ckv.astype(BF16)
    ckvt_o_ref[0] = ckv.T.astype(BF16)

    kidx_o_ref[...] = _layer_norm(kidx_ref[...], lng_ref[...], lnb_ref[...]).astype(BF16)

    wt = small_ref[...].T
    wt_o_ref[...] = wt[SMALL_W:SMALL_W + IDX_HEADS, :] * (IDX_HEADS ** -0.5 * IDX_DIM ** -0.5)


def _dsa_prep(proj, qg, kvg, wuq, wuk, wqi, lng, lnb, *, tm=S_TILE):
    t = proj.shape[0]
    nq = t // Q_BLOCK
    full = lambda *shape: pl.BlockSpec(shape, lambda i: (0,) * len(shape))
    return pl.pallas_call(
        _dsa_prep_body,
        out_shape=(
            jax.ShapeDtypeStruct((nq, A_HEADS, Q_BLOCK, KV_LORA), BF16),
            jax.ShapeDtypeStruct((nq, IDX_HEADS, Q_BLOCK, IDX_DIM), BF16),
            jax.ShapeDtypeStruct((t, KV_LORA), BF16),
            jax.ShapeDtypeStruct((t // tm, KV_LORA, tm), BF16),
            jax.ShapeDtypeStruct((t, IDX_DIM), BF16),
            jax.ShapeDtypeStruct((IDX_HEADS, t), F32),
        ),
        grid=(t // tm,),
        in_specs=[
            pl.BlockSpec((tm, Q_LORA), lambda i: (i, COL_CQ // Q_LORA)),
            pl.BlockSpec((tm, KV_LORA), lambda i: (i, COL_CKV // KV_LORA)),
            pl.BlockSpec((tm, IDX_DIM), lambda i: (i, COL_KIDX // IDX_DIM)),
            pl.BlockSpec((tm, LANES), lambda i: (i, COL_SMALL // LANES)),
            full(1, Q_LORA), full(1, KV_LORA),
            full(Q_LORA, A_HEADS * A_HEAD_DIM),
            full(A_HEADS, A_HEAD_DIM, KV_LORA),
            full(Q_LORA, IDX_HEADS * IDX_DIM),
            full(1, IDX_DIM), full(1, IDX_DIM),
        ],
        out_specs=(
            pl.BlockSpec((tm // Q_BLOCK, A_HEADS, Q_BLOCK, KV_LORA), lambda i: (i, 0, 0, 0)),
            pl.BlockSpec((tm // Q_BLOCK, IDX_HEADS, Q_BLOCK, IDX_DIM), lambda i: (i, 0, 0, 0)),
            pl.BlockSpec((tm, KV_LORA), lambda i: (i, 0)),
            pl.BlockSpec((1, KV_LORA, tm), lambda i: (i, 0, 0)),
            pl.BlockSpec((tm, IDX_DIM), lambda i: (i, 0)),
            pl.BlockSpec((IDX_HEADS, tm), lambda i: (0, i)),
        ),
        compiler_params=_params("parallel"),
        name="dsa_prep",
    )(proj, proj, proj, proj, qg, kvg, wuq, wuk, wqi, lng, lnb)


def _key_to_f32(key):
    return pltpu.bitcast(key ^ ((key >> 31) & 0x7FFFFFFF), F32)


def _dsa_attn_body(qidx_ref, qlat_ref, wt_ref, kidx_ref, ckv_ref, ckvt_ref, wuv_ref, o_ref,
                   sc_ref, sh_ref, bias_ref, m_ref, l_ref, acc_ref, *, n_sel):
    i = pl.program_id(1)
    t0 = i * Q_BLOCK
    n_tiles = (t0 + Q_BLOCK + S_TILE - 1) // S_TILE

    t_lane = t0 + lax.broadcasted_iota(jnp.int32, (S_TILE, Q_BLOCK), 1)
    s_row = lax.broadcasted_iota(jnp.int32, (S_TILE, Q_BLOCK), 0)

    def tile_rows(j):
        return pl.ds(pl.multiple_of(j * S_TILE, S_TILE), S_TILE)

    wt = wt_ref[...]

    def score_tile(j, carry):
        k_tile = kidx_ref[tile_rows(j), :]
        sc = jnp.zeros((S_TILE, Q_BLOCK), F32)
        for g in range(IDX_HEADS // IDX_GROUP):
            qg = qidx_ref[0, g * IDX_GROUP:(g + 1) * IDX_GROUP].reshape(IDX_GROUP * Q_BLOCK, IDX_DIM)
            logits = _dot_nt(k_tile, qg)
            for u in range(IDX_GROUP):
                h = g * IDX_GROUP + u
                sc = sc + wt[h:h + 1, :] * jnp.maximum(logits[:, u * Q_BLOCK:(u + 1) * Q_BLOCK], 0.0)
        sc = jnp.where(s_row + j * S_TILE <= t_lane, sc, -jnp.inf)
        sc_ref[tile_rows(j), :] = sc
        sh_ref[tile_rows(j), :] = sc.astype(BF16)
        return carry

    lax.fori_loop(0, n_tiles, score_tile, 0)

    def tree_sum(parts, chains=8):
        sums = list(parts[:chains])
        for n, piece in enumerate(parts[chains:]):
            sums[n % chains] = sums[n % chains] + piece
        while len(sums) > 1:
            sums = [a + b for a, b in zip(sums[0::2], sums[1::2])]
        return sums[0]

    def count(pred):
        def body(j, acc):
            hit = jnp.where(pred(sc_ref[tile_rows(j), :]), 1, 0)
            return acc + tree_sum([hit[r:r + SUBLANES, :] for r in range(0, S_TILE, SUBLANES)])

        acc = lax.fori_loop(0, n_tiles, body, jnp.zeros((SUBLANES, Q_BLOCK), jnp.int32))
        return jnp.sum(acc, axis=0, keepdims=True)

    one16, zero16 = jnp.ones((), BF16), jnp.zeros((), BF16)

    def count_rounded_ge(key16):
        cand = _key_to_f32((key16 << 16) | ((key16 >> 31) & 0xFFFF)).astype(BF16)

        def body(j, acc):
            hit = jnp.where(sh_ref[tile_rows(j), :] >= cand, one16, zero16)
            return acc + tree_sum([hit[r:r + BF16_ROWS, :] for r in range(0, S_TILE, BF16_ROWS)])

        acc = lax.fori_loop(0, n_tiles, body, jnp.zeros((BF16_ROWS, Q_BLOCK), BF16))
        return jnp.sum(acc.astype(F32), axis=0, keepdims=True).astype(jnp.int32)

    def try_key16(cand16, key16):
        return jnp.where(count_rounded_ge(cand16) >= n_sel, cand16, key16)

    zero = jnp.zeros((1, Q_BLOCK), jnp.int32)
    key16 = try_key16(zero, zero - 2 ** 15)
    key16 = lax.fori_loop(0, 15, lambda b, k: try_key16(k | (1 << (14 - b)), k), key16)

    coarse = (key16 << 16) | ((key16 >> 31) & 0xFFFF)
    base = coarse - 2 ** 15

    def try_offset(cand_off, off, key, n_ge):
        cand_key = base + cand_off
        cand = _key_to_f32(cand_key)
        n = count(lambda sc: sc >= cand)
        ok = n >= n_sel
        return jnp.where(ok, cand_off, off), jnp.where(ok, cand_key, key), jnp.where(ok, n, n_ge)

    carry = try_offset(zero, zero, zero + INT_MIN, zero)
    _, key, n_ge = lax.fori_loop(0, 17, lambda b, c: try_offset(c[0] | (1 << (16 - b)), *c), carry)
    thr = _key_to_f32(key)
    thr = jnp.maximum(jnp.where(thr != thr, F32_LOWEST, thr), F32_LOWEST)

    @pl.when(jnp.max(n_ge) > n_sel)
    def _():
        need = (n_sel - count(lambda sc: sc > thr)).astype(F32)
        r = lax.broadcasted_iota(jnp.int32, (S_TILE, S_TILE), 0)
        c = lax.broadcasted_iota(jnp.int32, (S_TILE, S_TILE), 1)
        lower_incl = jnp.where(c <= r, 1.0, 0.0).astype(BF16)

        def body(j, run):
            sc = sc_ref[tile_rows(j), :]
            tied = sc == thr
            rank = run + _dot(lower_incl, jnp.where(tied, 1.0, 0.0).astype(BF16))
            sc_ref[tile_rows(j), :] = jnp.where(tied & (rank > need), -jnp.inf, sc)
            return rank[S_TILE - 1:S_TILE, :]

        lax.fori_loop(0, n_tiles, body, jnp.zeros((1, Q_BLOCK), F32))

    gw = ATT_GROUP * Q_BLOCK
    groups = range(A_HEADS // ATT_GROUP)
    lane = lax.broadcasted_iota(jnp.int32, (1, gw), 1)
    slope_rows = []
    for g in groups:
        row_g = jnp.zeros((1, gw), F32)
        for u in range(ATT_GROUP):
            slope = 2.0 ** (-8.0 * (g * ATT_GROUP + u + 1) / A_HEADS) * LOG2E
            row_g = jnp.where(lane >= u * Q_BLOCK, slope, row_g)
        slope_rows.append(row_g)

    @pl.when(i == 0)
    def _():
        s_rel = lax.broadcasted_iota(jnp.int32, (S_TILE, gw), 0).astype(F32)
        for g in groups:
            bias_ref[g] = slope_rows[g] * s_rel

    m_ref[...] = jnp.full(m_ref.shape, -1e30, F32)
    l_ref[...] = jnp.zeros_like(l_ref)
    acc_ref[...] = jnp.zeros_like(acc_ref)

    def att_tile(j, carry):
        drop = jnp.where(sc_ref[tile_rows(j), :] >= thr, 0.0, -jnp.inf)
        drop = jnp.concatenate([drop] * ATT_GROUP, axis=1)
        ckv_tile = ckv_ref[tile_rows(j), :]
        ckvt_tile = ckvt_ref[j]
        off = (j * S_TILE - t0).astype(F32)
        logits = [_dot_nt(ckv_tile, qlat_ref[0, g * ATT_GROUP:(g + 1) * ATT_GROUP].reshape(gw, KV_LORA))
                  for g in groups]
        probs, alphas = [], []
        for g in groups:
            a = logits[g] + bias_ref[g] + drop
            shift = slope_rows[g] * off
            m_old = m_ref[g:g + 1, :]
            m_new = jnp.maximum(m_old, jnp.max(a, axis=0, keepdims=True) + shift)
            alpha = jnp.exp2(m_old - m_new)
            p = jnp.exp2(a - (m_new - shift))
            l_ref[g:g + 1, :] = alpha * l_ref[g:g + 1, :] + jnp.sum(p, axis=0, keepdims=True)
            m_ref[g:g + 1, :] = m_new
            probs.append(p.astype(BF16))
            alphas.append(alpha)
        for g in groups:
            acc_ref[g] = alphas[g] * acc_ref[g] + _dot(ckvt_tile, probs[g])
        return carry

    lax.fori_loop(0, n_tiles, att_tile, 0)

    for g in groups:
        o_lat_t = acc_ref[g] * (1.0 / l_ref[g:g + 1, :])
        for u in range(ATT_GROUP):
            h = g * ATT_GROUP + u
            o_lat = o_lat_t[:, u * Q_BLOCK:(u + 1) * Q_BLOCK].T.astype(BF16)
            o_ref[:, h * A_HEAD_DIM:(h + 1) * A_HEAD_DIM] = _dot(o_lat, wuv_ref[h]).astype(BF16)


def _dsa_attn(qlat, qidx, wt, kidx, ckv, ckvt, wuv, *, bsz, seq):
    nq = seq // Q_BLOCK
    ns = seq // S_TILE
    n_sel = min(TOPK_MAX, seq // 4)
    ng, gw = A_HEADS // ATT_GROUP, ATT_GROUP * Q_BLOCK
    return pl.pallas_call(
        functools.partial(_dsa_attn_body, n_sel=n_sel),
        out_shape=jax.ShapeDtypeStruct((bsz * seq, A_HEADS * A_HEAD_DIM), BF16),
        grid=(bsz, nq),
        in_specs=[
            pl.BlockSpec((1, IDX_HEADS, Q_BLOCK, IDX_DIM), lambda b, i: (b * nq + i, 0, 0, 0)),
            pl.BlockSpec((1, A_HEADS, Q_BLOCK, KV_LORA), lambda b, i: (b * nq + i, 0, 0, 0)),
            pl.BlockSpec((IDX_HEADS, Q_BLOCK), lambda b, i: (0, b * nq + i)),
            pl.BlockSpec((seq, IDX_DIM), lambda b, i: (b, 0)),
            pl.BlockSpec((seq, KV_LORA), lambda b, i: (b, 0)),
            pl.BlockSpec((ns, KV_LORA, S_TILE), lambda b, i: (b, 0, 0)),
            pl.BlockSpec((A_HEADS, KV_LORA, A_HEAD_DIM), lambda b, i: (0, 0, 0)),
        ],
        out_specs=pl.BlockSpec((Q_BLOCK, A_HEADS * A_HEAD_DIM), lambda b, i: (b * nq + i, 0)),
        scratch_shapes=[
            pltpu.VMEM((seq, Q_BLOCK), F32),
            pltpu.VMEM((seq, Q_BLOCK), BF16),
            pltpu.VMEM((ng, S_TILE, gw), F32),
            pltpu.VMEM((ng, gw), F32),
            pltpu.VMEM((ng, gw), F32),
            pltpu.VMEM((ng, KV_LORA, gw), F32),
        ],
        compiler_params=_params("parallel", "arbitrary"),
        name="dsa_attn",
    )(qidx, qlat, wt, kidx, ckv, ckvt, wuv)


def _gdn_prep_body(x_ref, halo_ref, small_ref, cw_ref, alog_ref, dtb_ref,
                   q_ref, k_ref, v_ref, gcol_ref, grow_ref):
    ts = x_ref.shape[0]
    first = pl.program_id(1) == 0
    x = x_ref[...]
    xb = x.astype(BF16)
    halo = jnp.where(first, 0.0, halo_ref[...]).astype(BF16).astype(F32)
    top = jnp.concatenate([halo, xb[:SUBLANES].astype(F32)], axis=0)
    r = lax.broadcasted_iota(jnp.int32, (ts, ts), 0)
    c = lax.broadcasted_iota(jnp.int32, (ts, ts), 1)
    cw = cw_ref[...]
    y = cw[CONV_WIDTH - 1:CONV_WIDTH, :] * x
    for d in range(1, CONV_WIDTH):
        moved = _dot(jnp.where(c == r - d, 1.0, 0.0).astype(BF16), xb)
        moved = jnp.concatenate([top[SUBLANES - d:2 * SUBLANES - d], moved[SUBLANES:]], axis=0)
        y = y + cw[CONV_WIDTH - 1 - d:CONV_WIDTH - d, :] * moved
    y = y * _sigmoid(y)

    for h in range(B_HEADS):
        lo, hi = h * B_HEAD_DIM, (h + 1) * B_HEAD_DIM
        qh = y[:, lo:hi]
        q_ref[:, lo:hi] = qh * lax.rsqrt(jnp.sum(qh * qh, axis=-1, keepdims=True) + EPS) * (
            B_HEAD_DIM ** -0.5)
        kh = y[:, B_WIDTH + lo:B_WIDTH + hi]
        k_ref[:, lo:hi] = kh * lax.rsqrt(jnp.sum(kh * kh, axis=-1, keepdims=True) + EPS)
    v_ref[...] = y[:, 2 * B_WIDTH:]

    small = small_ref[...]
    pre = small + dtb_ref[...]
    softplus = jnp.maximum(pre, 0.0) + jnp.log1p(jnp.exp(-jnp.abs(pre)))
    g = -jnp.exp(alog_ref[...]) * softplus
    row = lax.broadcasted_iota(jnp.int32, g.shape, 0) & (GDN_CHUNK - 1)
    shift = 1
    while shift < GDN_CHUNK:
        g = g + jnp.where(row >= shift, pltpu.roll(g, shift, 0), 0.0)
        shift *= 2
    lane = lax.broadcasted_iota(jnp.int32, g.shape, 1)
    is_a = (lane >= SMALL_A) & (lane < SMALL_A + B_HEADS)
    gb = jnp.where(is_a, g, _sigmoid(small))
    gcol_ref[...] = gb
    grow_ref[0] = gb.T


def _gdn_prep(proj, conv_w, alog_p, dtb_p, *, bsz, seq, ts=256):
    t = bsz * seq
    ns = seq // ts
    c = 3 * B_WIDTH
    return pl.pallas_call(
        _gdn_prep_body,
        out_shape=(
            jax.ShapeDtypeStruct((t, B_WIDTH), F32),
            jax.ShapeDtypeStruct((t, B_WIDTH), F32),
            jax.ShapeDtypeStruct((t, B_WIDTH), F32),
            jax.ShapeDtypeStruct((t, LANES), F32),
            jax.ShapeDtypeStruct((bsz, LANES, seq), F32),
        ),
        grid=(bsz, ns),
        in_specs=[
            pl.BlockSpec((ts, c), lambda b, i: (b * ns + i, COL_QKV // c)),
            pl.BlockSpec((SUBLANES, c),
                         lambda b, i: (jnp.maximum((b * ns + i) * (ts // SUBLANES) - 1, 0), 0)),
            pl.BlockSpec((ts, LANES), lambda b, i: (b * ns + i, COL_SMALL // LANES)),
            pl.BlockSpec((CONV_WIDTH, c), lambda b, i: (0, 0)),
            pl.BlockSpec((1, LANES), lambda b, i: (0, 0)),
            pl.BlockSpec((1, LANES), lambda b, i: (0, 0)),
        ],
        out_specs=(
            pl.BlockSpec((ts, B_WIDTH), lambda b, i: (b * ns + i, 0)),
            pl.BlockSpec((ts, B_WIDTH), lambda b, i: (b * ns + i, 0)),
            pl.BlockSpec((ts, B_WIDTH), lambda b, i: (b * ns + i, 0)),
            pl.BlockSpec((ts, LANES), lambda b, i: (b * ns + i, 0)),
            pl.BlockSpec((1, LANES, ts), lambda b, i: (b, 0, i)),
        ),
        compiler_params=_params("parallel", "arbitrary"),
        name="gdn_prep",
    )(proj, proj, proj, conv_w, alog_p, dtb_p)


def _dot1(a, b):
    return _dot(a.astype(BF16), b.astype(BF16))


def _each(fn, *lists):
    return [fn(*args) for args in zip(*lists)]


def _unit_lower_inverses(mats, row, col):
    base_log2 = 4
    eye = jnp.where(row == col, 1.0, 0.0)
    in_block = (row >> base_log2) == (col >> base_log2)
    d = [jnp.where(in_block, a, 0.0) for a in mats]
    x = [eye - dh for dh in d]
    p = _each(_dot1, d, d)
    for step in range(base_log2 - 1):
        x = _each(lambda xh, th: xh + th, x, _each(_dot1, x, p))
        if step < base_log2 - 2:
            p = _each(_dot1, p, p)
    sh = base_log2
    while (1 << sh) < GDN_CHUNK:
        same_parent = (row >> (sh + 1)) == (col >> (sh + 1))
        same_block = (row >> sh) == (col >> sh)
        quad = [jnp.where(same_block, 0.0, jnp.where(same_parent, a, 0.0)) for a in mats]
        x = _each(lambda xh, th: xh - th, x, _each(_dot1, x, _each(_dot1, quad, x)))
        sh += 1
    return x


def _gdn_chunk_body(q_ref, k_ref, v_ref, z_ref, gcol_ref, grow_ref, ng_ref, o_ref, state_ref):
    @pl.when(pl.program_id(0) == 0)
    def _():
        state_ref[...] = jnp.zeros_like(state_ref)

    c = GDN_CHUNK
    bsz = q_ref.shape[0]
    units = [(b, h) for b in range(bsz) for h in range(B_HEADS)]
    span = lambda h: slice(h * B_HEAD_DIM, (h + 1) * B_HEAD_DIM)
    row = lax.broadcasted_iota(jnp.int32, (c, c), 0)
    col = lax.broadcasted_iota(jnp.int32, (c, c), 1)
    gcol = [gcol_ref[b] for b in range(bsz)]
    grow = [grow_ref[b] for b in range(bsz)]
    e_cum = [jnp.exp(g) for g in gcol]
    e_rest = [jnp.exp(g[c - 1:c, :] - g) for g in gcol]
    e_last = [jnp.exp(g[:, c - 1:c]) for g in grow]
    col_a = lambda tab, b, h: tab[b][:, SMALL_A + h:SMALL_A + h + 1]

    q = [q_ref[b, :, span(h)] for b, h in units]
    k = [k_ref[b, :, span(h)] for b, h in units]
    v = [v_ref[b, :, span(h)] for b, h in units]
    beta = [gcol[b][:, SMALL_B + h:SMALL_B + h + 1] for b, h in units]
    decay = [jnp.exp(jnp.where(row >= col, col_a(gcol, b, h) - grow[b][SMALL_A + h:SMALL_A + h + 1, :],
                               -jnp.inf)) for b, h in units]
    k_beta = _each(lambda kh, bh: kh * bh, k, beta)
    k16 = [kh.astype(BF16) for kh in k]
    kk = _each(_dot_nt, [kb.astype(BF16) for kb in k_beta], k16)
    qk = _each(_dot_nt, [qh.astype(BF16) for qh in q], k16)
    a_mat = _each(lambda m, dh: jnp.where(row > col, m * dh, 0.0), kk, decay)
    attn = _each(lambda m, dh: (m * dh).astype(BF16), qk, decay)
    t_inv = _unit_lower_inverses(a_mat, row, col)
    rhs = [jnp.concatenate([v[u] * beta[u], k_beta[u] * col_a(e_cum, b, h)], axis=1)
           for u, (b, h) in enumerate(units)]
    sol = _each(_dot1, t_inv, rhs)
    q_dec = [(q[u] * col_a(e_cum, b, h)).astype(BF16) for u, (b, h) in enumerate(units)]
    k_dec_t = [(k[u] * col_a(e_rest, b, h)).T.astype(BF16) for u, (b, h) in enumerate(units)]

    state = [state_ref[u] for u in range(len(units))]
    s16 = [s.astype(BF16) for s in state]
    v_new = [sol[u][:, :B_HEAD_DIM] - _dot(sol[u][:, B_HEAD_DIM:].astype(BF16), s16[u])
             for u in range(len(units))]
    v16 = [x.astype(BF16) for x in v_new]
    for u, (b, h) in enumerate(units):
        state_ref[u] = state[u] * e_last[b][SMALL_A + h:SMALL_A + h + 1, :] + _dot(k_dec_t[u], v16[u])
    out = [_dot(q_dec[u], s16[u]) + _dot(attn[u], v16[u]) for u in range(len(units))]
    for u, (b, h) in enumerate(units):
        o = out[u]
        y = o * lax.rsqrt(jnp.mean(o * o, axis=-1, keepdims=True) + EPS) * ng_ref[...]
        zh = z_ref[b, :, span(h)]
        o_ref[b, :, span(h)] = (y * (zh * _sigmoid(zh))).astype(BF16)


def _gdn_chunk(q, k, v, proj, gcol, grow, ng, *, bsz, seq):
    c = GDN_CHUNK
    per_seq = lambda a: a.reshape(bsz, seq, a.shape[-1])
    tok = lambda i: (0, i, 0)
    out = pl.pallas_call(
        _gdn_chunk_body,
        out_shape=jax.ShapeDtypeStruct((bsz, seq, B_WIDTH), BF16),
        grid=(seq // c,),
        in_specs=[
            pl.BlockSpec((bsz, c, B_WIDTH), tok),
            pl.BlockSpec((bsz, c, B_WIDTH), tok),
            pl.BlockSpec((bsz, c, B_WIDTH), tok),
            pl.BlockSpec((bsz, c, B_WIDTH), lambda i: (0, i, COL_Z // B_WIDTH)),
            pl.BlockSpec((bsz, c, LANES), tok),
            pl.BlockSpec((bsz, LANES, c), lambda i: (0, 0, i)),
            pl.BlockSpec((1, B_HEAD_DIM), lambda i: (0, 0)),
        ],
        out_specs=pl.BlockSpec((bsz, c, B_WIDTH), tok),
        scratch_shapes=[pltpu.VMEM((bsz * B_HEADS, B_HEAD_DIM, B_HEAD_DIM), F32)],
        compiler_params=_params("arbitrary"),
        name="gdn_chunk",
    )(per_seq(q), per_seq(k), per_seq(v), per_seq(proj), per_seq(gcol), grow, ng)
    return out.reshape(bsz * seq, B_WIDTH)


def _out_ln_body(h_ref, oa_ref, ob_ref, wa_ref, wb_ref, g_ref, b_ref, o_ref, *, sub):
    spans = [slice(r, r + sub) for r in range(0, h_ref.shape[0], sub)]
    mixes = [_dot(oa_ref[s, :], wa_ref[...]) + _dot(ob_ref[s, :], wb_ref[...]) for s in spans]
    for s, mix in zip(spans, mixes):
        o_ref[s, :] = _layer_norm(DEEPNORM_ALPHA * h_ref[s, :] + mix, g_ref[...], b_ref[...])


def _out_ln(h, oa, ob, w, g, b, *, tm=1024, sub=256):
    t, d = h.shape
    da, db = oa.shape[1], ob.shape[1]
    assert da == db and w.shape == (da + db, d)
    return pl.pallas_call(
        functools.partial(_out_ln_body, sub=sub),
        out_shape=jax.ShapeDtypeStruct((t, d), F32),
        grid=(t // tm,),
        in_specs=[
            pl.BlockSpec((tm, d), lambda i: (i, 0)),
            pl.BlockSpec((tm, da), lambda i: (i, 0)),
            pl.BlockSpec((tm, db), lambda i: (i, 0)),
            pl.BlockSpec((da, d), lambda i: (0, 0), pipeline_mode=pl.Buffered(1)),
            pl.BlockSpec((db, d), lambda i: (1, 0), pipeline_mode=pl.Buffered(1)),
            pl.BlockSpec((1, d), lambda i: (0, 0)),
            pl.BlockSpec((1, d), lambda i: (0, 0)),
        ],
        out_specs=pl.BlockSpec((tm, d), lambda i: (i, 0)),
        compiler_params=_params("parallel"),
        name="out_ln",
    )(h, oa, ob, w, w, g, b)


def _regroup_w_in(w_in):
    cuts = [0, Q_LORA, KV_LORA, IDX_DIM, IDX_HEADS, B_WIDTH, B_WIDTH, B_WIDTH, B_WIDTH, B_HEADS, B_HEADS]
    offs = [sum(cuts[:n + 1]) for n in range(len(cuts))]
    w16 = w_in.astype(BF16)
    part = lambda lo, hi: w16[:, offs[lo]:offs[hi]]
    pad = jnp.zeros((w_in.shape[0], LANES - IDX_HEADS - 2 * B_HEADS), BF16)
    small = jnp.concatenate([part(0, 3), part(3, 4), part(8, 10), pad], axis=1)
    return part(4, 8), small


def _lane_pad(vec, offset):
    out = jnp.zeros((1, LANES), F32)
    return out.at[0, offset:offset + vec.shape[0]].set(vec.astype(F32))


def kernel(x, ffn1_w_gate, ffn1_w_up, ffn1_w_down, ln1_g, ln1_b, w_in, q_norm_g, kv_norm_g, w_uq, w_uk, w_uv, w_q_idx, k_idx_ln_g, k_idx_ln_b, conv_w, a_log, dt_bias, o_norm_g, w_out, ln2_g, ln2_b, ffn2_w_gate, ffn2_w_up, ffn2_w_down, ln3_g, ln3_b):
    bsz, seq, d = x.shape
    assert seq % S_TILE == 0 and seq % GDN_CHUNK == 0 and x.dtype == F32
    assert seq // BF16_ROWS <= 256, "packed bf16 hit counts must stay exactly representable"
    t = bsz * seq
    row = lambda p: p.reshape(1, -1).astype(F32)
    h = x.reshape(t, d)
    for l in range(DEPTH):
        h = _ffn_ln(h, ffn1_w_gate[l], ffn1_w_up[l], ffn1_w_down[l], row(ln1_g[l]), row(ln1_b[l]))
        proj = _in_proj(h, _regroup_w_in(w_in[l]))
        qlat, qidx, ckv, ckvt, kidx, wt = _dsa_prep(
            proj, row(q_norm_g[l]), row(kv_norm_g[l]),
            w_uq[l].reshape(Q_LORA, A_HEADS * A_HEAD_DIM).astype(BF16), w_uk[l].astype(BF16),
            w_q_idx[l].astype(BF16), row(k_idx_ln_g[l]), row(k_idx_ln_b[l]))
        o_a = _dsa_attn(qlat, qidx, wt, kidx, ckv, ckvt, w_uv[l].astype(BF16), bsz=bsz, seq=seq)
        gq, gk, gv, gcol, grow = _gdn_prep(proj, conv_w[l].astype(F32), _lane_pad(a_log[l], SMALL_A),
                                           _lane_pad(dt_bias[l], SMALL_A), bsz=bsz, seq=seq)
        o_b = _gdn_chunk(gq, gk, gv, proj, gcol, grow, row(o_norm_g[l]), bsz=bsz, seq=seq)
        h = _out_ln(h, o_a, o_b, w_out[l].astype(BF16), row(ln2_g[l]), row(ln2_b[l]))
        h = _ffn_ln(h, ffn2_w_gate[l], ffn2_w_up[l], ffn2_w_down[l], row(ln3_g[l]), row(ln3_b[l]))
    return h.reshape(bsz, seq, d)
```

```python
import functools
import math

import jax
import jax.numpy as jnp
from jax import lax
from jax.experimental import pallas as pl
from jax.experimental.pallas import tpu as pltpu

F32 = jnp.float32
BF16 = jnp.bfloat16

A_HEADS = 8
A_HEAD_DIM = 128
Q_LORA = 512
KV_LORA = 256
IDX_HEADS = 16
IDX_DIM = 128
TOPK_MAX = 256
B_HEADS = 8
B_HEAD_DIM = 128
B_WIDTH = B_HEADS * B_HEAD_DIM
CONV_WIDTH = 4
DEPTH = 1
DEEPNORM_ALPHA = (2.0 * DEPTH) ** 0.25
EPS = 1e-6

LANES = 128
SUBLANES = 8
BF16_ROWS = 16
VMEM_LIMIT_BYTES = 62 * 1024 * 1024

COL_QKV = 0
COL_Z = 3 * B_WIDTH
COL_CQ = COL_Z + B_WIDTH
COL_CKV = COL_CQ + Q_LORA
COL_KIDX = COL_CKV + KV_LORA
COL_SMALL = COL_KIDX + IDX_DIM
D_IN_PAD = COL_SMALL + LANES
SMALL_W = 0
SMALL_A = IDX_HEADS
SMALL_B = IDX_HEADS + B_HEADS

Q_BLOCK = 256
S_TILE = 512
IDX_GROUP = 1
ATT_GROUP = 1
ATT_STAGE = A_HEADS // ATT_GROUP
GDN_CHUNK = 128
INT_MIN = -2 ** 31
F32_LOWEST = -3.4028234663852886e38
LOG2E = math.log2(math.e)


def _dot(a, b):
    return jnp.dot(a, b, preferred_element_type=F32)


def _dot_nt(a, b):
    return lax.dot_general(a, b, (((1,), (1,)), ((), ())), preferred_element_type=F32)


def _sigmoid(x):
    return 1.0 / (1.0 + jnp.exp(-x))


def _layer_norm(y, g, b):
    mu = jnp.mean(y, axis=-1, keepdims=True)
    yc = y - mu
    var = jnp.mean(yc * yc, axis=-1, keepdims=True)
    return yc * lax.rsqrt(var + EPS) * g + b


def _params(*sem):
    return pltpu.CompilerParams(dimension_semantics=sem, vmem_limit_bytes=VMEM_LIMIT_BYTES)


def _ffn_ln_body(x_ref, wg_ref, wu_ref, wd_ref, g_ref, b_ref, o_ref, xb_ref):
    j = pl.program_id(1)

    @pl.when(j == 0)
    def _():
        xb_ref[...] = x_ref[...].astype(BF16)
        o_ref[...] = jnp.zeros_like(o_ref)

    xb = xb_ref[...]
    gate = _dot(xb, wg_ref[...].astype(BF16))
    up = _dot(xb, wu_ref[...].astype(BF16))
    act = gate * _sigmoid(gate) * up
    o_ref[...] += _dot(act.astype(BF16), wd_ref[...].astype(BF16))

    @pl.when(j == pl.num_programs(1) - 1)
    def _():
        y = DEEPNORM_ALPHA * x_ref[...] + 0.5 * o_ref[...]
        o_ref[...] = _layer_norm(y, g_ref[...], b_ref[...])


def _ffn_ln(x, wg, wu, wd, g, b, *, tm=1024, tf=256):
    t, d = x.shape
    f = wg.shape[1]
    return pl.pallas_call(
        _ffn_ln_body,
        out_shape=jax.ShapeDtypeStruct((t, d), F32),
        grid=(t // tm, f // tf),
        in_specs=[
            pl.BlockSpec((tm, d), lambda i, j: (i, 0)),
            pl.BlockSpec((d, tf), lambda i, j: (0, j)),
            pl.BlockSpec((d, tf), lambda i, j: (0, j)),
            pl.BlockSpec((tf, d), lambda i, j: (j, 0)),
            pl.BlockSpec((1, d), lambda i, j: (0, 0)),
            pl.BlockSpec((1, d), lambda i, j: (0, 0)),
        ],
        out_specs=pl.BlockSpec((tm, d), lambda i, j: (i, 0)),
        scratch_shapes=[pltpu.VMEM((tm, d), BF16)],
        compiler_params=_params("parallel", "arbitrary"),
        name="ffn_ln",
    )(x, wg, wu, wd, g, b)


def _in_proj_body(x_ref, *refs, tn):
    w_refs, o_ref = refs[:-1], refs[-1]
    xb = x_ref[...].astype(BF16)
    col = 0
    for w_ref in w_refs:
        for c in range(0, w_ref.shape[1], tn):
            o_ref[:, col + c:col + c + tn] = _dot(xb, w_ref[:, c:c + tn])
        col += w_ref.shape[1]


def _in_proj(x, weights, *, tm=512, tn=1024):
    t, d = x.shape
    n = sum(w.shape[1] for w in weights)
    return pl.pallas_call(
        functools.partial(_in_proj_body, tn=tn),
        out_shape=jax.ShapeDtypeStruct((t, n), F32),
        grid=(t // tm,),
        in_specs=[pl.BlockSpec((tm, d), lambda i: (i, 0))] + [
            pl.BlockSpec(w.shape, lambda i: (0, 0), pipeline_mode=pl.Buffered(1)) for w in weights],
        out_specs=pl.BlockSpec((tm, n), lambda i: (i, 0)),
        compiler_params=_params("parallel"),
        name="in_proj",
    )(x, *weights)


def _dsa_prep_body(cq_ref, ckv_ref, kidx_ref, small_ref, qg_ref, kvg_ref, wuq_ref, wuk_ref,
                   wqi_ref, lng_ref, lnb_ref,
                   qlat_ref, qidx_ref, ckv_o_ref, ckvt_o_ref, kidx_o_ref, wt_o_ref):
    tm = cq_ref.shape[0]
    nb = tm // Q_BLOCK

    cq = cq_ref[...]
    cq = cq * lax.rsqrt(jnp.mean(cq * cq, axis=-1, keepdims=True) + EPS) * qg_ref[...]
    cqb = cq.astype(BF16)

    q = _dot(cqb, wuq_ref[...]).astype(BF16)
    for h in range(A_HEADS):
        ql = _dot(q[:, h * A_HEAD_DIM:(h + 1) * A_HEAD_DIM], wuk_ref[h])
        ql = ql * (A_HEAD_DIM ** -0.5 * LOG2E)
        qlat_ref[:, h] = ql.astype(BF16).reshape(nb, Q_BLOCK, KV_LORA)

    qi = _dot(cqb, wqi_ref[...]).astype(BF16)
    for h in range(IDX_HEADS):
        qidx_ref[:, h] = qi[:, h * IDX_DIM:(h + 1) * IDX_DIM].reshape(nb, Q_BLOCK, IDX_DIM)

    ckv = ckv_ref[...]
    ckv = ckv * lax.rsqrt(jnp.mean(ckv * ckv, axis=-1, keepdims=True) + EPS) * kvg_ref[...]
    ckv_o_ref[...] = ckv.astype(BF16)
    ckvt_o_ref[0, :KV_LORA] = ckv.T.astype(BF16)
    ckvt_o_ref[0, KV_LORA:] = jnp.ones((BF16_ROWS, tm), BF16)

    kidx_o_ref[...] = _layer_norm(kidx_ref[...], lng_ref[...], lnb_ref[...]).astype(BF16)

    wt = small_ref[...].T
    wt_o_ref[...] = wt[SMALL_W:SMALL_W + IDX_HEADS, :] * (IDX_HEADS ** -0.5 * IDX_DIM ** -0.5)


def _dsa_prep(proj, qg, kvg, wuq, wuk, wqi, lng, lnb, *, tm=S_TILE):
    t = proj.shape[0]
    nq = t // Q_BLOCK
    full = lambda *shape: pl.BlockSpec(shape, lambda i: (0,) * len(shape))
    return pl.pallas_call(
        _dsa_prep_body,
        out_shape=(
            jax.ShapeDtypeStruct((nq, A_HEADS, Q_BLOCK, KV_LORA), BF16),
            jax.ShapeDtypeStruct((nq, IDX_HEADS, Q_BLOCK, IDX_DIM), BF16),
            jax.ShapeDtypeStruct((t, KV_LORA), BF16),
            jax.ShapeDtypeStruct((t // tm, KV_LORA + BF16_ROWS, tm), BF16),
            jax.ShapeDtypeStruct((t, IDX_DIM), BF16),
            jax.ShapeDtypeStruct((IDX_HEADS, t), F32),
        ),
        grid=(t // tm,),
        in_specs=[
            pl.BlockSpec((tm, Q_LORA), lambda i: (i, COL_CQ // Q_LORA)),
            pl.BlockSpec((tm, KV_LORA), lambda i: (i, COL_CKV // KV_LORA)),
            pl.BlockSpec((tm, IDX_DIM), lambda i: (i, COL_KIDX // IDX_DIM)),
            pl.BlockSpec((tm, LANES), lambda i: (i, COL_SMALL // LANES)),
            full(1, Q_LORA), full(1, KV_LORA),
            full(Q_LORA, A_HEADS * A_HEAD_DIM),
            full(A_HEADS, A_HEAD_DIM, KV_LORA),
            full(Q_LORA, IDX_HEADS * IDX_DIM),
            full(1, IDX_DIM), full(1, IDX_DIM),
        ],
        out_specs=(
            pl.BlockSpec((tm // Q_BLOCK, A_HEADS, Q_BLOCK, KV_LORA), lambda i: (i, 0, 0, 0)),
            pl.BlockSpec((tm // Q_BLOCK, IDX_HEADS, Q_BLOCK, IDX_DIM), lambda i: (i, 0, 0, 0)),
            pl.BlockSpec((tm, KV_LORA), lambda i: (i, 0)),
            pl.BlockSpec((1, KV_LORA + BF16_ROWS, tm), lambda i: (i, 0, 0)),
            pl.BlockSpec((tm, IDX_DIM), lambda i: (i, 0)),
            pl.BlockSpec((IDX_HEADS, tm), lambda i: (0, i)),
        ),
        compiler_params=_params("parallel"),
        name="dsa_prep",
    )(proj, proj, proj, proj, qg, kvg, wuq, wuk, wqi, lng, lnb)


def _key_to_f32(key):
    return pltpu.bitcast(key ^ ((key >> 31) & 0x7FFFFFFF), F32)


def _dsa_attn_body(qidx_ref, qlat_ref, wt_ref, kidx_ref, ckv_ref, ckvt_ref, wuv_ref, o_ref,
                   sc_ref, sh_ref, bias_ref, m_ref, l_ref, acc_ref, *, n_sel):
    i = pl.program_id(1)
    t0 = i * Q_BLOCK
    n_tiles = (t0 + Q_BLOCK + S_TILE - 1) // S_TILE

    t_lane = t0 + lax.broadcasted_iota(jnp.int32, (S_TILE, Q_BLOCK), 1)
    s_row = lax.broadcasted_iota(jnp.int32, (S_TILE, Q_BLOCK), 0)

    def tile_rows(j):
        return pl.ds(pl.multiple_of(j * S_TILE, S_TILE), S_TILE)

    wt = wt_ref[...]

    def score_tile(j, carry):
        k_tile = kidx_ref[tile_rows(j), :]
        sc = jnp.zeros((S_TILE, Q_BLOCK), F32)
        for g in range(IDX_HEADS // IDX_GROUP):
            qg = qidx_ref[0, g * IDX_GROUP:(g + 1) * IDX_GROUP].reshape(IDX_GROUP * Q_BLOCK, IDX_DIM)
            logits = _dot_nt(k_tile, qg)
            for u in range(IDX_GROUP):
                h = g * IDX_GROUP + u
                sc = sc + wt[h:h + 1, :] * jnp.maximum(logits[:, u * Q_BLOCK:(u + 1) * Q_BLOCK], 0.0)
        sc = jnp.where(s_row + j * S_TILE <= t_lane, sc, -jnp.inf)
        sc_ref[tile_rows(j), :] = sc
        sh_ref[tile_rows(j), :] = sc.astype(BF16)
        return carry

    lax.fori_loop(0, n_tiles, score_tile, 0)

    def tree_sum(parts, chains=8):
        sums = list(parts[:chains])
        for n, piece in enumerate(parts[chains:]):
            sums[n % chains] = sums[n % chains] + piece
        while len(sums) > 1:
            sums = [a + b for a, b in zip(sums[0::2], sums[1::2])]
        return sums[0]

    def count(pred):
        def body(j, acc):
            hit = jnp.where(pred(sc_ref[tile_rows(j), :]), 1, 0)
            return acc + tree_sum([hit[r:r + SUBLANES, :] for r in range(0, S_TILE, SUBLANES)])

        acc = lax.fori_loop(0, n_tiles, body, jnp.zeros((SUBLANES, Q_BLOCK), jnp.int32))
        return jnp.sum(acc, axis=0, keepdims=True)

    one16, zero16 = jnp.ones((), BF16), jnp.zeros((), BF16)

    def count_rounded_ge(key16):
        cand = _key_to_f32((key16 << 16) | ((key16 >> 31) & 0xFFFF)).astype(BF16)

        def body(j, acc):
            hit = jnp.where(sh_ref[tile_rows(j), :] >= cand, one16, zero16)
            return acc + tree_sum([hit[r:r + BF16_ROWS, :] for r in range(0, S_TILE, BF16_ROWS)])

        acc = lax.fori_loop(0, n_tiles, body, jnp.zeros((BF16_ROWS, Q_BLOCK), BF16))
        return jnp.sum(acc.astype(F32), axis=0, keepdims=True).astype(jnp.int32)

    def try_key16(cand16, key16):
        return jnp.where(count_rounded_ge(cand16) >= n_sel, cand16, key16)

    zero = jnp.zeros((1, Q_BLOCK), jnp.int32)
    key16 = try_key16(zero, zero - 2 ** 15)
    key16 = lax.fori_loop(0, 15, lambda b, k: try_key16(k | (1 << (14 - b)), k), key16)

    coarse = (key16 << 16) | ((key16 >> 31) & 0xFFFF)
    base = coarse - 2 ** 15

    def try_offset(cand_off, off, key, n_ge):
        cand_key = base + cand_off
        cand = _key_to_f32(cand_key)
        n = count(lambda sc: sc >= cand)
        ok = n >= n_sel
        return jnp.where(ok, cand_off, off), jnp.where(ok, cand_key, key), jnp.where(ok, n, n_ge)

    carry = try_offset(zero, zero, zero + INT_MIN, zero)
    _, key, n_ge = lax.fori_loop(0, 17, lambda b, c: try_offset(c[0] | (1 << (16 - b)), *c), carry)
    thr = _key_to_f32(key)
    thr = jnp.maximum(jnp.where(thr != thr, F32_LOWEST, thr), F32_LOWEST)

    @pl.when(jnp.max(n_ge) > n_sel)
    def _():
        need = (n_sel - count(lambda sc: sc > thr)).astype(F32)
        r = lax.broadcasted_iota(jnp.int32, (S_TILE, S_TILE), 0)
        c = lax.broadcasted_iota(jnp.int32, (S_TILE, S_TILE), 1)
        lower_incl = jnp.where(c <= r, 1.0, 0.0).astype(BF16)

        def body(j, run):
            sc = sc_ref[tile_rows(j), :]
            tied = sc == thr
            rank = run + _dot(lower_incl, jnp.where(tied, 1.0, 0.0).astype(BF16))
            sc_ref[tile_rows(j), :] = jnp.where(tied & (rank > need), -jnp.inf, sc)
            return rank[S_TILE - 1:S_TILE, :]

        lax.fori_loop(0, n_tiles, body, jnp.zeros((1, Q_BLOCK), F32))

    gw = ATT_GROUP * Q_BLOCK
    groups = range(A_HEADS // ATT_GROUP)
    lane = lax.broadcasted_iota(jnp.int32, (1, gw), 1)
    slope_rows = []
    for g in groups:
        row_g = jnp.zeros((1, gw), F32)
        for u in range(ATT_GROUP):
            slope = 2.0 ** (-8.0 * (g * ATT_GROUP + u + 1) / A_HEADS) * LOG2E
            row_g = jnp.where(lane >= u * Q_BLOCK, slope, row_g)
        slope_rows.append(row_g)

    @pl.when(i == 0)
    def _():
        s_rel = lax.broadcasted_iota(jnp.int32, (S_TILE, gw), 0).astype(F32)
        for g in groups:
            bias_ref[g] = slope_rows[g] * s_rel

    m_ref[...] = jnp.full(m_ref.shape, -1e30, F32)
    l_ref[...] = jnp.zeros_like(l_ref)
    acc_ref[...] = jnp.zeros_like(acc_ref)

    def att_tile(j, carry):
        drop = jnp.where(sc_ref[tile_rows(j), :] >= thr, 0.0, -jnp.inf)
        drop = jnp.concatenate([drop] * ATT_GROUP, axis=1)
        ckv_tile = ckv_ref[tile_rows(j), :]
        ckvt_tile = ckvt_ref[j]
        off = (j * S_TILE - t0).astype(F32)
        for first in range(0, len(groups), ATT_STAGE):
            batch = groups[first:first + ATT_STAGE]
            logits = {g: _dot_nt(ckv_tile,
                                 qlat_ref[0, g * ATT_GROUP:(g + 1) * ATT_GROUP].reshape(gw, KV_LORA))
                      for g in batch}
            probs, alphas = {}, {}
            for g in batch:
                a = logits[g] + bias_ref[g] + drop
                shift = slope_rows[g] * off
                m_old = m_ref[g:g + 1, :]
                m_new = jnp.maximum(m_old, jnp.max(a, axis=0, keepdims=True) + shift)
                alphas[g] = jnp.exp2(m_old - m_new)
                probs[g] = jnp.exp2(a - (m_new - shift)).astype(BF16)
                m_ref[g:g + 1, :] = m_new
            for g in batch:
                product = _dot(ckvt_tile, probs[g])
                acc_ref[g] = alphas[g] * acc_ref[g] + product[:KV_LORA]
                l_ref[g:g + 1, :] = alphas[g] * l_ref[g:g + 1, :] + product[KV_LORA:KV_LORA + 1]
        return carry

    lax.fori_loop(0, n_tiles, att_tile, 0)

    for g in groups:
        o_lat_t = acc_ref[g] * (1.0 / l_ref[g:g + 1, :])
        for u in range(ATT_GROUP):
            h = g * ATT_GROUP + u
            o_lat = o_lat_t[:, u * Q_BLOCK:(u + 1) * Q_BLOCK].T.astype(BF16)
            o_ref[:, h * A_HEAD_DIM:(h + 1) * A_HEAD_DIM] = _dot(o_lat, wuv_ref[h]).astype(BF16)


def _dsa_attn(qlat, qidx, wt, kidx, ckv, ckvt, wuv, *, bsz, seq):
    nq = seq // Q_BLOCK
    ns = seq // S_TILE
    n_sel = min(TOPK_MAX, seq // 4)
    ng, gw = A_HEADS // ATT_GROUP, ATT_GROUP * Q_BLOCK
    return pl.pallas_call(
        functools.partial(_dsa_attn_body, n_sel=n_sel),
        out_shape=jax.ShapeDtypeStruct((bsz * seq, A_HEADS * A_HEAD_DIM), BF16),
        grid=(bsz, nq),
        in_specs=[
            pl.BlockSpec((1, IDX_HEADS, Q_BLOCK, IDX_DIM), lambda b, i: (b * nq + i, 0, 0, 0)),
            pl.BlockSpec((1, A_HEADS, Q_BLOCK, KV_LORA), lambda b, i: (b * nq + i, 0, 0, 0)),
            pl.BlockSpec((IDX_HEADS, Q_BLOCK), lambda b, i: (0, b * nq + i)),
            pl.BlockSpec((seq, IDX_DIM), lambda b, i: (b, 0)),
            pl.BlockSpec((seq, KV_LORA), lambda b, i: (b, 0)),
            pl.BlockSpec((ns, KV_LORA + BF16_ROWS, S_TILE), lambda b, i: (b, 0, 0)),
            pl.BlockSpec((A_HEADS, KV_LORA, A_HEAD_DIM), lambda b, i: (0, 0, 0)),
        ],
        out_specs=pl.BlockSpec((Q_BLOCK, A_HEADS * A_HEAD_DIM), lambda b, i: (b * nq + i, 0)),
        scratch_shapes=[
            pltpu.VMEM((seq, Q_BLOCK), F32),
            pltpu.VMEM((seq, Q_BLOCK), BF16),
            pltpu.VMEM((ng, S_TILE, gw), F32),
            pltpu.VMEM((ng, gw), F32),
            pltpu.VMEM((ng, gw), F32),
            pltpu.VMEM((ng, KV_LORA, gw), F32),
        ],
        compiler_params=_params("parallel", "arbitrary"),
        name="dsa_attn",
    )(qidx, qlat, wt, kidx, ckv, ckvt, wuv)


def _gdn_prep_body(x_ref, halo_ref, small_ref, cw_ref, alog_ref, dtb_ref,
                   q_ref, k_ref, v_ref, gcol_ref, grow_ref):
    ts = x_ref.shape[0]
    first = pl.program_id(1) == 0
    x = x_ref[...]
    xb = x.astype(BF16)
    halo = jnp.where(first, 0.0, halo_ref[...]).astype(BF16).astype(F32)
    top = jnp.concatenate([halo, xb[:SUBLANES].astype(F32)], axis=0)
    r = lax.broadcasted_iota(jnp.int32, (ts, ts), 0)
    c = lax.broadcasted_iota(jnp.int32, (ts, ts), 1)
    cw = cw_ref[...]
    y = cw[CONV_WIDTH - 1:CONV_WIDTH, :] * x
    for d in range(1, CONV_WIDTH):
        moved = _dot(jnp.where(c == r - d, 1.0, 0.0).astype(BF16), xb)
        moved = jnp.concatenate([top[SUBLANES - d:2 * SUBLANES - d], moved[SUBLANES:]], axis=0)
        y = y + cw[CONV_WIDTH - 1 - d:CONV_WIDTH - d, :] * moved
    y = y * _sigmoid(y)

    for h in range(B_HEADS):
        lo, hi = h * B_HEAD_DIM, (h + 1) * B_HEAD_DIM
        qh = y[:, lo:hi]
        q_ref[:, lo:hi] = qh * lax.rsqrt(jnp.sum(qh * qh, axis=-1, keepdims=True) + EPS) * (
            B_HEAD_DIM ** -0.5)
        kh = y[:, B_WIDTH + lo:B_WIDTH + hi]
        k_ref[:, lo:hi] = kh * lax.rsqrt(jnp.sum(kh * kh, axis=-1, keepdims=True) + EPS)
    v_ref[...] = y[:, 2 * B_WIDTH:]

    small = small_ref[...]
    pre = small + dtb_ref[...]
    softplus = jnp.maximum(pre, 0.0) + jnp.log1p(jnp.exp(-jnp.abs(pre)))
    g = -jnp.exp(alog_ref[...]) * softplus
    row = lax.broadcasted_iota(jnp.int32, g.shape, 0) & (GDN_CHUNK - 1)
    shift = 1
    while shift < GDN_CHUNK:
        g = g + jnp.where(row >= shift, pltpu.roll(g, shift, 0), 0.0)
        shift *= 2
    lane = lax.broadcasted_iota(jnp.int32, g.shape, 1)
    is_a = (lane >= SMALL_A) & (lane < SMALL_A + B_HEADS)
    gb = jnp.where(is_a, g, _sigmoid(small))
    gcol_ref[...] = gb
    grow_ref[0] = gb.T


def _gdn_prep(proj, conv_w, alog_p, dtb_p, *, bsz, seq, ts=256):
    t = bsz * seq
    ns = seq // ts
    c = 3 * B_WIDTH
    return pl.pallas_call(
        _gdn_prep_body,
        out_shape=(
            jax.ShapeDtypeStruct((t, B_WIDTH), F32),
            jax.ShapeDtypeStruct((t, B_WIDTH), F32),
            jax.ShapeDtypeStruct((t, B_WIDTH), F32),
            jax.ShapeDtypeStruct((t, LANES), F32),
            jax.ShapeDtypeStruct((bsz, LANES, seq), F32),
        ),
        grid=(bsz, ns),
        in_specs=[
            pl.BlockSpec((ts, c), lambda b, i: (b * ns + i, COL_QKV // c)),
            pl.BlockSpec((SUBLANES, c),
                         lambda b, i: (jnp.maximum((b * ns + i) * (ts // SUBLANES) - 1, 0), 0)),
            pl.BlockSpec((ts, LANES), lambda b, i: (b * ns + i, COL_SMALL // LANES)),
            pl.BlockSpec((CONV_WIDTH, c), lambda b, i: (0, 0)),
            pl.BlockSpec((1, LANES), lambda b, i: (0, 0)),
            pl.BlockSpec((1, LANES), lambda b, i: (0, 0)),
        ],
        out_specs=(
            pl.BlockSpec((ts, B_WIDTH), lambda b, i: (b * ns + i, 0)),
            pl.BlockSpec((ts, B_WIDTH), lambda b, i: (b * ns + i, 0)),
            pl.BlockSpec((ts, B_WIDTH), lambda b, i: (b * ns + i, 0)),
            pl.BlockSpec((ts, LANES), lambda b, i: (b * ns + i, 0)),
            pl.BlockSpec((1, LANES, ts), lambda b, i: (b, 0, i)),
        ),
        compiler_params=_params("parallel", "arbitrary"),
        name="gdn_prep",
    )(proj, proj, proj, conv_w, alog_p, dtb_p)


def _dot1(a, b):
    return _dot(a.astype(BF16), b.astype(BF16))


def _each(fn, *lists):
    return [fn(*args) for args in zip(*lists)]


def _unit_lower_inverses(mats, row, col):
    base_log2 = 4
    eye = jnp.where(row == col, 1.0, 0.0)
    in_block = (row >> base_log2) == (col >> base_log2)
    d = [jnp.where(in_block, a, 0.0) for a in mats]
    x = [eye - dh for dh in d]
    p = _each(_dot1, d, d)
    for step in range(base_log2 - 1):
        x = _each(lambda xh, th: xh + th, x, _each(_dot1, x, p))
        if step < base_log2 - 2:
            p = _each(_dot1, p, p)
    sh = base_log2
    while (1 << sh) < GDN_CHUNK:
        same_parent = (row >> (sh + 1)) == (col >> (sh + 1))
        same_block = (row >> sh) == (col >> sh)
        quad = [jnp.where(same_block, 0.0, jnp.where(same_parent, a, 0.0)) for a in mats]
        x = _each(lambda xh, th: xh - th, x, _each(_dot1, x, _each(_dot1, quad, x)))
        sh += 1
    return x


def _gdn_chunk_body(q_ref, k_ref, v_ref, z_ref, gcol_ref, grow_ref, ng_ref, o_ref, state_ref):
    @pl.when(pl.program_id(0) == 0)
    def _():
        state_ref[...] = jnp.zeros_like(state_ref)

    c = GDN_CHUNK
    bsz = q_ref.shape[0]
    units = [(b, h) for b in range(bsz) for h in range(B_HEADS)]
    span = lambda h: slice(h * B_HEAD_DIM, (h + 1) * B_HEAD_DIM)
    row = lax.broadcasted_iota(jnp.int32, (c, c), 0)
    col = lax.broadcasted_iota(jnp.int32, (c, c), 1)
    gcol = [gcol_ref[b] for b in range(bsz)]
    grow = [grow_ref[b] for b in range(bsz)]
    e_cum = [jnp.exp(g) for g in gcol]
    e_rest = [jnp.exp(g[c - 1:c, :] - g) for g in gcol]
    e_last = [jnp.exp(g[:, c - 1:c]) for g in grow]
    col_a = lambda tab, b, h: tab[b][:, SMALL_A + h:SMALL_A + h + 1]

    q = [q_ref[b, :, span(h)] for b, h in units]
    k = [k_ref[b, :, span(h)] for b, h in units]
    v = [v_ref[b, :, span(h)] for b, h in units]
    beta = [gcol[b][:, SMALL_B + h:SMALL_B + h + 1] for b, h in units]
    decay = [jnp.exp(jnp.where(row >= col, col_a(gcol, b, h) - grow[b][SMALL_A + h:SMALL_A + h + 1, :],
                               -jnp.inf)) for b, h in units]
    k_beta = _each(lambda kh, bh: kh * bh, k, beta)
    k16 = [kh.astype(BF16) for kh in k]
    kk = _each(_dot_nt, [kb.astype(BF16) for kb in k_beta], k16)
    qk = _each(_dot_nt, [qh.astype(BF16) for qh in q], k16)
    a_mat = _each(lambda m, dh: jnp.where(row > col, m * dh, 0.0), kk, decay)
    attn = _each(lambda m, dh: (m * dh).astype(BF16), qk, decay)
    t_inv = _unit_lower_inverses(a_mat, row, col)
    rhs = [jnp.concatenate([v[u] * beta[u], k_beta[u] * col_a(e_cum, b, h)], axis=1)
           for u, (b, h) in enumerate(units)]
    sol = _each(_dot1, t_inv, rhs)
    q_dec = [(q[u] * col_a(e_cum, b, h)).astype(BF16) for u, (b, h) in enumerate(units)]
    k_dec_t = [(k[u] * col_a(e_rest, b, h)).T.astype(BF16) for u, (b, h) in enumerate(units)]

    state = [state_ref[u] for u in range(len(units))]
    s16 = [s.astype(BF16) for s in state]
    v_new = [sol[u][:, :B_HEAD_DIM] - _dot(sol[u][:, B_HEAD_DIM:].astype(BF16), s16[u])
             for u in range(len(units))]
    v16 = [x.astype(BF16) for x in v_new]
    for u, (b, h) in enumerate(units):
        state_ref[u] = state[u] * e_last[b][SMALL_A + h:SMALL_A + h + 1, :] + _dot(k_dec_t[u], v16[u])
    out = [_dot(q_dec[u], s16[u]) + _dot(attn[u], v16[u]) for u in range(len(units))]
    for u, (b, h) in enumerate(units):
        o = out[u]
        y = o * lax.rsqrt(jnp.mean(o * o, axis=-1, keepdims=True) + EPS) * ng_ref[...]
        zh = z_ref[b, :, span(h)]
        o_ref[b, :, span(h)] = (y * (zh * _sigmoid(zh))).astype(BF16)


def _gdn_chunk(q, k, v, proj, gcol, grow, ng, *, bsz, seq):
    c = GDN_CHUNK
    per_seq = lambda a: a.reshape(bsz, seq, a.shape[-1])
    tok = lambda i: (0, i, 0)
    out = pl.pallas_call(
        _gdn_chunk_body,
        out_shape=jax.ShapeDtypeStruct((bsz, seq, B_WIDTH), BF16),
        grid=(seq // c,),
        in_specs=[
            pl.BlockSpec((bsz, c, B_WIDTH), tok),
            pl.BlockSpec((bsz, c, B_WIDTH), tok),
            pl.BlockSpec((bsz, c, B_WIDTH), tok),
            pl.BlockSpec((bsz, c, B_WIDTH), lambda i: (0, i, COL_Z // B_WIDTH)),
            pl.BlockSpec((bsz, c, LANES), tok),
            pl.BlockSpec((bsz, LANES, c), lambda i: (0, 0, i)),
            pl.BlockSpec((1, B_HEAD_DIM), lambda i: (0, 0)),
        ],
        out_specs=pl.BlockSpec((bsz, c, B_WIDTH), tok),
        scratch_shapes=[pltpu.VMEM((bsz * B_HEADS, B_HEAD_DIM, B_HEAD_DIM), F32)],
        compiler_params=_params("arbitrary"),
        name="gdn_chunk",
    )(per_seq(q), per_seq(k), per_seq(v), per_seq(proj), per_seq(gcol), grow, ng)
    return out.reshape(bsz * seq, B_WIDTH)


def _out_ln_body(h_ref, oa_ref, ob_ref, wa_ref, wb_ref, g_ref, b_ref, o_ref, *, sub):
    spans = [slice(r, r + sub) for r in range(0, h_ref.shape[0], sub)]
    mixes = [_dot(oa_ref[s, :], wa_ref[...]) + _dot(ob_ref[s, :], wb_ref[...]) for s in spans]
    for s, mix in zip(spans, mixes):
        o_ref[s, :] = _layer_norm(DEEPNORM_ALPHA * h_ref[s, :] + mix, g_ref[...], b_ref[...])


def _out_ln(h, oa, ob, w, g, b, *, tm=1024, sub=256):
    t, d = h.shape
    da, db = oa.shape[1], ob.shape[1]
    assert da == db and w.shape == (da + db, d)
    return pl.pallas_call(
        functools.partial(_out_ln_body, sub=sub),
        out_shape=jax.ShapeDtypeStruct((t, d), F32),
        grid=(t // tm,),
        in_specs=[
            pl.BlockSpec((tm, d), lambda i: (i, 0)),
            pl.BlockSpec((tm, da), lambda i: (i, 0)),
            pl.BlockSpec((tm, db), lambda i: (i, 0)),
            pl.BlockSpec((da, d), lambda i: (0, 0), pipeline_mode=pl.Buffered(1)),
            pl.BlockSpec((db, d), lambda i: (1, 0), pipeline_mode=pl.Buffered(1)),
            pl.BlockSpec((1, d), lambda i: (0, 0)),
            pl.BlockSpec((1, d), lambda i: (0, 0)),
        ],
        out_specs=pl.BlockSpec((tm, d), lambda i: (i, 0)),
        compiler_params=_params("parallel"),
        name="out_ln",
    )(h, oa, ob, w, w, g, b)


def _regroup_w_in(w_in):
    cuts = [0, Q_LORA, KV_LORA, IDX_DIM, IDX_HEADS, B_WIDTH, B_WIDTH, B_WIDTH, B_WIDTH, B_HEADS, B_HEADS]
    offs = [sum(cuts[:n + 1]) for n in range(len(cuts))]
    w16 = w_in.astype(BF16)
    part = lambda lo, hi: w16[:, offs[lo]:offs[hi]]
    pad = jnp.zeros((w_in.shape[0], LANES - IDX_HEADS - 2 * B_HEADS), BF16)
    small = jnp.concatenate([part(0, 3), part(3, 4), part(8, 10), pad], axis=1)
    return part(4, 8), small


def _lane_pad(vec, offset):
    out = jnp.zeros((1, LANES), F32)
    return out.at[0, offset:offset + vec.shape[0]].set(vec.astype(F32))


def kernel(x, ffn1_w_gate, ffn1_w_up, ffn1_w_down, ln1_g, ln1_b, w_in, q_norm_g, kv_norm_g, w_uq, w_uk, w_uv, w_q_idx, k_idx_ln_g, k_idx_ln_b, conv_w, a_log, dt_bias, o_norm_g, w_out, ln2_g, ln2_b, ffn2_w_gate, ffn2_w_up, ffn2_w_down, ln3_g, ln3_b):
    bsz, seq, d = x.shape
    assert seq % S_TILE == 0 and seq % GDN_CHUNK == 0 and x.dtype == F32
    assert seq // BF16_ROWS <= 256, "packed bf16 hit counts must stay exactly representable"
    t = bsz * seq
    row = lambda p: p.reshape(1, -1).astype(F32)
    h = x.reshape(t, d)
    for l in range(DEPTH):
        h = _ffn_ln(h, ffn1_w_gate[l], ffn1_w_up[l], ffn1_w_down[l], row(ln1_g[l]), row(ln1_b[l]))
        proj = _in_proj(h, _regroup_w_in(w_in[l]))
        qlat, qidx, ckv, ckvt, kidx, wt = _dsa_prep(
            proj, row(q_norm_g[l]), row(kv_norm_g[l]),
            w_uq[l].reshape(Q_LORA, A_HEADS * A_HEAD_DIM).astype(BF16), w_uk[l].astype(BF16),
            w_q_idx[l].astype(BF16), row(k_idx_ln_g[l]), row(k_idx_ln_b[l]))
        o_a = _dsa_attn(qlat, qidx, wt, kidx, ckv, ckvt, w_uv[l].astype(BF16), bsz=bsz, seq=seq)
        gq, gk, gv, gcol, grow = _gdn_prep(proj, conv_w[l].astype(F32), _lane_pad(a_log[l], SMALL_A),
                                           _lane_pad(dt_bias[l], SMALL_A), bsz=bsz, seq=seq)
        o_b = _gdn_chunk(gq, gk, gv, proj, gcol, grow, row(o_norm_g[l]), bsz=bsz, seq=seq)
        h = _out_ln(h, o_a, o_b, w_out[l].astype(BF16), row(ln2_g[l]), row(ln2_b[l]))
        h = _ffn_ln(h, ffn2_w_gate[l], ffn2_w_up[l], ffn2_w_down[l], row(ln3_g[l]), row(ln3_b[l]))
    return h.reshape(bsz, seq, d)
```

```python
import functools
import math

import jax
import jax.numpy as jnp
from jax import lax
from jax.experimental import pallas as pl
from jax.experimental.pallas import tpu as pltpu

F32 = jnp.float32
BF16 = jnp.bfloat16

A_HEADS = 8
A_HEAD_DIM = 128
Q_LORA = 512
KV_LORA = 256
IDX_HEADS = 16
IDX_DIM = 128
TOPK_MAX = 256
B_HEADS = 8
B_HEAD_DIM = 128
B_WIDTH = B_HEADS * B_HEAD_DIM
CONV_WIDTH = 4
DEPTH = 1
DEEPNORM_ALPHA = (2.0 * DEPTH) ** 0.25
EPS = 1e-6

LANES = 128
SUBLANES = 8
BF16_ROWS = 16
VMEM_LIMIT_BYTES = 62 * 1024 * 1024

COL_QKV = 0
COL_Z = 3 * B_WIDTH
COL_CQ = COL_Z + B_WIDTH
COL_CKV = COL_CQ + Q_LORA
COL_KIDX = COL_CKV + KV_LORA
COL_SMALL = COL_KIDX + IDX_DIM
D_IN_PAD = COL_SMALL + LANES
SMALL_W = 0
SMALL_A = IDX_HEADS
SMALL_B = IDX_HEADS + B_HEADS

Q_BLOCK = 512
S_TILE = 512
IDX_GROUP = 1
ATT_GROUP = 1
ATT_STAGE = A_HEADS // ATT_GROUP
GDN_CHUNK = 128
INT_MIN = -2 ** 31
F32_LOWEST = -3.4028234663852886e38
LOG2E = math.log2(math.e)


def _dot(a, b):
    return jnp.dot(a, b, preferred_element_type=F32)


def _dot_nt(a, b):
    return lax.dot_general(a, b, (((1,), (1,)), ((), ())), preferred_element_type=F32)


def _sigmoid(x):
    return 1.0 / (1.0 + jnp.exp(-x))


def _layer_norm(y, g, b):
    mu = jnp.mean(y, axis=-1, keepdims=True)
    yc = y - mu
    var = jnp.mean(yc * yc, axis=-1, keepdims=True)
    return yc * lax.rsqrt(var + EPS) * g + b


def _params(*sem):
    return pltpu.CompilerParams(dimension_semantics=sem, vmem_limit_bytes=VMEM_LIMIT_BYTES)


def _ffn_ln_body(x_ref, wg_ref, wu_ref, wd_ref, g_ref, b_ref, o_ref, xb_ref):
    j = pl.program_id(1)

    @pl.when(j == 0)
    def _():
        xb_ref[...] = x_ref[...].astype(BF16)
        o_ref[...] = jnp.zeros_like(o_ref)

    xb = xb_ref[...]
    gate = _dot(xb, wg_ref[...].astype(BF16))
    up = _dot(xb, wu_ref[...].astype(BF16))
    act = gate * _sigmoid(gate) * up
    o_ref[...] += _dot(act.astype(BF16), wd_ref[...].astype(BF16))

    @pl.when(j == pl.num_programs(1) - 1)
    def _():
        y = DEEPNORM_ALPHA * x_ref[...] + 0.5 * o_ref[...]
        o_ref[...] = _layer_norm(y, g_ref[...], b_ref[...])


def _ffn_ln(x, wg, wu, wd, g, b, *, tm=1024, tf=256):
    t, d = x.shape
    f = wg.shape[1]
    return pl.pallas_call(
        _ffn_ln_body,
        out_shape=jax.ShapeDtypeStruct((t, d), F32),
        grid=(t // tm, f // tf),
        in_specs=[
            pl.BlockSpec((tm, d), lambda i, j: (i, 0)),
            pl.BlockSpec((d, tf), lambda i, j: (0, j)),
            pl.BlockSpec((d, tf), lambda i, j: (0, j)),
            pl.BlockSpec((tf, d), lambda i, j: (j, 0)),
            pl.BlockSpec((1, d), lambda i, j: (0, 0)),
            pl.BlockSpec((1, d), lambda i, j: (0, 0)),
        ],
        out_specs=pl.BlockSpec((tm, d), lambda i, j: (i, 0)),
        scratch_shapes=[pltpu.VMEM((tm, d), BF16)],
        compiler_params=_params("parallel", "arbitrary"),
        name="ffn_ln",
    )(x, wg, wu, wd, g, b)


def _in_proj_body(x_ref, *refs, tn):
    w_refs, o_ref = refs[:-1], refs[-1]
    xb = x_ref[...].astype(BF16)
    col = 0
    for w_ref in w_refs:
        for c in range(0, w_ref.shape[1], tn):
            o_ref[:, col + c:col + c + tn] = _dot(xb, w_ref[:, c:c + tn])
        col += w_ref.shape[1]


def _in_proj(x, weights, *, tm=512, tn=1024):
    t, d = x.shape
    n = sum(w.shape[1] for w in weights)
    return pl.pallas_call(
        functools.partial(_in_proj_body, tn=tn),
        out_shape=jax.ShapeDtypeStruct((t, n), F32),
        grid=(t // tm,),
        in_specs=[pl.BlockSpec((tm, d), lambda i: (i, 0))] + [
            pl.BlockSpec(w.shape, lambda i: (0, 0), pipeline_mode=pl.Buffered(1)) for w in weights],
        out_specs=pl.BlockSpec((tm, n), lambda i: (i, 0)),
        compiler_params=_params("parallel"),
        name="in_proj",
    )(x, *weights)


def _dsa_prep_body(cq_ref, ckv_ref, kidx_ref, small_ref, qg_ref, kvg_ref, wuq_ref, wuk_ref,
                   wqi_ref, lng_ref, lnb_ref,
                   qlat_ref, qidx_ref, ckv_o_ref, ckvt_o_ref, kidx_o_ref, wt_o_ref):
    tm = cq_ref.shape[0]
    nb = tm // Q_BLOCK

    cq = cq_ref[...]
    cq = cq * lax.rsqrt(jnp.mean(cq * cq, axis=-1, keepdims=True) + EPS) * qg_ref[...]
    cqb = cq.astype(BF16)

    q = _dot(cqb, wuq_ref[...]).astype(BF16)
    for h in range(A_HEADS):
        ql = _dot(q[:, h * A_HEAD_DIM:(h + 1) * A_HEAD_DIM], wuk_ref[h])
        ql = ql * (A_HEAD_DIM ** -0.5 * LOG2E)
        qlat_ref[:, h] = ql.astype(BF16).reshape(nb, Q_BLOCK, KV_LORA)

    qi = _dot(cqb, wqi_ref[...]).astype(BF16)
    for h in range(IDX_HEADS):
        qidx_ref[:, h] = qi[:, h * IDX_DIM:(h + 1) * IDX_DIM].reshape(nb, Q_BLOCK, IDX_DIM)

    ckv = ckv_ref[...]
    ckv = ckv * lax.rsqrt(jnp.mean(ckv * ckv, axis=-1, keepdims=True) + EPS) * kvg_ref[...]
    ckv_o_ref[...] = ckv.astype(BF16)
    ckvt_o_ref[0, :KV_LORA] = ckv.T.astype(BF16)
    ckvt_o_ref[0, KV_LORA:] = jnp.ones((BF16_ROWS, tm), BF16)

    kidx_o_ref[...] = _layer_norm(kidx_ref[...], lng_ref[...], lnb_ref[...]).astype(BF16)

    wt = small_ref[...].T
    wt_o_ref[...] = wt[SMALL_W:SMALL_W + IDX_HEADS, :] * (IDX_HEADS ** -0.5 * IDX_DIM ** -0.5)


def _dsa_prep(proj, qg, kvg, wuq, wuk, wqi, lng, lnb, *, tm=S_TILE):
    t = proj.shape[0]
    nq = t // Q_BLOCK
    full = lambda *shape: pl.BlockSpec(shape, lambda i: (0,) * len(shape))
    return pl.pallas_call(
        _dsa_prep_body,
        out_shape=(
            jax.ShapeDtypeStruct((nq, A_HEADS, Q_BLOCK, KV_LORA), BF16),
            jax.ShapeDtypeStruct((nq, IDX_HEADS, Q_BLOCK, IDX_DIM), BF16),
            jax.ShapeDtypeStruct((t, KV_LORA), BF16),
            jax.ShapeDtypeStruct((t // tm, KV_LORA + BF16_ROWS, tm), BF16),
            jax.ShapeDtypeStruct((t, IDX_DIM), BF16),
            jax.ShapeDtypeStruct((IDX_HEADS, t), F32),
        ),
        grid=(t // tm,),
        in_specs=[
            pl.BlockSpec((tm, Q_LORA), lambda i: (i, COL_CQ // Q_LORA)),
            pl.BlockSpec((tm, KV_LORA), lambda i: (i, COL_CKV // KV_LORA)),
            pl.BlockSpec((tm, IDX_DIM), lambda i: (i, COL_KIDX // IDX_DIM)),
            pl.BlockSpec((tm, LANES), lambda i: (i, COL_SMALL // LANES)),
            full(1, Q_LORA), full(1, KV_LORA),
            full(Q_LORA, A_HEADS * A_HEAD_DIM),
            full(A_HEADS, A_HEAD_DIM, KV_LORA),
            full(Q_LORA, IDX_HEADS * IDX_DIM),
            full(1, IDX_DIM), full(1, IDX_DIM),
        ],
        out_specs=(
            pl.BlockSpec((tm // Q_BLOCK, A_HEADS, Q_BLOCK, KV_LORA), lambda i: (i, 0, 0, 0)),
            pl.BlockSpec((tm // Q_BLOCK, IDX_HEADS, Q_BLOCK, IDX_DIM), lambda i: (i, 0, 0, 0)),
            pl.BlockSpec((tm, KV_LORA), lambda i: (i, 0)),
            pl.BlockSpec((1, KV_LORA + BF16_ROWS, tm), lambda i: (i, 0, 0)),
            pl.BlockSpec((tm, IDX_DIM), lambda i: (i, 0)),
            pl.BlockSpec((IDX_HEADS, tm), lambda i: (0, i)),
        ),
        compiler_params=_params("parallel"),
        name="dsa_prep",
    )(proj, proj, proj, proj, qg, kvg, wuq, wuk, wqi, lng, lnb)


def _key_to_f32(key):
    return pltpu.bitcast(key ^ ((key >> 31) & 0x7FFFFFFF), F32)


def _dsa_attn_body(qidx_ref, qlat_ref, wt_ref, kidx_ref, ckv_ref, ckvt_ref, wuv_ref, o_ref,
                   sc_ref, sh_ref, bias_ref, m_ref, l_ref, acc_ref, *, n_sel):
    i = pl.program_id(1)
    t0 = i * Q_BLOCK
    n_tiles = (t0 + Q_BLOCK + S_TILE - 1) // S_TILE

    t_lane = t0 + lax.broadcasted_iota(jnp.int32, (S_TILE, Q_BLOCK), 1)
    s_row = lax.broadcasted_iota(jnp.int32, (S_TILE, Q_BLOCK), 0)

    def tile_rows(j):
        return pl.ds(pl.multiple_of(j * S_TILE, S_TILE), S_TILE)

    wt = wt_ref[...]

    def score_tile(j, carry):
        k_tile = kidx_ref[tile_rows(j), :]
        sc = jnp.zeros((S_TILE, Q_BLOCK), F32)
        for g in range(IDX_HEADS // IDX_GROUP):
            qg = qidx_ref[0, g * IDX_GROUP:(g + 1) * IDX_GROUP].reshape(IDX_GROUP * Q_BLOCK, IDX_DIM)
            logits = _dot_nt(k_tile, qg)
            for u in range(IDX_GROUP):
                h = g * IDX_GROUP + u
                sc = sc + wt[h:h + 1, :] * jnp.maximum(logits[:, u * Q_BLOCK:(u + 1) * Q_BLOCK], 0.0)
        sc = jnp.where(s_row + j * S_TILE <= t_lane, sc, -jnp.inf)
        sc_ref[tile_rows(j), :] = sc
        sh_ref[tile_rows(j), :] = sc.astype(BF16)
        return carry

    lax.fori_loop(0, n_tiles, score_tile, 0)

    def tree_sum(parts, chains=8):
        sums = list(parts[:chains])
        for n, piece in enumerate(parts[chains:]):
            sums[n % chains] = sums[n % chains] + piece
        while len(sums) > 1:
            sums = [a + b for a, b in zip(sums[0::2], sums[1::2])]
        return sums[0]

    def count(pred):
        def body(j, acc):
            hit = jnp.where(pred(sc_ref[tile_rows(j), :]), 1, 0)
            return acc + tree_sum([hit[r:r + SUBLANES, :] for r in range(0, S_TILE, SUBLANES)])

        acc = lax.fori_loop(0, n_tiles, body, jnp.zeros((SUBLANES, Q_BLOCK), jnp.int32))
        return jnp.sum(acc, axis=0, keepdims=True)

    one16, zero16 = jnp.ones((), BF16), jnp.zeros((), BF16)

    def count_rounded_ge(key16):
        cand = _key_to_f32((key16 << 16) | ((key16 >> 31) & 0xFFFF)).astype(BF16)

        def body(j, acc):
            hit = jnp.where(sh_ref[tile_rows(j), :] >= cand, one16, zero16)
            return acc + tree_sum([hit[r:r + BF16_ROWS, :] for r in range(0, S_TILE, BF16_ROWS)])

        acc = lax.fori_loop(0, n_tiles, body, jnp.zeros((BF16_ROWS, Q_BLOCK), BF16))
        return jnp.sum(acc.astype(F32), axis=0, keepdims=True).astype(jnp.int32)

    def try_key16(cand16, key16):
        return jnp.where(count_rounded_ge(cand16) >= n_sel, cand16, key16)

    zero = jnp.zeros((1, Q_BLOCK), jnp.int32)
    key16 = try_key16(zero, zero - 2 ** 15)
    key16 = lax.fori_loop(0, 15, lambda b, k: try_key16(k | (1 << (14 - b)), k), key16)

    coarse = (key16 << 16) | ((key16 >> 31) & 0xFFFF)
    base = coarse - 2 ** 15

    def try_offset(cand_off, off, key, n_ge):
        cand_key = base + cand_off
        cand = _key_to_f32(cand_key)
        n = count(lambda sc: sc >= cand)
        ok = n >= n_sel
        return jnp.where(ok, cand_off, off), jnp.where(ok, cand_key, key), jnp.where(ok, n, n_ge)

    carry = try_offset(zero, zero, zero + INT_MIN, zero)
    _, key, n_ge = lax.fori_loop(0, 17, lambda b, c: try_offset(c[0] | (1 << (16 - b)), *c), carry)
    thr = _key_to_f32(key)
    thr = jnp.maximum(jnp.where(thr != thr, F32_LOWEST, thr), F32_LOWEST)

    @pl.when(jnp.max(n_ge) > n_sel)
    def _():
        need = (n_sel - count(lambda sc: sc > thr)).astype(F32)
        r = lax.broadcasted_iota(jnp.int32, (S_TILE, S_TILE), 0)
        c = lax.broadcasted_iota(jnp.int32, (S_TILE, S_TILE), 1)
        lower_incl = jnp.where(c <= r, 1.0, 0.0).astype(BF16)

        def body(j, run):
            sc = sc_ref[tile_rows(j), :]
            tied = sc == thr
            rank = run + _dot(lower_incl, jnp.where(tied, 1.0, 0.0).astype(BF16))
            sc_ref[tile_rows(j), :] = jnp.where(tied & (rank > need), -jnp.inf, sc)
            return rank[S_TILE - 1:S_TILE, :]

        lax.fori_loop(0, n_tiles, body, jnp.zeros((1, Q_BLOCK), F32))

    gw = ATT_GROUP * Q_BLOCK
    groups = range(A_HEADS // ATT_GROUP)
    lane = lax.broadcasted_iota(jnp.int32, (1, gw), 1)
    slope_rows = []
    for g in groups:
        row_g = jnp.zeros((1, gw), F32)
        for u in range(ATT_GROUP):
            slope = 2.0 ** (-8.0 * (g * ATT_GROUP + u + 1) / A_HEADS) * LOG2E
            row_g = jnp.where(lane >= u * Q_BLOCK, slope, row_g)
        slope_rows.append(row_g)

    @pl.when(i == 0)
    def _():
        s_rel = lax.broadcasted_iota(jnp.int32, (S_TILE, gw), 0).astype(F32)
        for g in groups:
            bias_ref[g] = slope_rows[g] * s_rel

    m_ref[...] = jnp.full(m_ref.shape, -1e30, F32)
    l_ref[...] = jnp.zeros_like(l_ref)
    acc_ref[...] = jnp.zeros_like(acc_ref)

    def att_tile(j, carry):
        drop = jnp.where(sc_ref[tile_rows(j), :] >= thr, 0.0, -jnp.inf)
        drop = jnp.concatenate([drop] * ATT_GROUP, axis=1)
        ckv_tile = ckv_ref[tile_rows(j), :]
        ckvt_tile = ckvt_ref[j]
        off = (j * S_TILE - t0).astype(F32)
        for first in range(0, len(groups), ATT_STAGE):
            batch = groups[first:first + ATT_STAGE]
            logits = {g: _dot_nt(ckv_tile,
                                 qlat_ref[0, g * ATT_GROUP:(g + 1) * ATT_GROUP].reshape(gw, KV_LORA))
                      for g in batch}
            probs, alphas = {}, {}
            for g in batch:
                a = logits[g] + bias_ref[g] + drop
                shift = slope_rows[g] * off
                m_old = m_ref[g:g + 1, :]
                m_new = jnp.maximum(m_old, jnp.max(a, axis=0, keepdims=True) + shift)
                alphas[g] = jnp.exp2(m_old - m_new)
                probs[g] = jnp.exp2(a - (m_new - shift)).astype(BF16)
                m_ref[g:g + 1, :] = m_new
            for g in batch:
                product = _dot(ckvt_tile, probs[g])
                acc_ref[g] = alphas[g] * acc_ref[g] + product[:KV_LORA]
                l_ref[g:g + 1, :] = alphas[g] * l_ref[g:g + 1, :] + product[KV_LORA:KV_LORA + 1]
        return carry

    lax.fori_loop(0, n_tiles, att_tile, 0)

    for g in groups:
        o_lat_t = acc_ref[g] * (1.0 / l_ref[g:g + 1, :])
        for u in range(ATT_GROUP):
            h = g * ATT_GROUP + u
            o_lat = o_lat_t[:, u * Q_BLOCK:(u + 1) * Q_BLOCK].T.astype(BF16)
            o_ref[:, h * A_HEAD_DIM:(h + 1) * A_HEAD_DIM] = _dot(o_lat, wuv_ref[h]).astype(BF16)


def _dsa_attn(qlat, qidx, wt, kidx, ckv, ckvt, wuv, *, bsz, seq):
    nq = seq // Q_BLOCK
    ns = seq // S_TILE
    n_sel = min(TOPK_MAX, seq // 4)
    ng, gw = A_HEADS // ATT_GROUP, ATT_GROUP * Q_BLOCK
    return pl.pallas_call(
        functools.partial(_dsa_attn_body, n_sel=n_sel),
        out_shape=jax.ShapeDtypeStruct((bsz * seq, A_HEADS * A_HEAD_DIM), BF16),
        grid=(bsz, nq),
        in_specs=[
            pl.BlockSpec((1, IDX_HEADS, Q_BLOCK, IDX_DIM), lambda b, i: (b * nq + i, 0, 0, 0)),
            pl.BlockSpec((1, A_HEADS, Q_BLOCK, KV_LORA), lambda b, i: (b * nq + i, 0, 0, 0)),
            pl.BlockSpec((IDX_HEADS, Q_BLOCK), lambda b, i: (0, b * nq + i)),
            pl.BlockSpec((seq, IDX_DIM), lambda b, i: (b, 0)),
            pl.BlockSpec((seq, KV_LORA), lambda b, i: (b, 0)),
            pl.BlockSpec((ns, KV_LORA + BF16_ROWS, S_TILE), lambda b, i: (b, 0, 0)),
            pl.BlockSpec((A_HEADS, KV_LORA, A_HEAD_DIM), lambda b, i: (0, 0, 0)),
        ],
        out_specs=pl.BlockSpec((Q_BLOCK, A_HEADS * A_HEAD_DIM), lambda b, i: (b * nq + i, 0)),
        scratch_shapes=[
            pltpu.VMEM((seq, Q_BLOCK), F32),
            pltpu.VMEM((seq, Q_BLOCK), BF16),
            pltpu.VMEM((ng, S_TILE, gw), F32),
            pltpu.VMEM((ng, gw), F32),
            pltpu.VMEM((ng, gw), F32),
            pltpu.VMEM((ng, KV_LORA, gw), F32),
        ],
        compiler_params=_params("parallel", "arbitrary"),
        name="dsa_attn",
    )(qidx, qlat, wt, kidx, ckv, ckvt, wuv)


def _gdn_prep_body(x_ref, halo_ref, small_ref, cw_ref, alog_ref, dtb_ref,
                   q_ref, k_ref, v_ref, gcol_ref, grow_ref):
    ts = x_ref.shape[0]
    first = pl.program_id(1) == 0
    x = x_ref[...]
    xb = x.astype(BF16)
    halo = jnp.where(first, 0.0, halo_ref[...]).astype(BF16).astype(F32)
    top = jnp.concatenate([halo, xb[:SUBLANES].astype(F32)], axis=0)
    r = lax.broadcasted_iota(jnp.int32, (ts, ts), 0)
    c = lax.broadcasted_iota(jnp.int32, (ts, ts), 1)
    cw = cw_ref[...]
    y = cw[CONV_WIDTH - 1:CONV_WIDTH, :] * x
    for d in range(1, CONV_WIDTH):
        moved = _dot(jnp.where(c == r - d, 1.0, 0.0).astype(BF16), xb)
        moved = jnp.concatenate([top[SUBLANES - d:2 * SUBLANES - d], moved[SUBLANES:]], axis=0)
        y = y + cw[CONV_WIDTH - 1 - d:CONV_WIDTH - d, :] * moved
    y = y * _sigmoid(y)

    for h in range(B_HEADS):
        lo, hi = h * B_HEAD_DIM, (h + 1) * B_HEAD_DIM
        qh = y[:, lo:hi]
        q_ref[:, lo:hi] = qh * lax.rsqrt(jnp.sum(qh * qh, axis=-1, keepdims=True) + EPS) * (
            B_HEAD_DIM ** -0.5)
        kh = y[:, B_WIDTH + lo:B_WIDTH + hi]
        k_ref[:, lo:hi] = kh * lax.rsqrt(jnp.sum(kh * kh, axis=-1, keepdims=True) + EPS)
    v_ref[...] = y[:, 2 * B_WIDTH:]

    small = small_ref[...]
    pre = small + dtb_ref[...]
    softplus = jnp.maximum(pre, 0.0) + jnp.log1p(jnp.exp(-jnp.abs(pre)))
    g = -jnp.exp(alog_ref[...]) * softplus
    row = lax.broadcasted_iota(jnp.int32, g.shape, 0) & (GDN_CHUNK - 1)
    shift = 1
    while shift < GDN_CHUNK:
        g = g + jnp.where(row >= shift, pltpu.roll(g, shift, 0), 0.0)
        shift *= 2
    lane = lax.broadcasted_iota(jnp.int32, g.shape, 1)
    is_a = (lane >= SMALL_A) & (lane < SMALL_A + B_HEADS)
    gb = jnp.where(is_a, g, _sigmoid(small))
    gcol_ref[...] = gb
    grow_ref[0] = gb.T


def _gdn_prep(proj, conv_w, alog_p, dtb_p, *, bsz, seq, ts=256):
    t = bsz * seq
    ns = seq // ts
    c = 3 * B_WIDTH
    return pl.pallas_call(
        _gdn_prep_body,
        out_shape=(
            jax.ShapeDtypeStruct((t, B_WIDTH), F32),
            jax.ShapeDtypeStruct((t, B_WIDTH), F32),
            jax.ShapeDtypeStruct((t, B_WIDTH), F32),
            jax.ShapeDtypeStruct((t, LANES), F32),
            jax.ShapeDtypeStruct((bsz, LANES, seq), F32),
        ),
        grid=(bsz, ns),
        in_specs=[
            pl.BlockSpec((ts, c), lambda b, i: (b * ns + i, COL_QKV // c)),
            pl.BlockSpec((SUBLANES, c),
                         lambda b, i: (jnp.maximum((b * ns + i) * (ts // SUBLANES) - 1, 0), 0)),
            pl.BlockSpec((ts, LANES), lambda b, i: (b * ns + i, COL_SMALL // LANES)),
            pl.BlockSpec((CONV_WIDTH, c), lambda b, i: (0, 0)),
            pl.BlockSpec((1, LANES), lambda b, i: (0, 0)),
            pl.BlockSpec((1, LANES), lambda b, i: (0, 0)),
        ],
        out_specs=(
            pl.BlockSpec((ts, B_WIDTH), lambda b, i: (b * ns + i, 0)),
            pl.BlockSpec((ts, B_WIDTH), lambda b, i: (b * ns + i, 0)),
            pl.BlockSpec((ts, B_WIDTH), lambda b, i: (b * ns + i, 0)),
            pl.BlockSpec((ts, LANES), lambda b, i: (b * ns + i, 0)),
            pl.BlockSpec((1, LANES, ts), lambda b, i: (b, 0, i)),
        ),
        compiler_params=_params("parallel", "arbitrary"),
        name="gdn_prep",
    )(proj, proj, proj, conv_w, alog_p, dtb_p)


def _dot1(a, b):
    return _dot(a.astype(BF16), b.astype(BF16))


def _each(fn, *lists):
    return [fn(*args) for args in zip(*lists)]


def _unit_lower_inverses(mats, row, col):
    base_log2 = 4
    eye = jnp.where(row == col, 1.0, 0.0)
    in_block = (row >> base_log2) == (col >> base_log2)
    d = [jnp.where(in_block, a, 0.0) for a in mats]
    x = [eye - dh for dh in d]
    p = _each(_dot1, d, d)
    for step in range(base_log2 - 1):
        x = _each(lambda xh, th: xh + th, x, _each(_dot1, x, p))
        if step < base_log2 - 2:
            p = _each(_dot1, p, p)
    sh = base_log2
    while (1 << sh) < GDN_CHUNK:
        same_parent = (row >> (sh + 1)) == (col >> (sh + 1))
        same_block = (row >> sh) == (col >> sh)
        quad = [jnp.where(same_block, 0.0, jnp.where(same_parent, a, 0.0)) for a in mats]
        x = _each(lambda xh, th: xh - th, x, _each(_dot1, x, _each(_dot1, quad, x)))
        sh += 1
    return x


def _gdn_chunk_body(q_ref, k_ref, v_ref, z_ref, gcol_ref, grow_ref, ng_ref, o_ref, state_ref):
    @pl.when(pl.program_id(0) == 0)
    def _():
        state_ref[...] = jnp.zeros_like(state_ref)

    c = GDN_CHUNK
    bsz = q_ref.shape[0]
    units = [(b, h) for b in range(bsz) for h in range(B_HEADS)]
    span = lambda h: slice(h * B_HEAD_DIM, (h + 1) * B_HEAD_DIM)
    row = lax.broadcasted_iota(jnp.int32, (c, c), 0)
    col = lax.broadcasted_iota(jnp.int32, (c, c), 1)
    gcol = [gcol_ref[b] for b in range(bsz)]
    grow = [grow_ref[b] for b in range(bsz)]
    e_cum = [jnp.exp(g) for g in gcol]
    e_rest = [jnp.exp(g[c - 1:c, :] - g) for g in gcol]
    e_last = [jnp.exp(g[:, c - 1:c]) for g in grow]
    col_a = lambda tab, b, h: tab[b][:, SMALL_A + h:SMALL_A + h + 1]

    q = [q_ref[b, :, span(h)] for b, h in units]
    k = [k_ref[b, :, span(h)] for b, h in units]
    v = [v_ref[b, :, span(h)] for b, h in units]
    beta = [gcol[b][:, SMALL_B + h:SMALL_B + h + 1] for b, h in units]
    decay = [jnp.exp(jnp.where(row >= col, col_a(gcol, b, h) - grow[b][SMALL_A + h:SMALL_A + h + 1, :],
                               -jnp.inf)) for b, h in units]
    k_beta = _each(lambda kh, bh: kh * bh, k, beta)
    k16 = [kh.astype(BF16) for kh in k]
    kk = _each(_dot_nt, [kb.astype(BF16) for kb in k_beta], k16)
    qk = _each(_dot_nt, [qh.astype(BF16) for qh in q], k16)
    a_mat = _each(lambda m, dh: jnp.where(row > col, m * dh, 0.0), kk, decay)
    attn = _each(lambda m, dh: (m * dh).astype(BF16), qk, decay)
    t_inv = _unit_lower_inverses(a_mat, row, col)
    rhs = [jnp.concatenate([v[u] * beta[u], k_beta[u] * col_a(e_cum, b, h)], axis=1)
           for u, (b, h) in enumerate(units)]
    sol = _each(_dot1, t_inv, rhs)
    q_dec = [(q[u] * col_a(e_cum, b, h)).astype(BF16) for u, (b, h) in enumerate(units)]
    k_dec_t = [(k[u] * col_a(e_rest, b, h)).T.astype(BF16) for u, (b, h) in enumerate(units)]

    state = [state_ref[u] for u in range(len(units))]
    s16 = [s.astype(BF16) for s in state]
    v_new = [sol[u][:, :B_HEAD_DIM] - _dot(sol[u][:, B_HEAD_DIM:].astype(BF16), s16[u])
             for u in range(len(units))]
    v16 = [x.astype(BF16) for x in v_new]
    for u, (b, h) in enumerate(units):
        state_ref[u] = state[u] * e_last[b][SMALL_A + h:SMALL_A + h + 1, :] + _dot(k_dec_t[u], v16[u])
    out = [_dot(q_dec[u], s16[u]) + _dot(attn[u], v16[u]) for u in range(len(units))]
    for u, (b, h) in enumerate(units):
        o = out[u]
        y = o * lax.rsqrt(jnp.mean(o * o, axis=-1, keepdims=True) + EPS) * ng_ref[...]
        zh = z_ref[b, :, span(h)]
        o_ref[b, :, span(h)] = (y * (zh * _sigmoid(zh))).astype(BF16)


def _gdn_chunk(q, k, v, proj, gcol, grow, ng, *, bsz, seq):
    c = GDN_CHUNK
    per_seq = lambda a: a.reshape(bsz, seq, a.shape[-1])
    tok = lambda i: (0, i, 0)
    out = pl.pallas_call(
        _gdn_chunk_body,
        out_shape=jax.ShapeDtypeStruct((bsz, seq, B_WIDTH), BF16),
        grid=(seq // c,),
        in_specs=[
            pl.BlockSpec((bsz, c, B_WIDTH), tok),
            pl.BlockSpec((bsz, c, B_WIDTH), tok),
            pl.BlockSpec((bsz, c, B_WIDTH), tok),
            pl.BlockSpec((bsz, c, B_WIDTH), lambda i: (0, i, COL_Z // B_WIDTH)),
            pl.BlockSpec((bsz, c, LANES), tok),
            pl.BlockSpec((bsz, LANES, c), lambda i: (0, 0, i)),
            pl.BlockSpec((1, B_HEAD_DIM), lambda i: (0, 0)),
        ],
        out_specs=pl.BlockSpec((bsz, c, B_WIDTH), tok),
        scratch_shapes=[pltpu.VMEM((bsz * B_HEADS, B_HEAD_DIM, B_HEAD_DIM), F32)],
        compiler_params=_params("arbitrary"),
        name="gdn_chunk",
    )(per_seq(q), per_seq(k), per_seq(v), per_seq(proj), per_seq(gcol), grow, ng)
    return out.reshape(bsz * seq, B_WIDTH)


def _out_ln_body(h_ref, oa_ref, ob_ref, wa_ref, wb_ref, g_ref, b_ref, o_ref, *, sub):
    spans = [slice(r, r + sub) for r in range(0, h_ref.shape[0], sub)]
    mixes = [_dot(oa_ref[s, :], wa_ref[...]) + _dot(ob_ref[s, :], wb_ref[...]) for s in spans]
    for s, mix in zip(spans, mixes):
        o_ref[s, :] = _layer_norm(DEEPNORM_ALPHA * h_ref[s, :] + mix, g_ref[...], b_ref[...])


def _out_ln(h, oa, ob, w, g, b, *, tm=1024, sub=256):
    t, d = h.shape
    da, db = oa.shape[1], ob.shape[1]
    assert da == db and w.shape == (da + db, d)
    return pl.pallas_call(
        functools.partial(_out_ln_body, sub=sub),
        out_shape=jax.ShapeDtypeStruct((t, d), F32),
        grid=(t // tm,),
        in_specs=[
            pl.BlockSpec((tm, d), lambda i: (i, 0)),
            pl.BlockSpec((tm, da), lambda i: (i, 0)),
            pl.BlockSpec((tm, db), lambda i: (i, 0)),
            pl.BlockSpec((da, d), lambda i: (0, 0), pipeline_mode=pl.Buffered(1)),
            pl.BlockSpec((db, d), lambda i: (1, 0), pipeline_mode=pl.Buffered(1)),
            pl.BlockSpec((1, d), lambda i: (0, 0)),
            pl.BlockSpec((1, d), lambda i: (0, 0)),
        ],
        out_specs=pl.BlockSpec((tm, d), lambda i: (i, 0)),
        compiler_params=_params("parallel"),
        name="out_ln",
    )(h, oa, ob, w, w, g, b)


def _regroup_w_in(w_in):
    cuts = [0, Q_LORA, KV_LORA, IDX_DIM, IDX_HEADS, B_WIDTH, B_WIDTH, B_WIDTH, B_WIDTH, B_HEADS, B_HEADS]
    offs = [sum(cuts[:n + 1]) for n in range(len(cuts))]
    w16 = w_in.astype(BF16)
    part = lambda lo, hi: w16[:, offs[lo]:offs[hi]]
    pad = jnp.zeros((w_in.shape[0], LANES - IDX_HEADS - 2 * B_HEADS), BF16)
    small = jnp.concatenate([part(0, 3), part(3, 4), part(8, 10), pad], axis=1)
    return part(4, 8), small


def _lane_pad(vec, offset):
    out = jnp.zeros((1, LANES), F32)
    return out.at[0, offset:offset + vec.shape[0]].set(vec.astype(F32))


def kernel(x, ffn1_w_gate, ffn1_w_up, ffn1_w_down, ln1_g, ln1_b, w_in, q_norm_g, kv_norm_g, w_uq, w_uk, w_uv, w_q_idx, k_idx_ln_g, k_idx_ln_b, conv_w, a_log, dt_bias, o_norm_g, w_out, ln2_g, ln2_b, ffn2_w_gate, ffn2_w_up, ffn2_w_down, ln3_g, ln3_b):
    bsz, seq, d = x.shape
    assert seq % S_TILE == 0 and seq % GDN_CHUNK == 0 and x.dtype == F32
    assert seq // BF16_ROWS <= 256, "packed bf16 hit counts must stay exactly representable"
    t = bsz * seq
    row = lambda p: p.reshape(1, -1).astype(F32)
    h = x.reshape(t, d)
    for l in range(DEPTH):
        h = _ffn_ln(h, ffn1_w_gate[l], ffn1_w_up[l], ffn1_w_down[l], row(ln1_g[l]), row(ln1_b[l]))
        proj = _in_proj(h, _regroup_w_in(w_in[l]))
        qlat, qidx, ckv, ckvt, kidx, wt = _dsa_prep(
            proj, row(q_norm_g[l]), row(kv_norm_g[l]),
            w_uq[l].reshape(Q_LORA, A_HEADS * A_HEAD_DIM).astype(BF16), w_uk[l].astype(BF16),
            w_q_idx[l].astype(BF16), row(k_idx_ln_g[l]), row(k_idx_ln_b[l]))
        o_a = _dsa_attn(qlat, qidx, wt, kidx, ckv, ckvt, w_uv[l].astype(BF16), bsz=bsz, seq=seq)
        gq, gk, gv, gcol, grow = _gdn_prep(proj, conv_w[l].astype(F32), _lane_pad(a_log[l], SMALL_A),
                                           _lane_pad(dt_bias[l], SMALL_A), bsz=bsz, seq=seq)
        o_b = _gdn_chunk(gq, gk, gv, proj, gcol, grow, row(o_norm_g[l]), bsz=bsz, seq=seq)
        h = _out_ln(h, o_a, o_b, w_out[l].astype(BF16), row(ln2_g[l]), row(ln2_b[l]))
        h = _ffn_ln(h, ffn2_w_gate[l], ffn2_w_up[l], ffn2_w_down[l], row(ln3_g[l]), row(ln3_b[l]))
    return h.reshape(bsz, seq, d)
```

```python
import functools
import math

import jax
import jax.numpy as jnp
from jax import lax
from jax.experimental import pallas as pl
from jax.experimental.pallas import tpu as pltpu

F32 = jnp.float32
BF16 = jnp.bfloat16

A_HEADS = 8
A_HEAD_DIM = 128
Q_LORA = 512
KV_LORA = 256
IDX_HEADS = 16
IDX_DIM = 128
TOPK_MAX = 256
B_HEADS = 8
B_HEAD_DIM = 128
B_WIDTH = B_HEADS * B_HEAD_DIM
CONV_WIDTH = 4
DEPTH = 1
DEEPNORM_ALPHA = (2.0 * DEPTH) ** 0.25
EPS = 1e-6

LANES = 128
SUBLANES = 8
BF16_ROWS = 16
VMEM_LIMIT_BYTES = 62 * 1024 * 1024

COL_QKV = 0
COL_Z = 3 * B_WIDTH
COL_CQ = COL_Z + B_WIDTH
COL_CKV = COL_CQ + Q_LORA
COL_KIDX = COL_CKV + KV_LORA
COL_SMALL = COL_KIDX + IDX_DIM
D_IN_PAD = COL_SMALL + LANES
SMALL_W = 0
SMALL_A = IDX_HEADS
SMALL_B = IDX_HEADS + B_HEADS

Q_BLOCK = 512
S_TILE = 512
GDN_CHUNK = 128
INT_MIN = -2 ** 31
F32_LOWEST = -3.4028234663852886e38
LOG2E = math.log2(math.e)


def _dot(a, b):
    return jnp.dot(a, b, preferred_element_type=F32)


def _dot_nt(a, b):
    return lax.dot_general(a, b, (((1,), (1,)), ((), ())), preferred_element_type=F32)


def _sigmoid(x):
    return 1.0 / (1.0 + jnp.exp(-x))


def _layer_norm(y, g, b):
    mu = jnp.mean(y, axis=-1, keepdims=True)
    yc = y - mu
    var = jnp.mean(yc * yc, axis=-1, keepdims=True)
    return yc * lax.rsqrt(var + EPS) * g + b


def _params(*sem):
    return pltpu.CompilerParams(dimension_semantics=sem, vmem_limit_bytes=VMEM_LIMIT_BYTES)


def _ffn_ln_body(x_ref, wg_ref, wu_ref, wd_ref, g_ref, b_ref, o_ref, xb_ref):
    j = pl.program_id(1)

    @pl.when(j == 0)
    def _():
        xb_ref[...] = x_ref[...].astype(BF16)
        o_ref[...] = jnp.zeros_like(o_ref)

    xb = xb_ref[...]
    gate = _dot(xb, wg_ref[...].astype(BF16))
    up = _dot(xb, wu_ref[...].astype(BF16))
    act = gate * _sigmoid(gate) * up
    o_ref[...] += _dot(act.astype(BF16), wd_ref[...].astype(BF16))

    @pl.when(j == pl.num_programs(1) - 1)
    def _():
        y = DEEPNORM_ALPHA * x_ref[...] + 0.5 * o_ref[...]
        o_ref[...] = _layer_norm(y, g_ref[...], b_ref[...])


def _ffn_ln(x, wg, wu, wd, g, b, *, tm=1024, tf=256):
    t, d = x.shape
    f = wg.shape[1]
    return pl.pallas_call(
        _ffn_ln_body,
        out_shape=jax.ShapeDtypeStruct((t, d), F32),
        grid=(t // tm, f // tf),
        in_specs=[
            pl.BlockSpec((tm, d), lambda i, j: (i, 0)),
            pl.BlockSpec((d, tf), lambda i, j: (0, j)),
            pl.BlockSpec((d, tf), lambda i, j: (0, j)),
            pl.BlockSpec((tf, d), lambda i, j: (j, 0)),
            pl.BlockSpec((1, d), lambda i, j: (0, 0)),
            pl.BlockSpec((1, d), lambda i, j: (0, 0)),
        ],
        out_specs=pl.BlockSpec((tm, d), lambda i, j: (i, 0)),
        scratch_shapes=[pltpu.VMEM((tm, d), BF16)],
        compiler_params=_params("parallel", "arbitrary"),
        name="ffn_ln",
    )(x, wg, wu, wd, g, b)


def _in_proj_body(x_ref, *refs, tn):
    w_refs, o_ref = refs[:-1], refs[-1]
    xb = x_ref[...].astype(BF16)
    col = 0
    for w_ref in w_refs:
        for c in range(0, w_ref.shape[1], tn):
            o_ref[:, col + c:col + c + tn] = _dot(xb, w_ref[:, c:c + tn])
        col += w_ref.shape[1]


def _in_proj(x, weights, *, tm=512, tn=1024):
    t, d = x.shape
    n = sum(w.shape[1] for w in weights)
    return pl.pallas_call(
        functools.partial(_in_proj_body, tn=tn),
        out_shape=jax.ShapeDtypeStruct((t, n), F32),
        grid=(t // tm,),
        in_specs=[pl.BlockSpec((tm, d), lambda i: (i, 0))] + [
            pl.BlockSpec(w.shape, lambda i: (0, 0), pipeline_mode=pl.Buffered(1)) for w in weights],
        out_specs=pl.BlockSpec((tm, n), lambda i: (i, 0)),
        compiler_params=_params("parallel"),
        name="in_proj",
    )(x, *weights)


def _dsa_prep_body(cq_ref, ckv_ref, kidx_ref, small_ref, qg_ref, kvg_ref, wuq_ref, wuk_ref,
                   wqi_ref, lng_ref, lnb_ref,
                   qlat_ref, qidx_ref, ckv_o_ref, ckvt_o_ref, kidx_o_ref, wt_o_ref):
    tm = cq_ref.shape[0]
    nb = tm // Q_BLOCK

    cq = cq_ref[...]
    cq = cq * lax.rsqrt(jnp.mean(cq * cq, axis=-1, keepdims=True) + EPS) * qg_ref[...]
    cqb = cq.astype(BF16)

    q = _dot(cqb, wuq_ref[...]).astype(BF16)
    for h in range(A_HEADS):
        ql = _dot(q[:, h * A_HEAD_DIM:(h + 1) * A_HEAD_DIM], wuk_ref[h])
        ql = ql * (A_HEAD_DIM ** -0.5 * LOG2E)
        qlat_ref[:, h] = ql.astype(BF16).reshape(nb, Q_BLOCK, KV_LORA)

    qi = _dot(cqb, wqi_ref[...]).astype(BF16)
    for h in range(IDX_HEADS):
        qidx_ref[:, h] = qi[:, h * IDX_DIM:(h + 1) * IDX_DIM].reshape(nb, Q_BLOCK, IDX_DIM)

    ckv = ckv_ref[...]
    ckv = ckv * lax.rsqrt(jnp.mean(ckv * ckv, axis=-1, keepdims=True) + EPS) * kvg_ref[...]
    ckv_o_ref[...] = ckv.astype(BF16)
    ckvt_o_ref[0, :KV_LORA] = ckv.T.astype(BF16)
    ckvt_o_ref[0, KV_LORA:] = jnp.ones((BF16_ROWS, tm), BF16)

    kidx_o_ref[...] = _layer_norm(kidx_ref[...], lng_ref[...], lnb_ref[...]).astype(BF16)

    wt = small_ref[...].T
    wt_o_ref[...] = wt[SMALL_W:SMALL_W + IDX_HEADS, :] * (IDX_HEADS ** -0.5 * IDX_DIM ** -0.5)


def _dsa_prep(proj, qg, kvg, wuq, wuk, wqi, lng, lnb, *, tm=S_TILE):
    t = proj.shape[0]
    nq = t // Q_BLOCK
    full = lambda *shape: pl.BlockSpec(shape, lambda i: (0,) * len(shape))
    return pl.pallas_call(
        _dsa_prep_body,
        out_shape=(
            jax.ShapeDtypeStruct((nq, A_HEADS, Q_BLOCK, KV_LORA), BF16),
            jax.ShapeDtypeStruct((nq, IDX_HEADS, Q_BLOCK, IDX_DIM), BF16),
            jax.ShapeDtypeStruct((t, KV_LORA), BF16),
            jax.ShapeDtypeStruct((t // tm, KV_LORA + BF16_ROWS, tm), BF16),
            jax.ShapeDtypeStruct((t, IDX_DIM), BF16),
            jax.ShapeDtypeStruct((IDX_HEADS, t), F32),
        ),
        grid=(t // tm,),
        in_specs=[
            pl.BlockSpec((tm, Q_LORA), lambda i: (i, COL_CQ // Q_LORA)),
            pl.BlockSpec((tm, KV_LORA), lambda i: (i, COL_CKV // KV_LORA)),
            pl.BlockSpec((tm, IDX_DIM), lambda i: (i, COL_KIDX // IDX_DIM)),
            pl.BlockSpec((tm, LANES), lambda i: (i, COL_SMALL // LANES)),
            full(1, Q_LORA), full(1, KV_LORA),
            full(Q_LORA, A_HEADS * A_HEAD_DIM),
            full(A_HEADS, A_HEAD_DIM, KV_LORA),
            full(Q_LORA, IDX_HEADS * IDX_DIM),
            full(1, IDX_DIM), full(1, IDX_DIM),
        ],
        out_specs=(
            pl.BlockSpec((tm // Q_BLOCK, A_HEADS, Q_BLOCK, KV_LORA), lambda i: (i, 0, 0, 0)),
            pl.BlockSpec((tm // Q_BLOCK, IDX_HEADS, Q_BLOCK, IDX_DIM), lambda i: (i, 0, 0, 0)),
            pl.BlockSpec((tm, KV_LORA), lambda i: (i, 0)),
            pl.BlockSpec((1, KV_LORA + BF16_ROWS, tm), lambda i: (i, 0, 0)),
            pl.BlockSpec((tm, IDX_DIM), lambda i: (i, 0)),
            pl.BlockSpec((IDX_HEADS, tm), lambda i: (0, i)),
        ),
        compiler_params=_params("parallel"),
        name="dsa_prep",
    )(proj, proj, proj, proj, qg, kvg, wuq, wuk, wqi, lng, lnb)


def _key_to_f32(key):
    return pltpu.bitcast(key ^ ((key >> 31) & 0x7FFFFFFF), F32)


def _dsa_attn_body(qidx_ref, qlat_ref, wt_ref, kidx_ref, ckv_ref, ckvt_ref, wuv_ref, o_ref,
                   sc_ref, sh_ref, bias_ref, m_ref, l_ref, acc_ref, *, n_sel):
    i = pl.program_id(1)
    t0 = i * Q_BLOCK
    n_tiles = (t0 + Q_BLOCK + S_TILE - 1) // S_TILE

    t_lane = t0 + lax.broadcasted_iota(jnp.int32, (S_TILE, Q_BLOCK), 1)
    s_row = lax.broadcasted_iota(jnp.int32, (S_TILE, Q_BLOCK), 0)

    def tile_rows(j):
        return pl.ds(pl.multiple_of(j * S_TILE, S_TILE), S_TILE)

    wt = wt_ref[...]

    def score_tile(j, carry):
        k_tile = kidx_ref[tile_rows(j), :]
        sc = jnp.zeros((S_TILE, Q_BLOCK), F32)
        for h in range(IDX_HEADS):
            logits = _dot_nt(k_tile, qidx_ref[0, h])
            sc = sc + wt[h:h + 1, :] * jnp.maximum(logits, 0.0)
        sc = jnp.where(s_row + j * S_TILE <= t_lane, sc, -jnp.inf)
        sc_ref[tile_rows(j), :] = sc
        sh_ref[tile_rows(j), :] = sc.astype(BF16)
        return carry

    lax.fori_loop(0, n_tiles, score_tile, 0)

    def tree_sum(parts, chains=8):
        sums = list(parts[:chains])
        for n, piece in enumerate(parts[chains:]):
            sums[n % chains] = sums[n % chains] + piece
        while len(sums) > 1:
            sums = [a + b for a, b in zip(sums[0::2], sums[1::2])]
        return sums[0]

    def count(pred):
        def body(j, acc):
            hit = jnp.where(pred(sc_ref[tile_rows(j), :]), 1, 0)
            return acc + tree_sum([hit[r:r + SUBLANES, :] for r in range(0, S_TILE, SUBLANES)])

        acc = lax.fori_loop(0, n_tiles, body, jnp.zeros((SUBLANES, Q_BLOCK), jnp.int32))
        return jnp.sum(acc, axis=0, keepdims=True)

    one16, zero16 = jnp.ones((), BF16), jnp.zeros((), BF16)

    def count_rounded_ge(key16):
        cand = _key_to_f32((key16 << 16) | ((key16 >> 31) & 0xFFFF)).astype(BF16)

        def body(j, acc):
            hit = jnp.where(sh_ref[tile_rows(j), :] >= cand, one16, zero16)
            return acc + tree_sum([hit[r:r + BF16_ROWS, :] for r in range(0, S_TILE, BF16_ROWS)])

        acc = lax.fori_loop(0, n_tiles, body, jnp.zeros((BF16_ROWS, Q_BLOCK), BF16))
        return jnp.sum(acc.astype(F32), axis=0, keepdims=True).astype(jnp.int32)

    def try_key16(cand16, key16):
        return jnp.where(count_rounded_ge(cand16) >= n_sel, cand16, key16)

    zero = jnp.zeros((1, Q_BLOCK), jnp.int32)
    key16 = try_key16(zero, zero - 2 ** 15)
    key16 = lax.fori_loop(0, 15, lambda b, k: try_key16(k | (1 << (14 - b)), k), key16)

    coarse = (key16 << 16) | ((key16 >> 31) & 0xFFFF)
    base = coarse - 2 ** 15

    def try_offset(cand_off, off, key, n_ge):
        cand_key = base + cand_off
        cand = _key_to_f32(cand_key)
        n = count(lambda sc: sc >= cand)
        ok = n >= n_sel
        return jnp.where(ok, cand_off, off), jnp.where(ok, cand_key, key), jnp.where(ok, n, n_ge)

    carry = try_offset(zero, zero, zero + INT_MIN, zero)
    _, key, n_ge = lax.fori_loop(0, 17, lambda b, c: try_offset(c[0] | (1 << (16 - b)), *c), carry)
    thr = _key_to_f32(key)
    thr = jnp.maximum(jnp.where(thr != thr, F32_LOWEST, thr), F32_LOWEST)

    @pl.when(jnp.max(n_ge) > n_sel)
    def _():
        need = (n_sel - count(lambda sc: sc > thr)).astype(F32)
        r = lax.broadcasted_iota(jnp.int32, (S_TILE, S_TILE), 0)
        c = lax.broadcasted_iota(jnp.int32, (S_TILE, S_TILE), 1)
        lower_incl = jnp.where(c <= r, 1.0, 0.0).astype(BF16)

        def body(j, run):
            sc = sc_ref[tile_rows(j), :]
            tied = sc == thr
            rank = run + _dot(lower_incl, jnp.where(tied, 1.0, 0.0).astype(BF16))
            sc_ref[tile_rows(j), :] = jnp.where(tied & (rank > need), -jnp.inf, sc)
            return rank[S_TILE - 1:S_TILE, :]

        lax.fori_loop(0, n_tiles, body, jnp.zeros((1, Q_BLOCK), F32))

    heads = range(A_HEADS)
    slopes = [2.0 ** (-8.0 * (h + 1) / A_HEADS) * LOG2E for h in heads]

    @pl.when(i == 0)
    def _():
        s_rel = lax.broadcasted_iota(jnp.int32, (S_TILE, LANES), 0).astype(F32)
        for h in heads:
            bias_ref[h] = slopes[h] * s_rel

    m_ref[...] = jnp.full(m_ref.shape, -1e30, F32)
    l_ref[...] = jnp.zeros_like(l_ref)
    acc_ref[...] = jnp.zeros_like(acc_ref)

    def att_tile(j, carry):
        drop = jnp.where(sc_ref[tile_rows(j), :] >= thr, 0.0, -jnp.inf)
        ckv_tile = ckv_ref[tile_rows(j), :]
        ckvt_tile = ckvt_ref[j]
        off = (j * S_TILE - t0).astype(F32)
        logits = [_dot_nt(ckv_tile, qlat_ref[0, h]) for h in heads]
        probs, alphas = [], []
        for h in heads:
            bias = jnp.concatenate([bias_ref[h]] * (Q_BLOCK // LANES), axis=1)
            a = logits[h] + bias + drop
            shift = slopes[h] * off
            m_old = m_ref[h:h + 1, :]
            m_new = jnp.maximum(m_old, jnp.max(a, axis=0, keepdims=True) + shift)
            alphas.append(jnp.exp2(m_old - m_new))
            probs.append(jnp.exp2(a - (m_new - shift)).astype(BF16))
            m_ref[h:h + 1, :] = m_new
        for h in heads:
            product = _dot(ckvt_tile, probs[h])
            acc_ref[h] = alphas[h] * acc_ref[h] + product[:KV_LORA]
            l_ref[h:h + 1, :] = alphas[h] * l_ref[h:h + 1, :] + product[KV_LORA:KV_LORA + 1]
        return carry

    lax.fori_loop(0, n_tiles, att_tile, 0)

    for h in heads:
        o_lat = (acc_ref[h] * (1.0 / l_ref[h:h + 1, :])).T.astype(BF16)
        o_ref[:, h * A_HEAD_DIM:(h + 1) * A_HEAD_DIM] = _dot(o_lat, wuv_ref[h]).astype(BF16)


def _dsa_attn(qlat, qidx, wt, kidx, ckv, ckvt, wuv, *, bsz, seq):
    nq = seq // Q_BLOCK
    ns = seq // S_TILE
    n_sel = min(TOPK_MAX, seq // 4)
    return pl.pallas_call(
        functools.partial(_dsa_attn_body, n_sel=n_sel),
        out_shape=jax.ShapeDtypeStruct((bsz * seq, A_HEADS * A_HEAD_DIM), BF16),
        grid=(bsz, nq),
        in_specs=[
            pl.BlockSpec((1, IDX_HEADS, Q_BLOCK, IDX_DIM), lambda b, i: (b * nq + i, 0, 0, 0)),
            pl.BlockSpec((1, A_HEADS, Q_BLOCK, KV_LORA), lambda b, i: (b * nq + i, 0, 0, 0)),
            pl.BlockSpec((IDX_HEADS, Q_BLOCK), lambda b, i: (0, b * nq + i)),
            pl.BlockSpec((seq, IDX_DIM), lambda b, i: (b, 0)),
            pl.BlockSpec((seq, KV_LORA), lambda b, i: (b, 0)),
            pl.BlockSpec((ns, KV_LORA + BF16_ROWS, S_TILE), lambda b, i: (b, 0, 0)),
            pl.BlockSpec((A_HEADS, KV_LORA, A_HEAD_DIM), lambda b, i: (0, 0, 0)),
        ],
        out_specs=pl.BlockSpec((Q_BLOCK, A_HEADS * A_HEAD_DIM), lambda b, i: (b * nq + i, 0)),
        scratch_shapes=[
            pltpu.VMEM((seq, Q_BLOCK), F32),
            pltpu.VMEM((seq, Q_BLOCK), BF16),
            pltpu.VMEM((A_HEADS, S_TILE, LANES), F32),
            pltpu.VMEM((A_HEADS, Q_BLOCK), F32),
            pltpu.VMEM((A_HEADS, Q_BLOCK), F32),
            pltpu.VMEM((A_HEADS, KV_LORA, Q_BLOCK), F32),
        ],
        compiler_params=_params("parallel", "arbitrary"),
        name="dsa_attn",
    )(qidx, qlat, wt, kidx, ckv, ckvt, wuv)


def _gdn_prep_body(x_ref, halo_ref, small_ref, cw_ref, alog_ref, dtb_ref,
                   q_ref, k_ref, v_ref, gcol_ref, grow_ref):
    ts = x_ref.shape[0]
    first = pl.program_id(1) == 0
    x = x_ref[...]
    xb = x.astype(BF16)
    halo = jnp.where(first, 0.0, halo_ref[...]).astype(BF16).astype(F32)
    top = jnp.concatenate([halo, xb[:SUBLANES].astype(F32)], axis=0)
    r = lax.broadcasted_iota(jnp.int32, (ts, ts), 0)
    c = lax.broadcasted_iota(jnp.int32, (ts, ts), 1)
    cw = cw_ref[...]
    y = cw[CONV_WIDTH - 1:CONV_WIDTH, :] * x
    for d in range(1, CONV_WIDTH):
        moved = _dot(jnp.where(c == r - d, 1.0, 0.0).astype(BF16), xb)
        moved = jnp.concatenate([top[SUBLANES - d:2 * SUBLANES - d], moved[SUBLANES:]], axis=0)
        y = y + cw[CONV_WIDTH - 1 - d:CONV_WIDTH - d, :] * moved
    y = y * _sigmoid(y)

    for h in range(B_HEADS):
        lo, hi = h * B_HEAD_DIM, (h + 1) * B_HEAD_DIM
        qh = y[:, lo:hi]
        q_ref[:, lo:hi] = qh * lax.rsqrt(jnp.sum(qh * qh, axis=-1, keepdims=True) + EPS) * (
            B_HEAD_DIM ** -0.5)
        kh = y[:, B_WIDTH + lo:B_WIDTH + hi]
        k_ref[:, lo:hi] = kh * lax.rsqrt(jnp.sum(kh * kh, axis=-1, keepdims=True) + EPS)
    v_ref[...] = y[:, 2 * B_WIDTH:]

    small = small_ref[...]
    pre = small + dtb_ref[...]
    softplus = jnp.maximum(pre, 0.0) + jnp.log1p(jnp.exp(-jnp.abs(pre)))
    g = -jnp.exp(alog_ref[...]) * softplus
    row = lax.broadcasted_iota(jnp.int32, g.shape, 0) & (GDN_CHUNK - 1)
    shift = 1
    while shift < GDN_CHUNK:
        g = g + jnp.where(row >= shift, pltpu.roll(g, shift, 0), 0.0)
        shift *= 2
    lane = lax.broadcasted_iota(jnp.int32, g.shape, 1)
    is_a = (lane >= SMALL_A) & (lane < SMALL_A + B_HEADS)
    gb = jnp.where(is_a, g, _sigmoid(small))
    gcol_ref[...] = gb
    grow_ref[0] = gb.T


def _gdn_prep(proj, conv_w, alog_p, dtb_p, *, bsz, seq, ts=256):
    t = bsz * seq
    ns = seq // ts
    c = 3 * B_WIDTH
    return pl.pallas_call(
        _gdn_prep_body,
        out_shape=(
            jax.ShapeDtypeStruct((t, B_WIDTH), F32),
            jax.ShapeDtypeStruct((t, B_WIDTH), F32),
            jax.ShapeDtypeStruct((t, B_WIDTH), F32),
            jax.ShapeDtypeStruct((t, LANES), F32),
            jax.ShapeDtypeStruct((bsz, LANES, seq), F32),
        ),
        grid=(bsz, ns),
        in_specs=[
            pl.BlockSpec((ts, c), lambda b, i: (b * ns + i, COL_QKV // c)),
            pl.BlockSpec((SUBLANES, c),
                         lambda b, i: (jnp.maximum((b * ns + i) * (ts // SUBLANES) - 1, 0), 0)),
            pl.BlockSpec((ts, LANES), lambda b, i: (b * ns + i, COL_SMALL // LANES)),
            pl.BlockSpec((CONV_WIDTH, c), lambda b, i: (0, 0)),
            pl.BlockSpec((1, LANES), lambda b, i: (0, 0)),
            pl.BlockSpec((1, LANES), lambda b, i: (0, 0)),
        ],
        out_specs=(
            pl.BlockSpec((ts, B_WIDTH), lambda b, i: (b * ns + i, 0)),
            pl.BlockSpec((ts, B_WIDTH), lambda b, i: (b * ns + i, 0)),
            pl.BlockSpec((ts, B_WIDTH), lambda b, i: (b * ns + i, 0)),
            pl.BlockSpec((ts, LANES), lambda b, i: (b * ns + i, 0)),
            pl.BlockSpec((1, LANES, ts), lambda b, i: (b, 0, i)),
        ),
        compiler_params=_params("parallel", "arbitrary"),
        name="gdn_prep",
    )(proj, proj, proj, conv_w, alog_p, dtb_p)


def _dot1(a, b):
    return _dot(a.astype(BF16), b.astype(BF16))


def _each(fn, *lists):
    return [fn(*args) for args in zip(*lists)]


def _unit_lower_inverses(mats, row, col):
    base_log2 = 4
    eye = jnp.where(row == col, 1.0, 0.0)
    in_block = (row >> base_log2) == (col >> base_log2)
    d = [jnp.where(in_block, a, 0.0) for a in mats]
    x = [eye - dh for dh in d]
    p = _each(_dot1, d, d)
    for step in range(base_log2 - 1):
        x = _each(lambda xh, th: xh + th, x, _each(_dot1, x, p))
        if step < base_log2 - 2:
            p = _each(_dot1, p, p)
    sh = base_log2
    while (1 << sh) < GDN_CHUNK:
        same_parent = (row >> (sh + 1)) == (col >> (sh + 1))
        same_block = (row >> sh) == (col >> sh)
        quad = [jnp.where(same_block, 0.0, jnp.where(same_parent, a, 0.0)) for a in mats]
        x = _each(lambda xh, th: xh - th, x, _each(_dot1, x, _each(_dot1, quad, x)))
        sh += 1
    return x


def _gdn_chunk_body(q_ref, k_ref, v_ref, z_ref, gcol_ref, grow_ref, ng_ref, o_ref, state_ref):
    @pl.when(pl.program_id(0) == 0)
    def _():
        state_ref[...] = jnp.zeros_like(state_ref)

    c = GDN_CHUNK
    bsz = q_ref.shape[0]
    units = [(b, h) for b in range(bsz) for h in range(B_HEADS)]
    span = lambda h: slice(h * B_HEAD_DIM, (h + 1) * B_HEAD_DIM)
    row = lax.broadcasted_iota(jnp.int32, (c, c), 0)
    col = lax.broadcasted_iota(jnp.int32, (c, c), 1)
    gcol = [gcol_ref[b] for b in range(bsz)]
    grow = [grow_ref[b] for b in range(bsz)]
    e_cum = [jnp.exp(g) for g in gcol]
    e_rest = [jnp.exp(g[c - 1:c, :] - g) for g in gcol]
    e_last = [jnp.exp(g[:, c - 1:c]) for g in grow]
    col_a = lambda tab, b, h: tab[b][:, SMALL_A + h:SMALL_A + h + 1]

    q = [q_ref[b, :, span(h)] for b, h in units]
    k = [k_ref[b, :, span(h)] for b, h in units]
    v = [v_ref[b, :, span(h)] for b, h in units]
    beta = [gcol[b][:, SMALL_B + h:SMALL_B + h + 1] for b, h in units]
    decay = [jnp.exp(jnp.where(row >= col, col_a(gcol, b, h) - grow[b][SMALL_A + h:SMALL_A + h + 1, :],
                               -jnp.inf)) for b, h in units]
    k_beta = _each(lambda kh, bh: kh * bh, k, beta)
    k16 = [kh.astype(BF16) for kh in k]
    kk = _each(_dot_nt, [kb.astype(BF16) for kb in k_beta], k16)
    qk = _each(_dot_nt, [qh.astype(BF16) for qh in q], k16)
    a_mat = _each(lambda m, dh: jnp.where(row > col, m * dh, 0.0), kk, decay)
    attn = _each(lambda m, dh: (m * dh).astype(BF16), qk, decay)
    t_inv = _unit_lower_inverses(a_mat, row, col)
    rhs = [jnp.concatenate([v[u] * beta[u], k_beta[u] * col_a(e_cum, b, h)], axis=1)
           for u, (b, h) in enumerate(units)]
    sol = _each(_dot1, t_inv, rhs)
    q_dec = [(q[u] * col_a(e_cum, b, h)).astype(BF16) for u, (b, h) in enumerate(units)]
    k_dec_t = [(k[u] * col_a(e_rest, b, h)).T.astype(BF16) for u, (b, h) in enumerate(units)]

    state = [state_ref[u] for u in range(len(units))]
    s16 = [s.astype(BF16) for s in state]
    v_new = [sol[u][:, :B_HEAD_DIM] - _dot(sol[u][:, B_HEAD_DIM:].astype(BF16), s16[u])
             for u in range(len(units))]
    v16 = [x.astype(BF16) for x in v_new]
    for u, (b, h) in enumerate(units):
        state_ref[u] = state[u] * e_last[b][SMALL_A + h:SMALL_A + h + 1, :] + _dot(k_dec_t[u], v16[u])
    out = [_dot(q_dec[u], s16[u]) + _dot(attn[u], v16[u]) for u in range(len(units))]
    for u, (b, h) in enumerate(units):
        o = out[u]
        y = o * lax.rsqrt(jnp.mean(o * o, axis=-1, keepdims=True) + EPS) * ng_ref[...]
        zh = z_ref[b, :, span(h)]
        o_ref[b, :, span(h)] = (y * (zh * _sigmoid(zh))).astype(BF16)


def _gdn_chunk(q, k, v, proj, gcol, grow, ng, *, bsz, seq):
    c = GDN_CHUNK
    per_seq = lambda a: a.reshape(bsz, seq, a.shape[-1])
    tok = lambda i: (0, i, 0)
    out = pl.pallas_call(
        _gdn_chunk_body,
        out_shape=jax.ShapeDtypeStruct((bsz, seq, B_WIDTH), BF16),
        grid=(seq // c,),
        in_specs=[
            pl.BlockSpec((bsz, c, B_WIDTH), tok),
            pl.BlockSpec((bsz, c, B_WIDTH), tok),
            pl.BlockSpec((bsz, c, B_WIDTH), tok),
            pl.BlockSpec((bsz, c, B_WIDTH), lambda i: (0, i, COL_Z // B_WIDTH)),
            pl.BlockSpec((bsz, c, LANES), tok),
            pl.BlockSpec((bsz, LANES, c), lambda i: (0, 0, i)),
            pl.BlockSpec((1, B_HEAD_DIM), lambda i: (0, 0)),
        ],
        out_specs=pl.BlockSpec((bsz, c, B_WIDTH), tok),
        scratch_shapes=[pltpu.VMEM((bsz * B_HEADS, B_HEAD_DIM, B_HEAD_DIM), F32)],
        compiler_params=_params("arbitrary"),
        name="gdn_chunk",
    )(per_seq(q), per_seq(k), per_seq(v), per_seq(proj), per_seq(gcol), grow, ng)
    return out.reshape(bsz * seq, B_WIDTH)


def _out_ln_body(h_ref, oa_ref, ob_ref, wa_ref, wb_ref, g_ref, b_ref, o_ref, *, sub):
    spans = [slice(r, r + sub) for r in range(0, h_ref.shape[0], sub)]
    mixes = [_dot(oa_ref[s, :], wa_ref[...]) + _dot(ob_ref[s, :], wb_ref[...]) for s in spans]
    for s, mix in zip(spans, mixes):
        o_ref[s, :] = _layer_norm(DEEPNORM_ALPHA * h_ref[s, :] + mix, g_ref[...], b_ref[...])


def _out_ln(h, oa, ob, w, g, b, *, tm=1024, sub=256):
    t, d = h.shape
    da, db = oa.shape[1], ob.shape[1]
    assert da == db and w.shape == (da + db, d)
    return pl.pallas_call(
        functools.partial(_out_ln_body, sub=sub),
        out_shape=jax.ShapeDtypeStruct((t, d), F32),
        grid=(t // tm,),
        in_specs=[
            pl.BlockSpec((tm, d), lambda i: (i, 0)),
            pl.BlockSpec((tm, da), lambda i: (i, 0)),
            pl.BlockSpec((tm, db), lambda i: (i, 0)),
            pl.BlockSpec((da, d), lambda i: (0, 0), pipeline_mode=pl.Buffered(1)),
            pl.BlockSpec((db, d), lambda i: (1, 0), pipeline_mode=pl.Buffered(1)),
            pl.BlockSpec((1, d), lambda i: (0, 0)),
            pl.BlockSpec((1, d), lambda i: (0, 0)),
        ],
        out_specs=pl.BlockSpec((tm, d), lambda i: (i, 0)),
        compiler_params=_params("parallel"),
        name="out_ln",
    )(h, oa, ob, w, w, g, b)


def _regroup_w_in(w_in):
    cuts = [0, Q_LORA, KV_LORA, IDX_DIM, IDX_HEADS, B_WIDTH, B_WIDTH, B_WIDTH, B_WIDTH, B_HEADS, B_HEADS]
    offs = [sum(cuts[:n + 1]) for n in range(len(cuts))]
    w16 = w_in.astype(BF16)
    part = lambda lo, hi: w16[:, offs[lo]:offs[hi]]
    pad = jnp.zeros((w_in.shape[0], LANES - IDX_HEADS - 2 * B_HEADS), BF16)
    small = jnp.concatenate([part(0, 3), part(3, 4), part(8, 10), pad], axis=1)
    return part(4, 8), small


def _lane_pad(vec, offset):
    out = jnp.zeros((1, LANES), F32)
    return out.at[0, offset:offset + vec.shape[0]].set(vec.astype(F32))


def kernel(x, ffn1_w_gate, ffn1_w_up, ffn1_w_down, ln1_g, ln1_b, w_in, q_norm_g, kv_norm_g, w_uq, w_uk, w_uv, w_q_idx, k_idx_ln_g, k_idx_ln_b, conv_w, a_log, dt_bias, o_norm_g, w_out, ln2_g, ln2_b, ffn2_w_gate, ffn2_w_up, ffn2_w_down, ln3_g, ln3_b):
    bsz, seq, d = x.shape
    assert seq % S_TILE == 0 and seq % GDN_CHUNK == 0 and x.dtype == F32
    assert seq // BF16_ROWS <= 256, "packed bf16 hit counts must stay exactly representable"
    t = bsz * seq
    row = lambda p: p.reshape(1, -1).astype(F32)
    h = x.reshape(t, d)
    for l in range(DEPTH):
        h = _ffn_ln(h, ffn1_w_gate[l], ffn1_w_up[l], ffn1_w_down[l], row(ln1_g[l]), row(ln1_b[l]))
        proj = _in_proj(h, _regroup_w_in(w_in[l]))
        qlat, qidx, ckv, ckvt, kidx, wt = _dsa_prep(
            proj, row(q_norm_g[l]), row(kv_norm_g[l]),
            w_uq[l].reshape(Q_LORA, A_HEADS * A_HEAD_DIM).astype(BF16), w_uk[l].astype(BF16),
            w_q_idx[l].astype(BF16), row(k_idx_ln_g[l]), row(k_idx_ln_b[l]))
        o_a = _dsa_attn(qlat, qidx, wt, kidx, ckv, ckvt, w_uv[l].astype(BF16), bsz=bsz, seq=seq)
        gq, gk, gv, gcol, grow = _gdn_prep(proj, conv_w[l].astype(F32), _lane_pad(a_log[l], SMALL_A),
                                           _lane_pad(dt_bias[l], SMALL_A), bsz=bsz, seq=seq)
        o_b = _gdn_chunk(gq, gk, gv, proj, gcol, grow, row(o_norm_g[l]), bsz=bsz, seq=seq)
        h = _out_ln(h, o_a, o_b, w_out[l].astype(BF16), row(ln2_g[l]), row(ln2_b[l]))
        h = _ffn_ln(h, ffn2_w_gate[l], ffn2_w_up[l], ffn2_w_down[l], row(ln3_g[l]), row(ln3_b[l]))
    return h.reshape(bsz, seq, d)
```
